```python
import math
import jax
import jax.numpy as jnp
from jax import lax
import numpy as np

D_MODEL = 1024
BATCH = 2
SEQ = 16384
DEPTH = 2

D_MIX = D_MODEL
POOL_GROUPS = 4
POOL_GROUP_DIM = 64
POOL_WIDTH = POOL_GROUPS * POOL_GROUP_DIM
POOL_WINDOWS = (2, 4, 8, 16)
DN_HEADS = 4
DN_HEAD_DIM = 128
DN_WIDTH = DN_HEADS * DN_HEAD_DIM
DN_CONV = 4
DN_CHUNK = 64
MLA_HEADS = 4
MLA_NOPE = 64
MLA_ROPE = 32
MLA_QK_DIM = MLA_NOPE + MLA_ROPE
MLA_V = 64
MLA_WIDTH = MLA_HEADS * MLA_V
Q_LORA = 256
KV_LORA = 128
ROPE_THETA = 10000.0
Q_BLOCK = 128
D_FF = 4 * D_MODEL
EPS = 1e-6
IN_SPLITS = (POOL_WIDTH, 3 * DN_WIDTH, DN_WIDTH, DN_HEADS, DN_HEADS, Q_LORA, KV_LORA, MLA_ROPE)
D_IN = sum(IN_SPLITS)

kernel_name = "hybrid_pool_deltanet_mla_block"


def rms_norm(x, gain):
    xf = x.astype(jnp.float32)
    xf = xf * lax.rsqrt(jnp.mean(xf * xf, axis=-1, keepdims=True) + EPS)
    return (xf * gain.astype(jnp.float32)).astype(x.dtype)


def l2_normalize(t):
    return t * lax.rsqrt(jnp.sum(t * t, axis=-1, keepdims=True) + EPS)


def pool_mixer(xa, w_pool, pool_scale):
    B, S, _ = xa.shape
    xg = xa.astype(jnp.float32).reshape(B, S, POOL_GROUPS, POOL_GROUP_DIM)
    csum = jnp.cumsum(xg, axis=1)
    t = jnp.arange(S)
    pooled = []
    for g, w in enumerate(POOL_WINDOWS):
        cg = csum[:, :, g]
        prev = jnp.pad(cg, ((0, 0), (w, 0), (0, 0)))[:, :S]
        count = jnp.minimum(t + 1, w).astype(jnp.float32)[None, :, None]
        pooled.append((cg - prev) / count)
    y = jnp.stack(pooled, axis=2) - xg
    y = jnp.einsum("bsgc,gcd->bsgd", y, w_pool.astype(jnp.float32))
    y = y.reshape(B, S, POOL_WIDTH) * pool_scale.astype(jnp.float32)
    return y.astype(xa.dtype)


def causal_dwconv(x, w):
    K = w.shape[0]
    return lax.conv_general_dilated(
        x, w[:, None, :], window_strides=(1,), padding=((K - 1, 0),),
        dimension_numbers=("NWC", "WIO", "NWC"), feature_group_count=x.shape[-1])


def gated_deltanet(qkv, z, b, a, conv_w, a_log, dt_bias, norm_gain):
    B, S, _ = qkv.shape
    H, D, C = DN_HEADS, DN_HEAD_DIM, DN_CHUNK
    N = S // C
    f32 = jnp.float32
    qkv = jax.nn.silu(causal_dwconv(qkv.astype(f32), conv_w.astype(f32)))
    q, k, v = jnp.split(qkv, 3, axis=-1)

    def heads(t):
        return t.reshape(B, S, H, D).transpose(0, 2, 1, 3).reshape(B, H, N, C, D)

    q = l2_normalize(heads(q)) * (D ** -0.5)
    k = l2_normalize(heads(k))
    v = heads(v)
    beta = jax.nn.sigmoid(b.astype(f32)).transpose(0, 2, 1).reshape(B, H, N, C)
    g = -jnp.exp(a_log.astype(f32)) * jax.nn.softplus(a.astype(f32) + dt_bias.astype(f32))
    g = g.transpose(0, 2, 1).reshape(B, H, N, C)
    gcum = jnp.cumsum(g, axis=-1)
    causal = jnp.tril(jnp.ones((C, C), bool))
    strict = jnp.tril(jnp.ones((C, C), bool), -1)
    diff = gcum[..., :, None] - gcum[..., None, :]
    decay_mask = jnp.where(causal, jnp.exp(jnp.where(causal, diff, 0.0)), 0.0)
    k_beta = k * beta[..., None]
    v_beta = v * beta[..., None]
    a_strict = jnp.where(strict, jnp.einsum("bhnid,bhnjd->bhnij", k_beta, k) * decay_mask, 0.0)
    rhs = jnp.concatenate([v_beta, k_beta * jnp.exp(gcum)[..., None]], axis=-1)
    sol = lax.linalg.triangular_solve(a_strict, rhs, left_side=True, lower=True, unit_diagonal=True)
    u, w = sol[..., :D], sol[..., D:]
    attn_intra = jnp.where(causal, jnp.einsum("bhnid,bhnjd->bhnij", q, k) * decay_mask, 0.0)
    q_dec = q * jnp.exp(gcum)[..., None]
    k_dec = k * jnp.exp(gcum[..., -1:] - gcum)[..., None]
    g_last = jnp.exp(gcum[..., -1])

    def step(state, inp):
        u_i, w_i, qd_i, kd_i, at_i, gl_i = inp
        v_new = u_i - jnp.einsum("bhck,bhkv->bhcv", w_i, state)
        o = jnp.einsum("bhck,bhkv->bhcv", qd_i, state) + jnp.einsum("bhij,bhjv->bhiv", at_i, v_new)
        state = state * gl_i[..., None, None] + jnp.einsum("bhck,bhcv->bhkv", kd_i, v_new)
        return state, o

    xs = tuple(jnp.moveaxis(t, 2, 0) for t in (u, w, q_dec, k_dec, attn_intra, g_last))
    state0 = jnp.zeros((B, H, D, D), f32)
    _, o = lax.scan(step, state0, xs)
    o = o.transpose(1, 0, 3, 2, 4).reshape(B, S, H, D)
    zh = z.astype(f32).reshape(B, S, H, D)
    o = rms_norm(o, norm_gain) * jax.nn.silu(zh)
    return o.reshape(B, S, DN_WIDTH).astype(z.dtype)


def apply_rope(t, cos, sin):
    tf = t.astype(jnp.float32)
    t1, t2 = jnp.split(tf, 2, axis=-1)
    return jnp.concatenate([t1 * cos - t2 * sin, t1 * sin + t2 * cos], axis=-1).astype(t.dtype)


def causal_block_attention(q, k, v):
    B, H, S, Dqk = q.shape
    Dv = v.shape[-1]
    nb = S // Q_BLOCK
    qb = q.reshape(B, H, nb, Q_BLOCK, Dqk).transpose(2, 0, 1, 3, 4)
    kpos = jnp.arange(S)
    scale = Dqk ** -0.5

    def one_block(args):
        i, q_i = args
        s = jnp.einsum("bhqd,bhkd->bhqk", q_i, k).astype(jnp.float32) * scale
        qpos = i * Q_BLOCK + jnp.arange(Q_BLOCK)
        s = jnp.where(kpos[None, :] <= qpos[:, None], s, -jnp.inf)
        p = jax.nn.softmax(s, axis=-1).astype(v.dtype)
        return jnp.einsum("bhqk,bhkd->bhqd", p, v)

    o = lax.map(one_block, (jnp.arange(nb), qb))
    return o.transpose(1, 2, 0, 3, 4).reshape(B, H, S, Dv)


def mla_attention(c_q, c_kv, k_pe, positions, q_a_norm, w_q_b, kv_a_norm, w_kv_b, q_norm, k_norm):
    B, S, _ = c_q.shape
    H = MLA_HEADS
    q = (rms_norm(c_q, q_a_norm) @ w_q_b).reshape(B, S, H, MLA_QK_DIM)
    kv = (rms_norm(c_kv, kv_a_norm) @ w_kv_b).reshape(B, S, H, MLA_NOPE + MLA_V)
    k_nope, v = kv[..., :MLA_NOPE], kv[..., MLA_NOPE:]
    q_nope = rms_norm(q[..., :MLA_NOPE], q_norm[:MLA_NOPE])
    q_pe = rms_norm(q[..., MLA_NOPE:], q_norm[MLA_NOPE:])
    k_nope = rms_norm(k_nope, k_norm[:MLA_NOPE])
    k_pe = rms_norm(k_pe, k_norm[MLA_NOPE:])[:, :, None, :]
    inv_freq = ROPE_THETA ** (-jnp.arange(0, MLA_ROPE, 2, dtype=jnp.float32) / MLA_ROPE)
    ang = positions.astype(jnp.float32)[..., None] * inv_freq
    cos = jnp.cos(ang)[:, :, None, :]
    sin = jnp.sin(ang)[:, :, None, :]
    q_pe = apply_rope(q_pe, cos, sin)
    k_pe = apply_rope(k_pe, cos, sin)
    q = jnp.concatenate([q_nope, q_pe], axis=-1)
    k = jnp.concatenate([k_nope, jnp.broadcast_to(k_pe, (B, S, H, MLA_ROPE))], axis=-1)
    o = causal_block_attention(q.transpose(0, 2, 1, 3), k.transpose(0, 2, 1, 3), v.transpose(0, 2, 1, 3))
    return o.transpose(0, 2, 1, 3).reshape(B, S, MLA_WIDTH)


def setup_inputs(seed: int = 0) -> dict:
    key = jax.random.key(seed)
    ks = jax.random.split(key, 24)
    f32 = jnp.float32
    L = DEPTH

    def normal(k, shape, scale):
        return jax.random.normal(k, shape, f32) * scale

    def gain(k, shape):
        return 1.0 + 0.02 * jax.random.normal(k, shape, f32)

    x = jax.random.normal(ks[0], (BATCH, SEQ, D_MODEL), f32)
    positions = jnp.tile(jnp.arange(SEQ, dtype=jnp.int32)[None, :], (BATCH, 1))
    dt = jnp.exp(jax.random.uniform(ks[7], (L, DN_HEADS), f32, math.log(1e-3), math.log(1e-1)))
    return {
        "x": x,
        "positions": positions,
        "attn_norm": gain(ks[1], (L, D_MODEL)),
        "w_in": normal(ks[2], (L, D_MODEL, D_IN), D_MODEL ** -0.5),
        "pool_w": normal(ks[3], (L, POOL_GROUPS, POOL_GROUP_DIM, POOL_GROUP_DIM), POOL_GROUP_DIM ** -0.5),
        "pool_scale": gain(ks[4], (L, POOL_WIDTH)),
        "dn_conv": normal(ks[5], (L, DN_CONV, 3 * DN_WIDTH), DN_CONV ** -0.5),
        "dn_a_log": jnp.log(jax.random.uniform(ks[6], (L, DN_HEADS), f32, 1.0, 16.0)),
        "dn_dt_bias": dt + jnp.log(-jnp.expm1(-dt)),
        "dn_norm": gain(ks[8], (L, DN_HEAD_DIM)),
        "mla_q_a_norm": gain(ks[9], (L, Q_LORA)),
        "mla_w_q_b": normal(ks[10], (L, Q_LORA, MLA_HEADS * MLA_QK_DIM), Q_LORA ** -0.5),
        "mla_kv_a_norm": gain(ks[11], (L, KV_LORA)),
        "mla_w_kv_b": normal(ks[12], (L, KV_LORA, MLA_HEADS * (MLA_NOPE + MLA_V)), KV_LORA ** -0.5),
        "mla_q_norm": gain(ks[13], (L, MLA_QK_DIM)),
        "mla_k_norm": gain(ks[14], (L, MLA_QK_DIM)),
        "w_out": normal(ks[15], (L, D_MIX, D_MODEL), D_MIX ** -0.5),
        "mlp_norm": gain(ks[16], (L, D_MODEL)),
        "w_up": normal(ks[17], (L, D_MODEL, D_FF), D_MODEL ** -0.5),
        "w_down": normal(ks[18], (L, D_FF, D_MODEL), D_FF ** -0.5),
    }


def reference(x, positions, attn_norm, w_in, pool_w, pool_scale, dn_conv, dn_a_log, dn_dt_bias, dn_norm,
              mla_q_a_norm, mla_w_q_b, mla_kv_a_norm, mla_w_kv_b, mla_q_norm, mla_k_norm,
              w_out, mlp_norm, w_up, w_down):
    offsets = [int(o) for o in np.cumsum(IN_SPLITS)[:-1]]
    for l in range(DEPTH):
        h = rms_norm(x, attn_norm[l])
        proj = h @ w_in[l]
        xa, qkv, z, b, a, c_q, c_kv, k_pe = jnp.split(proj, offsets, axis=-1)
        y_a = pool_mixer(xa, pool_w[l], pool_scale[l])
        y_b = gated_deltanet(qkv, z, b, a, dn_conv[l], dn_a_log[l], dn_dt_bias[l], dn_norm[l])
        y_c = mla_attention(c_q, c_kv, k_pe, positions, mla_q_a_norm[l], mla_w_q_b[l],
                            mla_kv_a_norm[l], mla_w_kv_b[l], mla_q_norm[l], mla_k_norm[l])
        mixed = jnp.concatenate([y_a, y_b, y_c], axis=-1)
        x = x + mixed @ w_out[l]
        h = rms_norm(x, mlp_norm[l])
        x = x + jnp.square(jax.nn.relu(h @ w_up[l])) @ w_down[l]
    return x
```

```python
import functools
import math

import jax
import jax.numpy as jnp
import numpy as np
from jax import lax
from jax.experimental import pallas as pl
from jax.experimental.pallas import tpu as pltpu

F32 = jnp.float32
BF16 = jnp.bfloat16

D_MODEL = 1024
POOL_GROUPS = 4
POOL_GROUP_DIM = 64
POOL_WIDTH = POOL_GROUPS * POOL_GROUP_DIM
POOL_WINDOWS = (2, 4, 8, 16)
DN_HEADS = 4
DN_HEAD_DIM = 128
DN_WIDTH = DN_HEADS * DN_HEAD_DIM
DN_CONV = 4
DN_CHUNK = 64
MLA_HEADS = 4
MLA_NOPE = 64
MLA_ROPE = 32
MLA_QK_DIM = MLA_NOPE + MLA_ROPE
MLA_V = 64
MLA_WIDTH = MLA_HEADS * MLA_V
Q_LORA = 256
KV_LORA = 128
ROPE_THETA = 10000.0
D_FF = 4 * D_MODEL
EPS = 1e-6
IN_SPLITS = (POOL_WIDTH, 3 * DN_WIDTH, DN_WIDTH, DN_HEADS, DN_HEADS, Q_LORA, KV_LORA, MLA_ROPE)

LANES = 128
SUBLANES = 8
VMEM_LIMIT_BYTES = 56 * 1024 * 1024

TM_PROJ = 512
TM_PREP = 512
DN_TILE = 256
TQ = 512
TK = 512
HALF_ROPE = MLA_ROPE // 2
PE1_LANE = MLA_NOPE
PE2_LANE = MLA_NOPE + 2 * HALF_ROPE


def _dot(a, b):
    return jnp.dot(a, b, preferred_element_type=F32)


def _dot_nt(a, b):
    return lax.dot_general(a, b, (((1,), (1,)), ((), ())), preferred_element_type=F32)


def _rms_rows(x, gain):
    return x * lax.rsqrt(jnp.mean(x * x, axis=-1, keepdims=True) + EPS) * gain


def _split3(x):
    x1 = x.astype(BF16)
    r1 = x - x1.astype(F32)
    x2 = r1.astype(BF16)
    r2 = r1 - x2.astype(F32)
    return x1, x2, r2.astype(BF16)


def _sigmoid(x):
    return 1.0 / (1.0 + jnp.exp(-x))


def _params(*sem):
    return pltpu.CompilerParams(dimension_semantics=sem, vmem_limit_bytes=VMEM_LIMIT_BYTES)


def _const_spec(shape):
    nd = len(shape)
    return pl.BlockSpec(shape, lambda *_: (0,) * nd, pipeline_mode=pl.Buffered(1))


IN_SEGS = (POOL_WIDTH, 3 * DN_WIDTH, DN_WIDTH, Q_LORA, KV_LORA, LANES)


def _inproj_kernel(x_ref, g_ref, w_ref, *o_refs):
    h = _rms_rows(x_ref[...], g_ref[...]).astype(BF16)
    off = 0
    for o_ref in o_refs:
        n = o_ref.shape[-1]
        o_ref[...] = _dot(h, w_ref[:, off:off + n])
        off += n


def _inproj(x2, gain, w_cat):
    t = x2.shape[0]
    n_all = sum(IN_SEGS)
    return pl.pallas_call(
        _inproj_kernel,
        grid=(t // TM_PROJ,),
        in_specs=[pl.BlockSpec((TM_PROJ, D_MODEL), lambda i: (i, 0)),
                  _const_spec((1, D_MODEL)),
                  _const_spec((D_MODEL, n_all))],
        out_specs=[pl.BlockSpec((TM_PROJ, n), lambda i: (i, 0)) for n in IN_SEGS],
        out_shape=[jax.ShapeDtypeStruct((t, n), F32) for n in IN_SEGS],
        compiler_params=_params("parallel"),
        name="inproj",
    )(x2, gain, w_cat)


CONV_HALO = SUBLANES


def _dn_kernel(qkv_ref, z_ref, misc_ref, cw_ref, alog_ref, dtb_ref, ng_ref, y_ref,
               state_sc, halo_sc, ext_sc):
    L = DN_TILE
    C = DN_CHUNK
    nchunk = L // C
    D = DN_HEAD_DIM
    s_idx = pl.program_id(1)

    @pl.when(s_idx == 0)
    def _():
        state_sc[...] = jnp.zeros_like(state_sc)
        halo_sc[...] = jnp.zeros_like(halo_sc)

    ext_sc[0:CONV_HALO, :] = halo_sc[...]
    ext_sc[CONV_HALO:, :] = qkv_ref[0]
    halo_sc[...] = qkv_ref[0, L - CONV_HALO:L, :]

    ri = lax.broadcasted_iota(jnp.int32, (L, L), 0)
    ci = lax.broadcasted_iota(jnp.int32, (L, L), 1)
    same_chunk = (ri >> 6) == (ci >> 6)
    causal_bd = same_chunk & (ci <= ri)
    strict_bd = same_chunk & (ci < ri)
    eye = (ri == ci).astype(F32)

    misc = misc_ref[0]
    beta_all = _sigmoid(misc)
    sp_in = misc + dtb_ref[...]
    softplus = jnp.maximum(sp_in, 0.0) + jnp.log1p(jnp.exp(-jnp.abs(sp_in)))
    g_all = -jnp.exp(alog_ref[...]) * softplus

    cum_mat = jnp.concatenate([causal_bd.astype(BF16), same_chunk.astype(BF16)], axis=0)
    g1, g2, g3 = _split3(g_all)
    cum = _dot(cum_mat, g1) + _dot(cum_mat, g2) + _dot(cum_mat, g3)
    gcum_all = cum[:L]
    glast_all = cum[L:]
    gcum_t = gcum_all.T

    for h in range(DN_HEADS):
        lane = DN_HEADS + h
        gc_col = gcum_all[:, lane:lane + 1]
        gl_col = glast_all[:, lane:lane + 1]
        gc_row = gcum_t[lane:lane + 1, :]
        beta = beta_all[:, h:h + 1]

        def conv_part(p):
            c0 = p * DN_WIDTH + h * D
            e = ext_sc[:, c0:c0 + D]
            cw = cw_ref[:, c0:c0 + D]
            acc = cw[DN_CONV - 1:DN_CONV] * e
            for j in range(1, DN_CONV):
                acc = acc + cw[DN_CONV - 1 - j:DN_CONV - j] * pltpu.roll(e, j, 0)
            acc = acc[CONV_HALO:]
            return acc * _sigmoid(acc)

        q = conv_part(0)
        k = conv_part(1)
        v = conv_part(2)
        qn = q * lax.rsqrt(jnp.sum(q * q, axis=-1, keepdims=True) + EPS) * (D ** -0.5)
        kn = k * lax.rsqrt(jnp.sum(k * k, axis=-1, keepdims=True) + EPS)
        kb = kn * beta
        vb = v * beta
        e_col = jnp.exp(gc_col)
        decay = jnp.exp(jnp.where(causal_bd, gc_col - gc_row, 0.0))
        kn16 = kn.astype(BF16)
        a_mat = jnp.where(strict_bd, _dot_nt(kb.astype(BF16), kn16) * decay, 0.0)
        attn = jnp.where(causal_bd, _dot_nt(qn.astype(BF16), kn16) * decay, 0.0).astype(BF16)

        xp = (-a_mat).astype(BF16)
        t_inv = eye - a_mat
        for _ in range(5):
            x2 = _dot(xp, xp)
            xp = x2.astype(BF16)
            t_inv = t_inv + _dot(t_inv.astype(BF16), xp)

        rhs = jnp.concatenate([vb, kb * e_col], axis=1).astype(BF16)
        uw = _dot(t_inv.astype(BF16), rhs)
        u = uw[:, :D]
        w16 = uw[:, D:].astype(BF16)
        qd16 = (qn * e_col).astype(BF16)
        kd_t16 = (kn * jnp.exp(gl_col - gc_col)).T.astype(BF16)
        gdec = jnp.exp(jnp.broadcast_to(gl_col, (L, D)))

        state = state_sc[h]
        o_parts = []
        zeros_c = jnp.zeros((C, D), BF16)
        for c in range(nchunk):
            r0 = c * C
            s16 = state.astype(BF16)
            wq = jnp.concatenate([w16[r0:r0 + C], qd16[r0:r0 + C]], axis=0)
            r = _dot(wq, s16)
            v_new = (u[r0:r0 + C] - r[:C]).astype(BF16)
            v_full = jnp.concatenate([zeros_c] * c + [v_new] + [zeros_c] * (nchunk - 1 - c), axis=0)
            o_parts.append(r[C:] + _dot(attn[r0:r0 + C, :], v_full))
            state = state * gdec[r0:r0 + 1, :] + _dot(kd_t16, v_full)
        state_sc[h] = state

        o = jnp.concatenate(o_parts, axis=0)
        zh = z_ref[0, :, h * D:(h + 1) * D]
        y_ref[0, :, h * D:(h + 1) * D] = _rms_rows(o, ng_ref[...]) * (zh * _sigmoid(zh))


def _deltanet(qkv, z, misc, conv_w, alog_l, dtb_l, norm_gain):
    b, s, _ = qkv.shape
    L = DN_TILE
    return pl.pallas_call(
        _dn_kernel,
        grid=(b, s // L),
        in_specs=[pl.BlockSpec((1, L, 3 * DN_WIDTH), lambda i, j: (i, j, 0)),
                  pl.BlockSpec((1, L, DN_WIDTH), lambda i, j: (i, j, 0)),
                  pl.BlockSpec((1, L, LANES), lambda i, j: (i, j, 0)),
                  _const_spec((DN_CONV, 3 * DN_WIDTH)),
                  _const_spec((1, LANES)),
                  _const_spec((1, LANES)),
                  _const_spec((1, DN_HEAD_DIM))],
        out_specs=pl.BlockSpec((1, L, DN_WIDTH), lambda i, j: (i, j, 0)),
        out_shape=jax.ShapeDtypeStruct((b, s, DN_WIDTH), F32),
        scratch_shapes=[pltpu.VMEM((DN_HEADS, DN_HEAD_DIM, DN_HEAD_DIM), F32),
                        pltpu.VMEM((CONV_HALO, 3 * DN_WIDTH), F32),
                        pltpu.VMEM((L + CONV_HALO, 3 * DN_WIDTH), F32)],
        compiler_params=_params("parallel", "arbitrary"),
        name="deltanet",
    )(qkv, z, misc, conv_w, alog_l, dtb_l, norm_gain)


def _rope_kernel(pos_ref, freq_ref, cos_ref, sina_ref, sinb_ref):
    tm = pos_ref.shape[1]
    ang = pos_ref[0].astype(F32) * freq_ref[...]
    lane = lax.broadcasted_iota(jnp.int32, (tm, LANES), 1)
    pe1 = (lane >= PE1_LANE) & (lane < PE1_LANE + HALF_ROPE)
    pe2 = (lane >= PE2_LANE) & (lane < PE2_LANE + HALF_ROPE)
    cos = jnp.cos(ang)
    sin = jnp.sin(ang)
    cos_ref[0] = jnp.where(lane < MLA_NOPE, 1.0, jnp.where(pe1 | pe2, cos, 0.0))
    sina_ref[0] = jnp.where(pe2, sin, 0.0)
    sinb_ref[0] = jnp.where(pe1, -sin, 0.0)


def _rope_tables(positions, freq_lanes):
    b, s = positions.shape
    tm = TM_PREP
    spec = pl.BlockSpec((1, tm, LANES), lambda i, j: (i, j, 0))
    return pl.pallas_call(
        _rope_kernel,
        grid=(b, s // tm),
        in_specs=[pl.BlockSpec((1, tm, 1), lambda i, j: (i, j, 0)), _const_spec((1, LANES))],
        out_specs=[spec, spec, spec],
        out_shape=[jax.ShapeDtypeStruct((b, s, LANES), F32)] * 3,
        compiler_params=_params("parallel", "parallel"),
        name="rope_tables",
    )(positions.reshape(b, s, 1), freq_lanes)


def _rope(x, cos, sina, sinb):
    return x * cos + pltpu.roll(x, 2 * HALF_ROPE, 1) * sina + pltpu.roll(x, LANES - 2 * HALF_ROPE, 1) * sinb


def _mla_prep_kernel(cq_ref, ckv_ref, misc_ref, cos_ref, sina_ref, sinb_ref,
                     qag_ref, kvag_ref, wq_ref, wk_ref, wvt_ref, qg_ref, kg_ref, kpg_ref,
                     q_ref, k_ref, vt_ref):
    tm = cq_ref.shape[1]
    cos = cos_ref[0]
    sina = sina_ref[0]
    sinb = sinb_ref[0]
    lane = lax.broadcasted_iota(jnp.int32, (tm, LANES), 1)
    is_nope = lane < MLA_NOPE

    cqn = _rms_rows(cq_ref[0], qag_ref[...]).astype(BF16)
    ckvn = _rms_rows(ckv_ref[0], kvag_ref[...]).astype(BF16)
    qf = _dot(cqn, wq_ref[...])
    kf = _dot(ckvn, wk_ref[...])
    vt_ref[0] = _dot_nt(wvt_ref[...], ckvn).astype(BF16)

    kp = jnp.where(is_nope, 0.0, misc_ref[0])
    kp_ms = jnp.sum(kp * kp, axis=-1, keepdims=True) * (1.0 / MLA_ROPE)
    kp = _rope(kp * lax.rsqrt(kp_ms + EPS) * kpg_ref[...], cos, sina, sinb)

    scale = MLA_QK_DIM ** -0.5
    for h in range(MLA_HEADS):
        xq = qf[:, h * LANES:(h + 1) * LANES]
        sq = xq * xq
        ms_n = jnp.sum(jnp.where(is_nope, sq, 0.0), axis=-1, keepdims=True) * (1.0 / MLA_NOPE)
        ms_p = jnp.sum(jnp.where(is_nope, 0.0, sq), axis=-1, keepdims=True) * (1.0 / MLA_ROPE)
        inv = jnp.where(is_nope, lax.rsqrt(ms_n + EPS), lax.rsqrt(ms_p + EPS))
        qh = _rope(xq * inv * qg_ref[...], cos, sina, sinb) * scale
        q_ref[0, h] = qh.astype(BF16)

        xk = kf[:, h * LANES:(h + 1) * LANES]
        ms_k = jnp.sum(xk * xk, axis=-1, keepdims=True) * (1.0 / MLA_NOPE)
        kh = xk * lax.rsqrt(ms_k + EPS) * kg_ref[...] + kp
        k_ref[0, h] = kh.astype(BF16)


def _mla_prep(cq, ckv, misc, cos, sina, sinb, qag, kvag, wq, wk, wvt, qg, kg, kpg):
    b, s, _ = cq.shape
    tm = TM_PREP
    hl = MLA_HEADS * LANES

    def tok(n):
        return pl.BlockSpec((1, tm, n), lambda i, j: (i, j, 0))

    return pl.pallas_call(
        _mla_prep_kernel,
        grid=(b, s // tm),
        in_specs=[tok(Q_LORA), tok(KV_LORA), tok(LANES), tok(LANES), tok(LANES), tok(LANES),
                  _const_spec((1, Q_LORA)), _const_spec((1, KV_LORA)),
                  _const_spec((Q_LORA, hl)), _const_spec((KV_LORA, hl)), _const_spec((MLA_WIDTH, KV_LORA)),
                  _const_spec((1, LANES)), _const_spec((1, LANES)), _const_spec((1, LANES))],
        out_specs=[pl.BlockSpec((1, MLA_HEADS, tm, LANES), lambda i, j: (i, 0, j, 0)),
                   pl.BlockSpec((1, MLA_HEADS, tm, LANES), lambda i, j: (i, 0, j, 0)),
                   pl.BlockSpec((1, MLA_WIDTH, tm), lambda i, j: (i, 0, j))],
        out_shape=[jax.ShapeDtypeStruct((b, MLA_HEADS, s, LANES), BF16),
                   jax.ShapeDtypeStruct((b, MLA_HEADS, s, LANES), BF16),
                   jax.ShapeDtypeStruct((b, MLA_WIDTH, s), BF16)],
        compiler_params=_params("parallel", "parallel"),
        name="mla_prep",
    )(cq, ckv, misc, cos, sina, sinb, qag, kvag, wq, wk, wvt, qg, kg, kpg)


def _attn_kernel(qi_ref, kj_ref, q_ref, k_ref, vt_ref, o_ref, m_sc, l_sc, acc_sc):
    p = pl.program_id(1)
    qi = qi_ref[p]
    kj = kj_ref[p]

    @pl.when(kj == 0)
    def _():
        m_sc[...] = jnp.full_like(m_sc, -jnp.inf)
        l_sc[...] = jnp.zeros_like(l_sc)
        acc_sc[...] = jnp.zeros_like(acc_sc)

    kpos = kj * TK + lax.broadcasted_iota(jnp.int32, (TK, TQ), 0)
    qpos = qi * TQ + lax.broadcasted_iota(jnp.int32, (TK, TQ), 1)
    visible = kpos <= qpos
    for h in range(MLA_HEADS):
        st = _dot_nt(k_ref[0, h], q_ref[0, h])
        st = jnp.where(visible, st, -jnp.inf)
        m_prev = m_sc[h]
        m_new = jnp.maximum(m_prev, jnp.max(st, axis=0, keepdims=True))
        alpha = jnp.exp(m_prev - m_new)
        pt = jnp.exp(st - m_new)
        l_sc[h] = alpha * l_sc[h] + jnp.sum(pt, axis=0, keepdims=True)
        acc_sc[h] = alpha * acc_sc[h] + _dot(vt_ref[0, h * MLA_V:(h + 1) * MLA_V, :], pt.astype(BF16))
        m_sc[h] = m_new

    @pl.when(kj == qi)
    def _():
        out_t = jnp.concatenate([acc_sc[h] / l_sc[h] for h in range(MLA_HEADS)], axis=0)
        o_ref[0] = out_t.T


def _attention(q, k, vt):
    b, _, s, _ = q.shape
    nq = s // TQ
    pairs = [(i, j) for i in range(nq) for j in range(i * TQ // TK + 1)] if TQ == TK else None
    assert pairs is not None
    qi = jnp.asarray(np.array([p[0] for p in pairs], np.int32))
    kj = jnp.asarray(np.array([p[1] for p in pairs], np.int32))
    grid_spec = pltpu.PrefetchScalarGridSpec(
        num_scalar_prefetch=2,
        grid=(b, len(pairs)),
        in_specs=[pl.BlockSpec((1, MLA_HEADS, TQ, LANES), lambda i, p, qi, kj: (i, 0, qi[p], 0)),
                  pl.BlockSpec((1, MLA_HEADS, TK, LANES), lambda i, p, qi, kj: (i, 0, kj[p], 0)),
                  pl.BlockSpec((1, MLA_WIDTH, TK), lambda i, p, qi, kj: (i, 0, kj[p]))],
        out_specs=pl.BlockSpec((1, TQ, MLA_WIDTH), lambda i, p, qi, kj: (i, qi[p], 0)),
        scratch_shapes=[pltpu.VMEM((MLA_HEADS, 1, TQ), F32),
                        pltpu.VMEM((MLA_HEADS, 1, TQ), F32),
                        pltpu.VMEM((MLA_HEADS, MLA_V, TQ), F32)],
    )
    return pl.pallas_call(
        _attn_kernel,
        grid_spec=grid_spec,
        out_shape=jax.ShapeDtypeStruct((b, s, MLA_WIDTH), F32),
        compiler_params=_params("parallel", "arbitrary"),
        name="mla_attention",
    )(qi, kj, q, k, vt)


POOL_HALO = 16


def _outproj_kernel(x_ref, xa_ref, yb_ref, yc_ref, wp_ref, ps_ref, wo_ref, o_ref, halo_sc, ext_sc):
    ts = x_ref.shape[1]
    s_idx = pl.program_id(1)

    @pl.when(s_idx == 0)
    def _():
        halo_sc[...] = jnp.zeros_like(halo_sc)

    xa = xa_ref[0]
    ext_sc[0:POOL_HALO, :] = halo_sc[...]
    ext_sc[POOL_HALO:, :] = xa
    halo_sc[...] = xa_ref[0, ts - POOL_HALO:ts, :]

    e = ext_sc[...]
    sums = []
    step = 1
    for _ in POOL_WINDOWS:
        e = e + pltpu.roll(e, step, 0)
        step *= 2
        sums.append(e[POOL_HALO:])
    lane = lax.broadcasted_iota(jnp.int32, (ts, POOL_WIDTH), 1)
    grp = lane >> 6
    win = jnp.left_shift(2, grp)
    t = s_idx * ts + lax.broadcasted_iota(jnp.int32, (ts, POOL_WIDTH), 0)
    count = jnp.minimum(t + 1, win).astype(F32)
    pooled = jnp.where(grp == 0, sums[0], jnp.where(grp == 1, sums[1], jnp.where(grp == 2, sums[2], sums[3])))
    ya = _dot((pooled / count - xa).astype(BF16), wp_ref[...]) * ps_ref[...]

    acc = x_ref[0] + _dot(ya.astype(BF16), wo_ref[0:POOL_WIDTH, :])
    acc = acc + _dot(yb_ref[0].astype(BF16), wo_ref[POOL_WIDTH:POOL_WIDTH + DN_WIDTH, :])
    acc = acc + _dot(yc_ref[0].astype(BF16), wo_ref[POOL_WIDTH + DN_WIDTH:, :])
    o_ref[0] = acc


def _outproj(x, xa, yb, yc, wp_bd, pool_scale, w_out):
    b, s, _ = x.shape
    ts = TM_PROJ

    def tok(n):
        return pl.BlockSpec((1, ts, n), lambda i, j: (i, j, 0))

    return pl.pallas_call(
        _outproj_kernel,
        grid=(b, s // ts),
        in_specs=[tok(D_MODEL), tok(POOL_WIDTH), tok(DN_WIDTH), tok(MLA_WIDTH),
                  _const_spec((POOL_WIDTH, POOL_WIDTH)), _const_spec((1, POOL_WIDTH)),
                  _const_spec((D_MODEL, D_MODEL))],
        out_specs=tok(D_MODEL),
        out_shape=jax.ShapeDtypeStruct((b, s, D_MODEL), F32),
        scratch_shapes=[pltpu.VMEM((POOL_HALO, POOL_WIDTH), F32),
                        pltpu.VMEM((ts + POOL_HALO, POOL_WIDTH), F32)],
        compiler_params=_params("parallel", "arbitrary"),
        name="outproj_pool",
    )(x, xa, yb, yc, wp_bd, pool_scale, w_out)


def _mlp_kernel(x_ref, g_ref, wu_ref, wd_ref, o_ref):
    x = x_ref[...]
    h = _rms_rows(x, g_ref[...]).astype(BF16)
    u = jnp.maximum(_dot(h, wu_ref[...]), 0.0)
    o_ref[...] = x + _dot((u * u).astype(BF16), wd_ref[...])


def _mlp(x2, gain, w_up, w_down):
    t = x2.shape[0]
    return pl.pallas_call(
        _mlp_kernel,
        grid=(t // TM_PROJ,),
        in_specs=[pl.BlockSpec((TM_PROJ, D_MODEL), lambda i: (i, 0)),
                  _const_spec((1, D_MODEL)),
                  _const_spec((D_MODEL, D_FF)),
                  _const_spec((D_FF, D_MODEL))],
        out_specs=pl.BlockSpec((TM_PROJ, D_MODEL), lambda i: (i, 0)),
        out_shape=jax.ShapeDtypeStruct((t, D_MODEL), F32),
        compiler_params=_params("parallel"),
        name="mlp",
    )(x2, gain, w_up, w_down)


def _head_block(nope, pe):
    z16 = jnp.zeros(pe.shape[:-1] + (HALF_ROPE,), pe.dtype)
    return jnp.concatenate([nope, pe[..., :HALF_ROPE], z16, pe[..., HALF_ROPE:], z16], axis=-1)


def _prep_layer(w_in, pool_w, pool_scale, dn_conv, dn_a_log, dn_dt_bias, dn_norm,
                q_a_norm, w_q_b, kv_a_norm, w_kv_b, q_norm, k_norm, w_out, w_up, w_down):
    offs = np.cumsum((0,) + IN_SPLITS)
    seg = [w_in[:, offs[i]:offs[i + 1]] for i in range(len(IN_SPLITS))]
    w_xa, w_qkv, w_z, w_b, w_a, w_cq, w_ckv, w_kpe = seg
    d = w_in.shape[0]
    z64 = jnp.zeros((d, MLA_NOPE), w_in.dtype)
    w_misc = jnp.concatenate([w_b, w_a, jnp.zeros((d, MLA_NOPE - 2 * DN_HEADS), w_in.dtype),
                              _head_block(z64, w_kpe)[:, MLA_NOPE:]], axis=1)
    w_cat = jnp.concatenate([w_xa, w_qkv, w_z, w_cq, w_ckv, w_misc], axis=1).astype(BF16)

    lane_pad = jnp.zeros((LANES - 2 * DN_HEADS,), F32)
    alog_l = jnp.concatenate([jnp.zeros((DN_HEADS,), F32), dn_a_log, lane_pad])[None]
    dtb_l = jnp.concatenate([jnp.zeros((DN_HEADS,), F32), dn_dt_bias, lane_pad])[None]

    wq = w_q_b.reshape(Q_LORA, MLA_HEADS, MLA_QK_DIM)
    wq = _head_block(wq[..., :MLA_NOPE], wq[..., MLA_NOPE:]).reshape(Q_LORA, MLA_HEADS * LANES).astype(BF16)
    wkv = w_kv_b.reshape(KV_LORA, MLA_HEADS, MLA_NOPE + MLA_V)
    wk = _head_block(wkv[..., :MLA_NOPE], jnp.zeros((KV_LORA, MLA_HEADS, MLA_ROPE), F32))
    wk = wk.reshape(KV_LORA, MLA_HEADS * LANES).astype(BF16)
    wvt = wkv[..., MLA_NOPE:].reshape(KV_LORA, MLA_WIDTH).T.astype(BF16)
    qg = _head_block(q_norm[:MLA_NOPE], q_norm[MLA_NOPE:])[None]
    kg = _head_block(k_norm[:MLA_NOPE], jnp.zeros((MLA_ROPE,), F32))[None]
    kpg = _head_block(jnp.zeros((MLA_NOPE,), F32), k_norm[MLA_NOPE:])[None]

    wp_bd = jax.scipy.linalg.block_diag(*[pool_w[g] for g in range(POOL_GROUPS)]).astype(BF16)
    return dict(w_cat=w_cat, alog_l=alog_l, dtb_l=dtb_l, conv_w=dn_conv, dn_norm=dn_norm[None],
                qag=q_a_norm[None], kvag=kv_a_norm[None], wq=wq, wk=wk, wvt=wvt, qg=qg, kg=kg, kpg=kpg,
                wp_bd=wp_bd, pool_scale=pool_scale[None], w_out=w_out.astype(BF16),
                w_up=w_up.astype(BF16), w_down=w_down.astype(BF16))


def kernel(x, positions, attn_norm, w_in, pool_w, pool_scale, dn_conv, dn_a_log, dn_dt_bias, dn_norm,
           mla_q_a_norm, mla_w_q_b, mla_kv_a_norm, mla_w_kv_b, mla_q_norm, mla_k_norm,
           w_out, mlp_norm, w_up, w_down):
    b, s, d = x.shape
    depth = w_in.shape[0]
    inv_freq = ROPE_THETA ** (-jnp.arange(0, MLA_ROPE, 2, dtype=F32) / MLA_ROPE)
    freq_lanes = _head_block(jnp.zeros((MLA_NOPE,), F32), jnp.concatenate([inv_freq, inv_freq]))[None]
    cos, sina, sinb = _rope_tables(positions, freq_lanes)

    for l in range(depth):
        p = _prep_layer(w_in[l], pool_w[l], pool_scale[l], dn_conv[l], dn_a_log[l], dn_dt_bias[l], dn_norm[l],
                        mla_q_a_norm[l], mla_w_q_b[l], mla_kv_a_norm[l], mla_w_kv_b[l], mla_q_norm[l],
                        mla_k_norm[l], w_out[l], w_up[l], w_down[l])
        xa, qkv, z, cq, ckv, misc = _inproj(x.reshape(b * s, d), attn_norm[l][None], p["w_cat"])
        r3 = lambda a: a.reshape(b, s, a.shape[-1])
        xa, qkv, z, cq, ckv, misc = map(r3, (xa, qkv, z, cq, ckv, misc))
        y_b = _deltanet(qkv, z, misc, p["conv_w"], p["alog_l"], p["dtb_l"], p["dn_norm"])
        q, k, vt = _mla_prep(cq, ckv, misc, cos, sina, sinb, p["qag"], p["kvag"], p["wq"], p["wk"], p["wvt"],
                             p["qg"], p["kg"], p["kpg"])
        y_c = _attention(q, k, vt)
        x = _outproj(x, xa, y_b, y_c, p["wp_bd"], p["pool_scale"], p["w_out"])
        x = _mlp(x.reshape(b * s, d), mlp_norm[l][None], p["w_up"], p["w_down"]).reshape(b, s, d)
    return x
```

```python
import functools
import math

import jax
import jax.numpy as jnp
import numpy as np
from jax import lax
from jax.experimental import pallas as pl
from jax.experimental.pallas import tpu as pltpu

F32 = jnp.float32
BF16 = jnp.bfloat16

D_MODEL = 1024
POOL_GROUPS = 4
POOL_GROUP_DIM = 64
POOL_WIDTH = POOL_GROUPS * POOL_GROUP_DIM
POOL_WINDOWS = (2, 4, 8, 16)
DN_HEADS = 4
DN_HEAD_DIM = 128
DN_WIDTH = DN_HEADS * DN_HEAD_DIM
DN_CONV = 4
DN_CHUNK = 64
MLA_HEADS = 4
MLA_NOPE = 64
MLA_ROPE = 32
MLA_QK_DIM = MLA_NOPE + MLA_ROPE
MLA_V = 64
MLA_WIDTH = MLA_HEADS * MLA_V
Q_LORA = 256
KV_LORA = 128
ROPE_THETA = 10000.0
D_FF = 4 * D_MODEL
EPS = 1e-6
IN_SPLITS = (POOL_WIDTH, 3 * DN_WIDTH, DN_WIDTH, DN_HEADS, DN_HEADS, Q_LORA, KV_LORA, MLA_ROPE)

LANES = 128
SUBLANES = 8
VMEM_LIMIT_BYTES = 56 * 1024 * 1024

TM_PROJ = 512
TM_PREP = 512
DN_TILE = 256
TQ = 512
TK = 1024
HALF_ROPE = MLA_ROPE // 2
PE1_LANE = MLA_NOPE
PE2_LANE = MLA_NOPE + 2 * HALF_ROPE


def _dot(a, b):
    return jnp.dot(a, b, preferred_element_type=F32)


def _dot_nt(a, b):
    return lax.dot_general(a, b, (((1,), (1,)), ((), ())), preferred_element_type=F32)


def _rms_rows(x, gain):
    return x * lax.rsqrt(jnp.mean(x * x, axis=-1, keepdims=True) + EPS) * gain


def _split3(x):
    x1 = x.astype(BF16)
    r1 = x - x1.astype(F32)
    x2 = r1.astype(BF16)
    r2 = r1 - x2.astype(F32)
    return x1, x2, r2.astype(BF16)


def _sigmoid(x):
    return 1.0 / (1.0 + jnp.exp(-x))


def _params(*sem):
    return pltpu.CompilerParams(dimension_semantics=sem, vmem_limit_bytes=VMEM_LIMIT_BYTES)


def _const_spec(shape):
    nd = len(shape)
    return pl.BlockSpec(shape, lambda *_: (0,) * nd, pipeline_mode=pl.Buffered(1))


IN_SEGS = (POOL_WIDTH, 3 * DN_WIDTH, DN_WIDTH, Q_LORA, KV_LORA, LANES)


def _inproj_kernel(x_ref, g_ref, w_ref, *o_refs):
    h = _rms_rows(x_ref[...], g_ref[...]).astype(BF16)
    off = 0
    for o_ref in o_refs:
        n = o_ref.shape[-1]
        o_ref[...] = _dot(h, w_ref[:, off:off + n])
        off += n


def _inproj(x2, gain, w_cat):
    t = x2.shape[0]
    n_all = sum(IN_SEGS)
    return pl.pallas_call(
        _inproj_kernel,
        grid=(t // TM_PROJ,),
        in_specs=[pl.BlockSpec((TM_PROJ, D_MODEL), lambda i: (i, 0)),
                  _const_spec((1, D_MODEL)),
                  _const_spec((D_MODEL, n_all))],
        out_specs=[pl.BlockSpec((TM_PROJ, n), lambda i: (i, 0)) for n in IN_SEGS],
        out_shape=[jax.ShapeDtypeStruct((t, n), F32) for n in IN_SEGS],
        compiler_params=_params("parallel"),
        name="inproj",
    )(x2, gain, w_cat)


CONV_HALO = SUBLANES


def _dn_kernel(qkv_ref, z_ref, misc_ref, cw_ref, alog_ref, dtb_ref, ng_ref, y_ref,
               state_sc, halo_sc, ext_sc):
    L = DN_TILE
    C = DN_CHUNK
    nchunk = L // C
    D = DN_HEAD_DIM
    s_idx = pl.program_id(1)

    @pl.when(s_idx == 0)
    def _():
        state_sc[...] = jnp.zeros_like(state_sc)
        halo_sc[...] = jnp.zeros_like(halo_sc)

    ext_sc[0:CONV_HALO, :] = halo_sc[...]
    ext_sc[CONV_HALO:, :] = qkv_ref[0]
    halo_sc[...] = qkv_ref[0, L - CONV_HALO:L, :]

    ri = lax.broadcasted_iota(jnp.int32, (L, L), 0)
    ci = lax.broadcasted_iota(jnp.int32, (L, L), 1)
    same_chunk = (ri >> 6) == (ci >> 6)
    causal_bd = same_chunk & (ci <= ri)
    strict_bd = same_chunk & (ci < ri)
    eye = (ri == ci).astype(F32)

    misc = misc_ref[0]
    beta_all = _sigmoid(misc)
    sp_in = misc + dtb_ref[...]
    softplus = jnp.maximum(sp_in, 0.0) + jnp.log1p(jnp.exp(-jnp.abs(sp_in)))
    g_all = -jnp.exp(alog_ref[...]) * softplus

    cum_mat = jnp.concatenate([causal_bd.astype(BF16), same_chunk.astype(BF16)], axis=0)
    g1, g2, g3 = _split3(g_all)
    cum = _dot(cum_mat, g1) + _dot(cum_mat, g2) + _dot(cum_mat, g3)
    gcum_all = cum[:L]
    glast_all = cum[L:]
    gcum_t = gcum_all.T

    for h in range(DN_HEADS):
        lane = DN_HEADS + h
        gc_col = gcum_all[:, lane:lane + 1]
        gl_col = glast_all[:, lane:lane + 1]
        gc_row = gcum_t[lane:lane + 1, :]
        beta = beta_all[:, h:h + 1]

        def conv_part(p):
            c0 = p * DN_WIDTH + h * D
            e = ext_sc[:, c0:c0 + D]
            cw = cw_ref[:, c0:c0 + D]
            acc = cw[DN_CONV - 1:DN_CONV] * e
            for j in range(1, DN_CONV):
                acc = acc + cw[DN_CONV - 1 - j:DN_CONV - j] * pltpu.roll(e, j, 0)
            acc = acc[CONV_HALO:]
            return acc * _sigmoid(acc)

        q = conv_part(0)
        k = conv_part(1)
        v = conv_part(2)
        qn = q * lax.rsqrt(jnp.sum(q * q, axis=-1, keepdims=True) + EPS) * (D ** -0.5)
        kn = k * lax.rsqrt(jnp.sum(k * k, axis=-1, keepdims=True) + EPS)
        kb = kn * beta
        vb = v * beta
        e_col = jnp.exp(gc_col)
        decay = jnp.exp(jnp.where(causal_bd, gc_col - gc_row, 0.0))
        kn16 = kn.astype(BF16)
        a_mat = jnp.where(strict_bd, _dot_nt(kb.astype(BF16), kn16) * decay, 0.0)
        attn = jnp.where(causal_bd, _dot_nt(qn.astype(BF16), kn16) * decay, 0.0).astype(BF16)

        xp = (-a_mat).astype(BF16)
        t_inv = eye - a_mat
        for _ in range(5):
            x2 = _dot(xp, xp)
            xp = x2.astype(BF16)
            t_inv = t_inv + _dot(t_inv.astype(BF16), xp)

        rhs = jnp.concatenate([vb, kb * e_col], axis=1).astype(BF16)
        uw = _dot(t_inv.astype(BF16), rhs)
        u = uw[:, :D]
        w16 = uw[:, D:].astype(BF16)
        qd16 = (qn * e_col).astype(BF16)
        kd_t16 = (kn * jnp.exp(gl_col - gc_col)).T.astype(BF16)
        gdec = jnp.exp(jnp.broadcast_to(gl_col, (L, D)))

        state = state_sc[h]
        o_parts = []
        zeros_c = jnp.zeros((C, D), BF16)
        for c in range(nchunk):
            r0 = c * C
            s16 = state.astype(BF16)
            wq = jnp.concatenate([w16[r0:r0 + C], qd16[r0:r0 + C]], axis=0)
            r = _dot(wq, s16)
            v_new = (u[r0:r0 + C] - r[:C]).astype(BF16)
            v_full = jnp.concatenate([zeros_c] * c + [v_new] + [zeros_c] * (nchunk - 1 - c), axis=0)
            o_parts.append(r[C:] + _dot(attn[r0:r0 + C, :], v_full))
            state = state * gdec[r0:r0 + 1, :] + _dot(kd_t16, v_full)
        state_sc[h] = state

        o = jnp.concatenate(o_parts, axis=0)
        zh = z_ref[0, :, h * D:(h + 1) * D]
        y_ref[0, :, h * D:(h + 1) * D] = _rms_rows(o, ng_ref[...]) * (zh * _sigmoid(zh))


def _deltanet(qkv, z, misc, conv_w, alog_l, dtb_l, norm_gain):
    b, s, _ = qkv.shape
    L = DN_TILE
    return pl.pallas_call(
        _dn_kernel,
        grid=(b, s // L),
        in_specs=[pl.BlockSpec((1, L, 3 * DN_WIDTH), lambda i, j: (i, j, 0)),
                  pl.BlockSpec((1, L, DN_WIDTH), lambda i, j: (i, j, 0)),
                  pl.BlockSpec((1, L, LANES), lambda i, j: (i, j, 0)),
                  _const_spec((DN_CONV, 3 * DN_WIDTH)),
                  _const_spec((1, LANES)),
                  _const_spec((1, LANES)),
                  _const_spec((1, DN_HEAD_DIM))],
        out_specs=pl.BlockSpec((1, L, DN_WIDTH), lambda i, j: (i, j, 0)),
        out_shape=jax.ShapeDtypeStruct((b, s, DN_WIDTH), F32),
        scratch_shapes=[pltpu.VMEM((DN_HEADS, DN_HEAD_DIM, DN_HEAD_DIM), F32),
                        pltpu.VMEM((CONV_HALO, 3 * DN_WIDTH), F32),
                        pltpu.VMEM((L + CONV_HALO, 3 * DN_WIDTH), F32)],
        compiler_params=_params("parallel", "arbitrary"),
        name="deltanet",
    )(qkv, z, misc, conv_w, alog_l, dtb_l, norm_gain)


def _rope_kernel(pos_ref, freq_ref, cos_ref, sina_ref, sinb_ref):
    tm = pos_ref.shape[1]
    ang = pos_ref[0].astype(F32) * freq_ref[...]
    lane = lax.broadcasted_iota(jnp.int32, (tm, LANES), 1)
    pe1 = (lane >= PE1_LANE) & (lane < PE1_LANE + HALF_ROPE)
    pe2 = (lane >= PE2_LANE) & (lane < PE2_LANE + HALF_ROPE)
    cos = jnp.cos(ang)
    sin = jnp.sin(ang)
    cos_ref[0] = jnp.where(lane < MLA_NOPE, 1.0, jnp.where(pe1 | pe2, cos, 0.0))
    sina_ref[0] = jnp.where(pe2, sin, 0.0)
    sinb_ref[0] = jnp.where(pe1, -sin, 0.0)


def _rope_tables(positions, freq_lanes):
    b, s = positions.shape
    tm = TM_PREP
    spec = pl.BlockSpec((1, tm, LANES), lambda i, j: (i, j, 0))
    return pl.pallas_call(
        _rope_kernel,
        grid=(b, s // tm),
        in_specs=[pl.BlockSpec((1, tm, 1), lambda i, j: (i, j, 0)), _const_spec((1, LANES))],
        out_specs=[spec, spec, spec],
        out_shape=[jax.ShapeDtypeStruct((b, s, LANES), F32)] * 3,
        compiler_params=_params("parallel", "parallel"),
        name="rope_tables",
    )(positions.reshape(b, s, 1), freq_lanes)


def _rope(x, cos, sina, sinb):
    return x * cos + pltpu.roll(x, 2 * HALF_ROPE, 1) * sina + pltpu.roll(x, LANES - 2 * HALF_ROPE, 1) * sinb


def _mla_prep_kernel(cq_ref, ckv_ref, misc_ref, cos_ref, sina_ref, sinb_ref,
                     qag_ref, kvag_ref, wq_ref, wk_ref, wvt_ref, qg_ref, kg_ref, kpg_ref,
                     q_ref, k_ref, vt_ref):
    tm = cq_ref.shape[1]
    cos = cos_ref[0]
    sina = sina_ref[0]
    sinb = sinb_ref[0]
    lane = lax.broadcasted_iota(jnp.int32, (tm, LANES), 1)
    is_nope = lane < MLA_NOPE

    cqn = _rms_rows(cq_ref[0], qag_ref[...]).astype(BF16)
    ckvn = _rms_rows(ckv_ref[0], kvag_ref[...]).astype(BF16)
    qf = _dot(cqn, wq_ref[...])
    kf = _dot(ckvn, wk_ref[...])
    vt_ref[0] = _dot_nt(wvt_ref[...], ckvn).astype(BF16)

    kp = jnp.where(is_nope, 0.0, misc_ref[0])
    kp_ms = jnp.sum(kp * kp, axis=-1, keepdims=True) * (1.0 / MLA_ROPE)
    kp = _rope(kp * lax.rsqrt(kp_ms + EPS) * kpg_ref[...], cos, sina, sinb)

    scale = MLA_QK_DIM ** -0.5 * math.log2(math.e)
    for h in range(MLA_HEADS):
        xq = qf[:, h * LANES:(h + 1) * LANES]
        sq = xq * xq
        ms_n = jnp.sum(jnp.where(is_nope, sq, 0.0), axis=-1, keepdims=True) * (1.0 / MLA_NOPE)
        ms_p = jnp.sum(jnp.where(is_nope, 0.0, sq), axis=-1, keepdims=True) * (1.0 / MLA_ROPE)
        inv = jnp.where(is_nope, lax.rsqrt(ms_n + EPS), lax.rsqrt(ms_p + EPS))
        qh = _rope(xq * inv * qg_ref[...], cos, sina, sinb) * scale
        q_ref[0, h] = qh.astype(BF16)

        xk = kf[:, h * LANES:(h + 1) * LANES]
        ms_k = jnp.sum(xk * xk, axis=-1, keepdims=True) * (1.0 / MLA_NOPE)
        kh = xk * lax.rsqrt(ms_k + EPS) * kg_ref[...] + kp
        k_ref[0, h] = kh.astype(BF16)


def _mla_prep(cq, ckv, misc, cos, sina, sinb, qag, kvag, wq, wk, wvt, qg, kg, kpg):
    b, s, _ = cq.shape
    tm = TM_PREP
    hl = MLA_HEADS * LANES

    def tok(n):
        return pl.BlockSpec((1, tm, n), lambda i, j: (i, j, 0))

    return pl.pallas_call(
        _mla_prep_kernel,
        grid=(b, s // tm),
        in_specs=[tok(Q_LORA), tok(KV_LORA), tok(LANES), tok(LANES), tok(LANES), tok(LANES),
                  _const_spec((1, Q_LORA)), _const_spec((1, KV_LORA)),
                  _const_spec((Q_LORA, hl)), _const_spec((KV_LORA, hl)), _const_spec((MLA_WIDTH, KV_LORA)),
                  _const_spec((1, LANES)), _const_spec((1, LANES)), _const_spec((1, LANES))],
        out_specs=[pl.BlockSpec((1, MLA_HEADS, tm, LANES), lambda i, j: (i, 0, j, 0)),
                   pl.BlockSpec((1, MLA_HEADS, tm, LANES), lambda i, j: (i, 0, j, 0)),
                   pl.BlockSpec((1, MLA_WIDTH, tm), lambda i, j: (i, 0, j))],
        out_shape=[jax.ShapeDtypeStruct((b, MLA_HEADS, s, LANES), BF16),
                   jax.ShapeDtypeStruct((b, MLA_HEADS, s, LANES), BF16),
                   jax.ShapeDtypeStruct((b, MLA_WIDTH, s), BF16)],
        compiler_params=_params("parallel", "parallel"),
        name="mla_prep",
    )(cq, ckv, misc, cos, sina, sinb, qag, kvag, wq, wk, wvt, qg, kg, kpg)


def _attn_block(q_ref, k_ref, vt_ref, m_sc, l_sc, acc_sc, visible):
    scores = [_dot_nt(k_ref[0, 0], q_ref[0, 0])]
    for h in range(MLA_HEADS):
        if h + 1 < MLA_HEADS:
            scores.append(_dot_nt(k_ref[0, h + 1], q_ref[0, h + 1]))
        st = scores[h]
        if visible is not None:
            st = jnp.where(visible, st, -jnp.inf)
        m_prev = m_sc[h]
        m_new = jnp.maximum(m_prev, jnp.max(st, axis=0, keepdims=True))
        alpha = jnp.exp2(m_prev - m_new)
        pt = jnp.exp2(st - m_new)
        l_sc[h] = alpha * l_sc[h] + jnp.sum(pt, axis=0, keepdims=True)
        acc_sc[h] = alpha * acc_sc[h] + _dot(vt_ref[0, h * MLA_V:(h + 1) * MLA_V, :], pt.astype(BF16))
        m_sc[h] = m_new


def _attn_kernel(qi_ref, kj_ref, last_ref, q_ref, k_ref, vt_ref, o_ref, m_sc, l_sc, acc_sc):
    p = pl.program_id(1)
    qi = qi_ref[p]
    kj = kj_ref[p]

    @pl.when(kj == 0)
    def _():
        m_sc[...] = jnp.full_like(m_sc, -jnp.inf)
        l_sc[...] = jnp.zeros_like(l_sc)
        acc_sc[...] = jnp.zeros_like(acc_sc)

    all_visible = kj * TK + (TK - 1) <= qi * TQ

    @pl.when(all_visible)
    def _():
        _attn_block(q_ref, k_ref, vt_ref, m_sc, l_sc, acc_sc, None)

    @pl.when(jnp.logical_not(all_visible))
    def _():
        kpos = kj * TK + lax.broadcasted_iota(jnp.int32, (TK, TQ), 0)
        qpos = qi * TQ + lax.broadcasted_iota(jnp.int32, (TK, TQ), 1)
        _attn_block(q_ref, k_ref, vt_ref, m_sc, l_sc, acc_sc, kpos <= qpos)

    @pl.when(last_ref[p] == 1)
    def _():
        out_t = jnp.concatenate([acc_sc[h] / l_sc[h] for h in range(MLA_HEADS)], axis=0)
        o_ref[0] = out_t.T


def _attention(q, k, vt):
    b, _, s, _ = q.shape
    nq = s // TQ
    pairs = [(i, j) for i in range(nq) for j in range((i * TQ + TQ - 1) // TK + 1)]
    qi = jnp.asarray(np.array([p[0] for p in pairs], np.int32))
    kj = jnp.asarray(np.array([p[1] for p in pairs], np.int32))
    last = jnp.asarray(np.array([int(p[1] == (p[0] * TQ + TQ - 1) // TK) for p in pairs], np.int32))
    grid_spec = pltpu.PrefetchScalarGridSpec(
        num_scalar_prefetch=3,
        grid=(b, len(pairs)),
        in_specs=[pl.BlockSpec((1, MLA_HEADS, TQ, LANES), lambda i, p, qi, kj, last: (i, 0, qi[p], 0)),
                  pl.BlockSpec((1, MLA_HEADS, TK, LANES), lambda i, p, qi, kj, last: (i, 0, kj[p], 0)),
                  pl.BlockSpec((1, MLA_WIDTH, TK), lambda i, p, qi, kj, last: (i, 0, kj[p]))],
        out_specs=pl.BlockSpec((1, TQ, MLA_WIDTH), lambda i, p, qi, kj, last: (i, qi[p], 0)),
        scratch_shapes=[pltpu.VMEM((MLA_HEADS, 1, TQ), F32),
                        pltpu.VMEM((MLA_HEADS, 1, TQ), F32),
                        pltpu.VMEM((MLA_HEADS, MLA_V, TQ), F32)],
    )
    return pl.pallas_call(
        _attn_kernel,
        grid_spec=grid_spec,
        out_shape=jax.ShapeDtypeStruct((b, s, MLA_WIDTH), F32),
        compiler_params=_params("parallel", "arbitrary"),
        name="mla_attention",
    )(qi, kj, last, q, k, vt)


POOL_HALO = 16


def _outproj_kernel(x_ref, xa_ref, yb_ref, yc_ref, wp_ref, ps_ref, wo_ref, o_ref, halo_sc, ext_sc):
    ts = x_ref.shape[1]
    s_idx = pl.program_id(1)

    @pl.when(s_idx == 0)
    def _():
        halo_sc[...] = jnp.zeros_like(halo_sc)

    xa = xa_ref[0]
    ext_sc[0:POOL_HALO, :] = halo_sc[...]
    ext_sc[POOL_HALO:, :] = xa
    halo_sc[...] = xa_ref[0, ts - POOL_HALO:ts, :]

    e = ext_sc[...]
    sums = []
    step = 1
    for _ in POOL_WINDOWS:
        e = e + pltpu.roll(e, step, 0)
        step *= 2
        sums.append(e[POOL_HALO:])
    lane = lax.broadcasted_iota(jnp.int32, (ts, POOL_WIDTH), 1)
    grp = lane >> 6
    win = jnp.left_shift(2, grp)
    t = s_idx * ts + lax.broadcasted_iota(jnp.int32, (ts, POOL_WIDTH), 0)
    count = jnp.minimum(t + 1, win).astype(F32)
    pooled = jnp.where(grp == 0, sums[0], jnp.where(grp == 1, sums[1], jnp.where(grp == 2, sums[2], sums[3])))
    ya = _dot((pooled / count - xa).astype(BF16), wp_ref[...]) * ps_ref[...]

    acc = x_ref[0] + _dot(ya.astype(BF16), wo_ref[0:POOL_WIDTH, :])
    acc = acc + _dot(yb_ref[0].astype(BF16), wo_ref[POOL_WIDTH:POOL_WIDTH + DN_WIDTH, :])
    acc = acc + _dot(yc_ref[0].astype(BF16), wo_ref[POOL_WIDTH + DN_WIDTH:, :])
    o_ref[0] = acc


def _outproj(x, xa, yb, yc, wp_bd, pool_scale, w_out):
    b, s, _ = x.shape
    ts = TM_PROJ

    def tok(n):
        return pl.BlockSpec((1, ts, n), lambda i, j: (i, j, 0))

    return pl.pallas_call(
        _outproj_kernel,
        grid=(b, s // ts),
        in_specs=[tok(D_MODEL), tok(POOL_WIDTH), tok(DN_WIDTH), tok(MLA_WIDTH),
                  _const_spec((POOL_WIDTH, POOL_WIDTH)), _const_spec((1, POOL_WIDTH)),
                  _const_spec((D_MODEL, D_MODEL))],
        out_specs=tok(D_MODEL),
        out_shape=jax.ShapeDtypeStruct((b, s, D_MODEL), F32),
        scratch_shapes=[pltpu.VMEM((POOL_HALO, POOL_WIDTH), F32),
                        pltpu.VMEM((ts + POOL_HALO, POOL_WIDTH), F32)],
        compiler_params=_params("parallel", "arbitrary"),
        name="outproj_pool",
    )(x, xa, yb, yc, wp_bd, pool_scale, w_out)


def _mlp_kernel(x_ref, g_ref, wu_ref, wd_ref, o_ref):
    x = x_ref[...]
    h = _rms_rows(x, g_ref[...]).astype(BF16)
    u = jnp.maximum(_dot(h, wu_ref[...]), 0.0)
    o_ref[...] = x + _dot((u * u).astype(BF16), wd_ref[...])


def _mlp(x2, gain, w_up, w_down):
    t = x2.shape[0]
    return pl.pallas_call(
        _mlp_kernel,
        grid=(t // TM_PROJ,),
        in_specs=[pl.BlockSpec((TM_PROJ, D_MODEL), lambda i: (i, 0)),
                  _const_spec((1, D_MODEL)),
                  _const_spec((D_MODEL, D_FF)),
                  _const_spec((D_FF, D_MODEL))],
        out_specs=pl.BlockSpec((TM_PROJ, D_MODEL), lambda i: (i, 0)),
        out_shape=jax.ShapeDtypeStruct((t, D_MODEL), F32),
        compiler_params=_params("parallel"),
        name="mlp",
    )(x2, gain, w_up, w_down)


def _head_block(nope, pe):
    z16 = jnp.zeros(pe.shape[:-1] + (HALF_ROPE,), pe.dtype)
    return jnp.concatenate([nope, pe[..., :HALF_ROPE], z16, pe[..., HALF_ROPE:], z16], axis=-1)


def _prep_layer(w_in, pool_w, pool_scale, dn_conv, dn_a_log, dn_dt_bias, dn_norm,
                q_a_norm, w_q_b, kv_a_norm, w_kv_b, q_norm, k_norm, w_out, w_up, w_down):
    offs = np.cumsum((0,) + IN_SPLITS)
    seg = [w_in[:, offs[i]:offs[i + 1]] for i in range(len(IN_SPLITS))]
    w_xa, w_qkv, w_z, w_b, w_a, w_cq, w_ckv, w_kpe = seg
    d = w_in.shape[0]
    z64 = jnp.zeros((d, MLA_NOPE), w_in.dtype)
    w_misc = jnp.concatenate([w_b, w_a, jnp.zeros((d, MLA_NOPE - 2 * DN_HEADS), w_in.dtype),
                              _head_block(z64, w_kpe)[:, MLA_NOPE:]], axis=1)
    w_cat = jnp.concatenate([w_xa, w_qkv, w_z, w_cq, w_ckv, w_misc], axis=1).astype(BF16)

    lane_pad = jnp.zeros((LANES - 2 * DN_HEADS,), F32)
    alog_l = jnp.concatenate([jnp.zeros((DN_HEADS,), F32), dn_a_log, lane_pad])[None]
    dtb_l = jnp.concatenate([jnp.zeros((DN_HEADS,), F32), dn_dt_bias, lane_pad])[None]

    wq = w_q_b.reshape(Q_LORA, MLA_HEADS, MLA_QK_DIM)
    wq = _head_block(wq[..., :MLA_NOPE], wq[..., MLA_NOPE:]).reshape(Q_LORA, MLA_HEADS * LANES).astype(BF16)
    wkv = w_kv_b.reshape(KV_LORA, MLA_HEADS, MLA_NOPE + MLA_V)
    wk = _head_block(wkv[..., :MLA_NOPE], jnp.zeros((KV_LORA, MLA_HEADS, MLA_ROPE), F32))
    wk = wk.reshape(KV_LORA, MLA_HEADS * LANES).astype(BF16)
    wvt = wkv[..., MLA_NOPE:].reshape(KV_LORA, MLA_WIDTH).T.astype(BF16)
    qg = _head_block(q_norm[:MLA_NOPE], q_norm[MLA_NOPE:])[None]
    kg = _head_block(k_norm[:MLA_NOPE], jnp.zeros((MLA_ROPE,), F32))[None]
    kpg = _head_block(jnp.zeros((MLA_NOPE,), F32), k_norm[MLA_NOPE:])[None]

    wp_bd = jax.scipy.linalg.block_diag(*[pool_w[g] for g in range(POOL_GROUPS)]).astype(BF16)
    return dict(w_cat=w_cat, alog_l=alog_l, dtb_l=dtb_l, conv_w=dn_conv, dn_norm=dn_norm[None],
                qag=q_a_norm[None], kvag=kv_a_norm[None], wq=wq, wk=wk, wvt=wvt, qg=qg, kg=kg, kpg=kpg,
                wp_bd=wp_bd, pool_scale=pool_scale[None], w_out=w_out.astype(BF16),
                w_up=w_up.astype(BF16), w_down=w_down.astype(BF16))


def kernel(x, positions, attn_norm, w_in, pool_w, pool_scale, dn_conv, dn_a_log, dn_dt_bias, dn_norm,
           mla_q_a_norm, mla_w_q_b, mla_kv_a_norm, mla_w_kv_b, mla_q_norm, mla_k_norm,
           w_out, mlp_norm, w_up, w_down):
    b, s, d = x.shape
    depth = w_in.shape[0]
    inv_freq = ROPE_THETA ** (-jnp.arange(0, MLA_ROPE, 2, dtype=F32) / MLA_ROPE)
    freq_lanes = _head_block(jnp.zeros((MLA_NOPE,), F32), jnp.concatenate([inv_freq, inv_freq]))[None]
    cos, sina, sinb = _rope_tables(positions, freq_lanes)

    for l in range(depth):
        p = _prep_layer(w_in[l], pool_w[l], pool_scale[l], dn_conv[l], dn_a_log[l], dn_dt_bias[l], dn_norm[l],
                        mla_q_a_norm[l], mla_w_q_b[l], mla_kv_a_norm[l], mla_w_kv_b[l], mla_q_norm[l],
                        mla_k_norm[l], w_out[l], w_up[l], w_down[l])
        xa, qkv, z, cq, ckv, misc = _inproj(x.reshape(b * s, d), attn_norm[l][None], p["w_cat"])
        r3 = lambda a: a.reshape(b, s, a.shape[-1])
        xa, qkv, z, cq, ckv, misc = map(r3, (xa, qkv, z, cq, ckv, misc))
        y_b = _deltanet(qkv, z, misc, p["conv_w"], p["alog_l"], p["dtb_l"], p["dn_norm"])
        q, k, vt = _mla_prep(cq, ckv, misc, cos, sina, sinb, p["qag"], p["kvag"], p["wq"], p["wk"], p["wvt"],
                             p["qg"], p["kg"], p["kpg"])
        y_c = _attention(q, k, vt)
        x = _outproj(x, xa, y_b, y_c, p["wp_bd"], p["pool_scale"], p["w_out"])
        x = _mlp(x.reshape(b * s, d), mlp_norm[l][None], p["w_up"], p["w_down"]).reshape(b, s, d)
    return x
```

```python
import functools
import math

import jax
import jax.numpy as jnp
import numpy as np
from jax import lax
from jax.experimental import pallas as pl
from jax.experimental.pallas import tpu as pltpu

F32 = jnp.float32
BF16 = jnp.bfloat16

D_MODEL = 1024
POOL_GROUPS = 4
POOL_GROUP_DIM = 64
POOL_WIDTH = POOL_GROUPS * POOL_GROUP_DIM
POOL_WINDOWS = (2, 4, 8, 16)
DN_HEADS = 4
DN_HEAD_DIM = 128
DN_WIDTH = DN_HEADS * DN_HEAD_DIM
DN_CONV = 4
DN_CHUNK = 64
MLA_HEADS = 4
MLA_NOPE = 64
MLA_ROPE = 32
MLA_QK_DIM = MLA_NOPE + MLA_ROPE
MLA_V = 64
MLA_WIDTH = MLA_HEADS * MLA_V
Q_LORA = 256
KV_LORA = 128
ROPE_THETA = 10000.0
D_FF = 4 * D_MODEL
EPS = 1e-6
IN_SPLITS = (POOL_WIDTH, 3 * DN_WIDTH, DN_WIDTH, DN_HEADS, DN_HEADS, Q_LORA, KV_LORA, MLA_ROPE)

LANES = 128
SUBLANES = 8
VMEM_LIMIT_BYTES = 56 * 1024 * 1024

TM_PROJ = 512
TM_PREP = 512
DN_TILE = 256
DN_PAIR = 2 * DN_CHUNK
TQ = 512
TK = 1024
HALF_ROPE = MLA_ROPE // 2
PE1_LANE = MLA_NOPE
PE2_LANE = MLA_NOPE + 2 * HALF_ROPE


def _dot(a, b):
    return jnp.dot(a, b, preferred_element_type=F32)


def _dot_nt(a, b):
    return lax.dot_general(a, b, (((1,), (1,)), ((), ())), preferred_element_type=F32)


def _rms_rows(x, gain):
    return x * lax.rsqrt(jnp.mean(x * x, axis=-1, keepdims=True) + EPS) * gain


def _split3(x):
    x1 = x.astype(BF16)
    r1 = x - x1.astype(F32)
    x2 = r1.astype(BF16)
    r2 = r1 - x2.astype(F32)
    return x1, x2, r2.astype(BF16)


def _sigmoid(x):
    return 1.0 / (1.0 + jnp.exp(-x))


def _params(*sem):
    return pltpu.CompilerParams(dimension_semantics=sem, vmem_limit_bytes=VMEM_LIMIT_BYTES)


def _const_spec(shape):
    nd = len(shape)
    return pl.BlockSpec(shape, lambda *_: (0,) * nd, pipeline_mode=pl.Buffered(1))


IN_SEGS = (POOL_WIDTH, 3 * DN_WIDTH, DN_WIDTH, Q_LORA, KV_LORA, LANES)


def _inproj_kernel(x_ref, g_ref, w_ref, *o_refs):
    h = _rms_rows(x_ref[...], g_ref[...]).astype(BF16)
    off = 0
    for o_ref in o_refs:
        n = o_ref.shape[-1]
        o_ref[...] = _dot(h, w_ref[:, off:off + n])
        off += n


def _inproj(x2, gain, w_cat):
    t = x2.shape[0]
    n_all = sum(IN_SEGS)
    return pl.pallas_call(
        _inproj_kernel,
        grid=(t // TM_PROJ,),
        in_specs=[pl.BlockSpec((TM_PROJ, D_MODEL), lambda i: (i, 0)),
                  _const_spec((1, D_MODEL)),
                  _const_spec((D_MODEL, n_all))],
        out_specs=[pl.BlockSpec((TM_PROJ, n), lambda i: (i, 0)) for n in IN_SEGS],
        out_shape=[jax.ShapeDtypeStruct((t, n), F32) for n in IN_SEGS],
        compiler_params=_params("parallel"),
        name="inproj",
    )(x2, gain, w_cat)


CONV_HALO = SUBLANES


def _dn_kernel(qkv_ref, z_ref, misc_ref, cw_ref, alog_ref, dtb_ref, ng_ref, y_ref,
               state_sc, halo_sc, ext_sc):
    nb = qkv_ref.shape[0]
    L = DN_TILE
    C = DN_CHUNK
    nchunk = L // C
    D = DN_HEAD_DIM
    s_idx = pl.program_id(0)
    chains = [(b, h) for b in range(nb) for h in range(DN_HEADS)]

    @pl.when(s_idx == 0)
    def _():
        state_sc[...] = jnp.zeros_like(state_sc)
        halo_sc[...] = jnp.zeros_like(halo_sc)

    for b in range(nb):
        ext_sc[b, 0:CONV_HALO, :] = halo_sc[b]
        ext_sc[b, CONV_HALO:, :] = qkv_ref[b]
        halo_sc[b] = qkv_ref[b, L - CONV_HALO:L, :]

    def chunk_masks(n):
        ri = lax.broadcasted_iota(jnp.int32, (n, n), 0)
        ci = lax.broadcasted_iota(jnp.int32, (n, n), 1)
        same = (ri >> 6) == (ci >> 6)
        return same, same & (ci <= ri), same & (ci < ri), ri == ci

    same_l, causal_l, _, _ = chunk_masks(L)
    cum_mat = jnp.concatenate([causal_l.astype(BF16), same_l.astype(BF16)], axis=0)
    P = DN_PAIR
    nblk = L // P
    _, causal_bd, strict_bd, diag = chunk_masks(P)
    eye = diag.astype(F32)

    beta_all, gcum_all, glast_all, gcum_t = [], [], [], []
    for b in range(nb):
        misc = misc_ref[b]
        beta_all.append(_sigmoid(misc))
        sp_in = misc + dtb_ref[...]
        softplus = jnp.maximum(sp_in, 0.0) + jnp.log1p(jnp.exp(-jnp.abs(sp_in)))
        g_all = -jnp.exp(alog_ref[...]) * softplus
        g1, g2, g3 = _split3(g_all)
        cum = _dot(cum_mat, g1) + _dot(cum_mat, g2) + _dot(cum_mat, g3)
        gcum_all.append(cum[:L])
        glast_all.append(cum[L:])
        gcum_t.append(cum[:L].T)

    def conv_part(b, h, p):
        c0 = p * DN_WIDTH + h * D
        e = ext_sc[b, :, c0:c0 + D]
        cw = cw_ref[:, c0:c0 + D]
        acc = cw[DN_CONV - 1:DN_CONV] * e
        for j in range(1, DN_CONV):
            acc = acc + cw[DN_CONV - 1 - j:DN_CONV - j] * pltpu.roll(e, j, 0)
        acc = acc[CONV_HALO:]
        return acc * _sigmoid(acc)

    def stage1(b, h):
        a_mats, attns = [], []
        lane = DN_HEADS + h
        gc_col = gcum_all[b][:, lane:lane + 1]
        gl_col = glast_all[b][:, lane:lane + 1]
        gc_row = gcum_t[b][lane:lane + 1, :]
        beta = beta_all[b][:, h:h + 1]
        q = conv_part(b, h, 0)
        k = conv_part(b, h, 1)
        v = conv_part(b, h, 2)
        qn = q * lax.rsqrt(jnp.sum(q * q, axis=-1, keepdims=True) + EPS) * (D ** -0.5)
        kn = k * lax.rsqrt(jnp.sum(k * k, axis=-1, keepdims=True) + EPS)
        kb = kn * beta
        e_col = jnp.exp(gc_col)
        kn16 = kn.astype(BF16)
        kb16 = kb.astype(BF16)
        qn16 = qn.astype(BF16)
        for j in range(nblk):
            rows = slice(j * P, (j + 1) * P)
            decay = jnp.exp(jnp.where(causal_bd, gc_col[rows] - gc_row[:, rows], 0.0))
            a_mats.append(jnp.where(strict_bd, _dot_nt(kb16[rows], kn16[rows]) * decay, 0.0))
            attns.append(jnp.where(causal_bd, _dot_nt(qn16[rows], kn16[rows]) * decay, 0.0).astype(BF16))
        return dict(a=a_mats, attn=attns,
                    rhs=jnp.concatenate([v * beta, kb * e_col], axis=1).astype(BF16),
                    qd=(qn * e_col).astype(BF16),
                    kdt=(kn * jnp.exp(gl_col - gc_col)).T.astype(BF16),
                    gdec=jnp.exp(jnp.broadcast_to(gl_col, (L, D))))

    def stage2(group):
        a_mats = [a for ch in group for a in ch["a"]]
        xps = [(-a).astype(BF16) for a in a_mats]
        t_invs = [eye - a for a in a_mats]
        for _ in range(5):
            xps = [_dot(xp, xp).astype(BF16) for xp in xps]
            t_invs = [t + _dot(t.astype(BF16), xp) for t, xp in zip(t_invs, xps)]
        for i, ch in enumerate(group):
            ch["uw"] = [_dot(t_invs[i * nblk + j].astype(BF16), ch["rhs"][j * P:(j + 1) * P])
                        for j in range(nblk)]

    groups = []
    for b in range(nb):
        if groups:
            stage2(groups[-1])
        groups.append([stage1(b, h) for h in range(DN_HEADS)])
    stage2(groups[-1])
    chs = [ch for group in groups for ch in group]

    states = [state_sc[b, h] for b, h in chains]
    o_parts = [[] for _ in chains]
    zeros_c = jnp.zeros((C, D), BF16)
    for c in range(nchunk):
        r0 = c * C
        j, half = divmod(c, P // C)
        p0 = half * C
        rs = [_dot(jnp.concatenate([ch["uw"][j][p0:p0 + C, D:].astype(BF16), ch["qd"][r0:r0 + C]], axis=0),
                   states[i].astype(BF16)) for i, ch in enumerate(chs)]
        for i, ch in enumerate(chs):
            v_new = (ch["uw"][j][p0:p0 + C, :D] - rs[i][:C]).astype(BF16)
            v_blk = jnp.concatenate([zeros_c] * half + [v_new] + [zeros_c] * (P // C - 1 - half), axis=0)
            lhs = jnp.concatenate([ch["attn"][j][p0:p0 + C, :], ch["kdt"][:, j * P:(j + 1) * P]], axis=0)
            m2 = _dot(lhs, v_blk)
            o_parts[i].append(rs[i][C:] + m2[:C])
            states[i] = states[i] * ch["gdec"][r0:r0 + 1, :] + m2[C:]

    for i, (b, h) in enumerate(chains):
        state_sc[b, h] = states[i]
        o = jnp.concatenate(o_parts[i], axis=0)
        zh = z_ref[b, :, h * D:(h + 1) * D]
        y_ref[b, :, h * D:(h + 1) * D] = _rms_rows(o, ng_ref[...]) * (zh * _sigmoid(zh))


def _deltanet(qkv, z, misc, conv_w, alog_l, dtb_l, norm_gain):
    b, s, _ = qkv.shape
    L = DN_TILE
    return pl.pallas_call(
        _dn_kernel,
        grid=(s // L,),
        in_specs=[pl.BlockSpec((b, L, 3 * DN_WIDTH), lambda j: (0, j, 0)),
                  pl.BlockSpec((b, L, DN_WIDTH), lambda j: (0, j, 0)),
                  pl.BlockSpec((b, L, LANES), lambda j: (0, j, 0)),
                  _const_spec((DN_CONV, 3 * DN_WIDTH)),
                  _const_spec((1, LANES)),
                  _const_spec((1, LANES)),
                  _const_spec((1, DN_HEAD_DIM))],
        out_specs=pl.BlockSpec((b, L, DN_WIDTH), lambda j: (0, j, 0)),
        out_shape=jax.ShapeDtypeStruct((b, s, DN_WIDTH), F32),
        scratch_shapes=[pltpu.VMEM((b, DN_HEADS, DN_HEAD_DIM, DN_HEAD_DIM), F32),
                        pltpu.VMEM((b, CONV_HALO, 3 * DN_WIDTH), F32),
                        pltpu.VMEM((b, L + CONV_HALO, 3 * DN_WIDTH), F32)],
        compiler_params=_params("arbitrary"),
        name="deltanet",
    )(qkv, z, misc, conv_w, alog_l, dtb_l, norm_gain)


def _rope_kernel(pos_ref, freq_ref, cos_ref, sina_ref, sinb_ref):
    tm = pos_ref.shape[1]
    ang = pos_ref[0].astype(F32) * freq_ref[...]
    lane = lax.broadcasted_iota(jnp.int32, (tm, LANES), 1)
    pe1 = (lane >= PE1_LANE) & (lane < PE1_LANE + HALF_ROPE)
    pe2 = (lane >= PE2_LANE) & (lane < PE2_LANE + HALF_ROPE)
    cos = jnp.cos(ang)
    sin = jnp.sin(ang)
    cos_ref[0] = jnp.where(lane < MLA_NOPE, 1.0, jnp.where(pe1 | pe2, cos, 0.0))
    sina_ref[0] = jnp.where(pe2, sin, 0.0)
    sinb_ref[0] = jnp.where(pe1, -sin, 0.0)


def _rope_tables(positions, freq_lanes):
    b, s = positions.shape
    tm = TM_PREP
    spec = pl.BlockSpec((1, tm, LANES), lambda i, j: (i, j, 0))
    return pl.pallas_call(
        _rope_kernel,
        grid=(b, s // tm),
        in_specs=[pl.BlockSpec((1, tm, 1), lambda i, j: (i, j, 0)), _const_spec((1, LANES))],
        out_specs=[spec, spec, spec],
        out_shape=[jax.ShapeDtypeStruct((b, s, LANES), F32)] * 3,
        compiler_params=_params("parallel", "parallel"),
        name="rope_tables",
    )(positions.reshape(b, s, 1), freq_lanes)


def _rope(x, cos, sina, sinb):
    return x * cos + pltpu.roll(x, 2 * HALF_ROPE, 1) * sina + pltpu.roll(x, LANES - 2 * HALF_ROPE, 1) * sinb


def _mla_prep_kernel(cq_ref, ckv_ref, misc_ref, cos_ref, sina_ref, sinb_ref,
                     qag_ref, kvag_ref, wq_ref, wk_ref, wvt_ref, qg_ref, kg_ref, kpg_ref,
                     q_ref, k_ref, vt_ref):
    tm = cq_ref.shape[1]
    cos = cos_ref[0]
    sina = sina_ref[0]
    sinb = sinb_ref[0]
    lane = lax.broadcasted_iota(jnp.int32, (tm, LANES), 1)
    is_nope = lane < MLA_NOPE

    cqn = _rms_rows(cq_ref[0], qag_ref[...]).astype(BF16)
    ckvn = _rms_rows(ckv_ref[0], kvag_ref[...]).astype(BF16)
    qf = _dot(cqn, wq_ref[...])
    kf = _dot(ckvn, wk_ref[...])
    vt_ref[0] = _dot_nt(wvt_ref[...], ckvn).astype(BF16)

    kp = jnp.where(is_nope, 0.0, misc_ref[0])
    kp_ms = jnp.sum(kp * kp, axis=-1, keepdims=True) * (1.0 / MLA_ROPE)
    kp = _rope(kp * lax.rsqrt(kp_ms + EPS) * kpg_ref[...], cos, sina, sinb)

    scale = MLA_QK_DIM ** -0.5 * math.log2(math.e)
    for h in range(MLA_HEADS):
        xq = qf[:, h * LANES:(h + 1) * LANES]
        sq = xq * xq
        ms_n = jnp.sum(jnp.where(is_nope, sq, 0.0), axis=-1, keepdims=True) * (1.0 / MLA_NOPE)
        ms_p = jnp.sum(jnp.where(is_nope, 0.0, sq), axis=-1, keepdims=True) * (1.0 / MLA_ROPE)
        inv = jnp.where(is_nope, lax.rsqrt(ms_n + EPS), lax.rsqrt(ms_p + EPS))
        qh = _rope(xq * inv * qg_ref[...], cos, sina, sinb) * scale
        q_ref[0, h] = qh.astype(BF16)

        xk = kf[:, h * LANES:(h + 1) * LANES]
        ms_k = jnp.sum(xk * xk, axis=-1, keepdims=True) * (1.0 / MLA_NOPE)
        kh = xk * lax.rsqrt(ms_k + EPS) * kg_ref[...] + kp
        k_ref[0, h] = kh.astype(BF16)


def _mla_prep(cq, ckv, misc, cos, sina, sinb, qag, kvag, wq, wk, wvt, qg, kg, kpg):
    b, s, _ = cq.shape
    tm = TM_PREP
    hl = MLA_HEADS * LANES

    def tok(n):
        return pl.BlockSpec((1, tm, n), lambda i, j: (i, j, 0))

    return pl.pallas_call(
        _mla_prep_kernel,
        grid=(b, s // tm),
        in_specs=[tok(Q_LORA), tok(KV_LORA), tok(LANES), tok(LANES), tok(LANES), tok(LANES),
                  _const_spec((1, Q_LORA)), _const_spec((1, KV_LORA)),
                  _const_spec((Q_LORA, hl)), _const_spec((KV_LORA, hl)), _const_spec((MLA_WIDTH, KV_LORA)),
                  _const_spec((1, LANES)), _const_spec((1, LANES)), _const_spec((1, LANES))],
        out_specs=[pl.BlockSpec((1, MLA_HEADS, tm, LANES), lambda i, j: (i, 0, j, 0)),
                   pl.BlockSpec((1, MLA_HEADS, tm, LANES), lambda i, j: (i, 0, j, 0)),
                   pl.BlockSpec((1, MLA_WIDTH, tm), lambda i, j: (i, 0, j))],
        out_shape=[jax.ShapeDtypeStruct((b, MLA_HEADS, s, LANES), BF16),
                   jax.ShapeDtypeStruct((b, MLA_HEADS, s, LANES), BF16),
                   jax.ShapeDtypeStruct((b, MLA_WIDTH, s), BF16)],
        compiler_params=_params("parallel", "parallel"),
        name="mla_prep",
    )(cq, ckv, misc, cos, sina, sinb, qag, kvag, wq, wk, wvt, qg, kg, kpg)


def _attn_block(q_ref, k_ref, vt_ref, m_sc, l_sc, acc_sc, visible):
    scores = [_dot_nt(k_ref[0, 0], q_ref[0, 0])]
    for h in range(MLA_HEADS):
        if h + 1 < MLA_HEADS:
            scores.append(_dot_nt(k_ref[0, h + 1], q_ref[0, h + 1]))
        st = scores[h]
        if visible is not None:
            st = jnp.where(visible, st, -jnp.inf)
        m_prev = m_sc[h]
        m_new = jnp.maximum(m_prev, jnp.max(st, axis=0, keepdims=True))
        alpha = jnp.exp2(m_prev - m_new)
        pt = jnp.exp2(st - m_new)
        l_sc[h] = alpha * l_sc[h] + jnp.sum(pt, axis=0, keepdims=True)
        acc_sc[h] = alpha * acc_sc[h] + _dot(vt_ref[0, h * MLA_V:(h + 1) * MLA_V, :], pt.astype(BF16))
        m_sc[h] = m_new


def _attn_kernel(qi_ref, kj_ref, last_ref, q_ref, k_ref, vt_ref, o_ref, m_sc, l_sc, acc_sc):
    p = pl.program_id(1)
    qi = qi_ref[p]
    kj = kj_ref[p]

    @pl.when(kj == 0)
    def _():
        m_sc[...] = jnp.full_like(m_sc, -jnp.inf)
        l_sc[...] = jnp.zeros_like(l_sc)
        acc_sc[...] = jnp.zeros_like(acc_sc)

    all_visible = kj * TK + (TK - 1) <= qi * TQ

    @pl.when(all_visible)
    def _():
        _attn_block(q_ref, k_ref, vt_ref, m_sc, l_sc, acc_sc, None)

    @pl.when(jnp.logical_not(all_visible))
    def _():
        kpos = kj * TK + lax.broadcasted_iota(jnp.int32, (TK, TQ), 0)
        qpos = qi * TQ + lax.broadcasted_iota(jnp.int32, (TK, TQ), 1)
        _attn_block(q_ref, k_ref, vt_ref, m_sc, l_sc, acc_sc, kpos <= qpos)

    @pl.when(last_ref[p] == 1)
    def _():
        out_t = jnp.concatenate([acc_sc[h] / l_sc[h] for h in range(MLA_HEADS)], axis=0)
        o_ref[0] = out_t.T


def _attention(q, k, vt):
    b, _, s, _ = q.shape
    nq = s // TQ
    pairs = [(i, j) for i in range(nq) for j in range((i * TQ + TQ - 1) // TK + 1)]
    qi = jnp.asarray(np.array([p[0] for p in pairs], np.int32))
    kj = jnp.asarray(np.array([p[1] for p in pairs], np.int32))
    last = jnp.asarray(np.array([int(p[1] == (p[0] * TQ + TQ - 1) // TK) for p in pairs], np.int32))
    grid_spec = pltpu.PrefetchScalarGridSpec(
        num_scalar_prefetch=3,
        grid=(b, len(pairs)),
        in_specs=[pl.BlockSpec((1, MLA_HEADS, TQ, LANES), lambda i, p, qi, kj, last: (i, 0, qi[p], 0)),
                  pl.BlockSpec((1, MLA_HEADS, TK, LANES), lambda i, p, qi, kj, last: (i, 0, kj[p], 0)),
                  pl.BlockSpec((1, MLA_WIDTH, TK), lambda i, p, qi, kj, last: (i, 0, kj[p]))],
        out_specs=pl.BlockSpec((1, TQ, MLA_WIDTH), lambda i, p, qi, kj, last: (i, qi[p], 0)),
        scratch_shapes=[pltpu.VMEM((MLA_HEADS, 1, TQ), F32),
                        pltpu.VMEM((MLA_HEADS, 1, TQ), F32),
                        pltpu.VMEM((MLA_HEADS, MLA_V, TQ), F32)],
    )
    return pl.pallas_call(
        _attn_kernel,
        grid_spec=grid_spec,
        out_shape=jax.ShapeDtypeStruct((b, s, MLA_WIDTH), F32),
        compiler_params=_params("parallel", "arbitrary"),
        name="mla_attention",
    )(qi, kj, last, q, k, vt)


POOL_HALO = 16


def _outproj_kernel(x_ref, xa_ref, yb_ref, yc_ref, wp_ref, ps_ref, wo_ref, o_ref, halo_sc, ext_sc):
    ts = x_ref.shape[1]
    s_idx = pl.program_id(1)

    @pl.when(s_idx == 0)
    def _():
        halo_sc[...] = jnp.zeros_like(halo_sc)

    xa = xa_ref[0]
    ext_sc[0:POOL_HALO, :] = halo_sc[...]
    ext_sc[POOL_HALO:, :] = xa
    halo_sc[...] = xa_ref[0, ts - POOL_HALO:ts, :]

    e = ext_sc[...]
    sums = []
    step = 1
    for _ in POOL_WINDOWS:
        e = e + pltpu.roll(e, step, 0)
        step *= 2
        sums.append(e[POOL_HALO:])
    lane = lax.broadcasted_iota(jnp.int32, (ts, POOL_WIDTH), 1)
    grp = lane >> 6
    win = jnp.left_shift(2, grp)
    t = s_idx * ts + lax.broadcasted_iota(jnp.int32, (ts, POOL_WIDTH), 0)
    count = jnp.minimum(t + 1, win).astype(F32)
    pooled = jnp.where(grp == 0, sums[0], jnp.where(grp == 1, sums[1], jnp.where(grp == 2, sums[2], sums[3])))
    ya = _dot((pooled / count - xa).astype(BF16), wp_ref[...]) * ps_ref[...]

    acc = x_ref[0] + _dot(ya.astype(BF16), wo_ref[0:POOL_WIDTH, :])
    acc = acc + _dot(yb_ref[0].astype(BF16), wo_ref[POOL_WIDTH:POOL_WIDTH + DN_WIDTH, :])
    acc = acc + _dot(yc_ref[0].astype(BF16), wo_ref[POOL_WIDTH + DN_WIDTH:, :])
    o_ref[0] = acc


def _outproj(x, xa, yb, yc, wp_bd, pool_scale, w_out):
    b, s, _ = x.shape
    ts = TM_PROJ

    def tok(n):
        return pl.BlockSpec((1, ts, n), lambda i, j: (i, j, 0))

    return pl.pallas_call(
        _outproj_kernel,
        grid=(b, s // ts),
        in_specs=[tok(D_MODEL), tok(POOL_WIDTH), tok(DN_WIDTH), tok(MLA_WIDTH),
                  _const_spec((POOL_WIDTH, POOL_WIDTH)), _const_spec((1, POOL_WIDTH)),
                  _const_spec((D_MODEL, D_MODEL))],
        out_specs=tok(D_MODEL),
        out_shape=jax.ShapeDtypeStruct((b, s, D_MODEL), F32),
        scratch_shapes=[pltpu.VMEM((POOL_HALO, POOL_WIDTH), F32),
                        pltpu.VMEM((ts + POOL_HALO, POOL_WIDTH), F32)],
        compiler_params=_params("parallel", "arbitrary"),
        name="outproj_pool",
    )(x, xa, yb, yc, wp_bd, pool_scale, w_out)


def _mlp_kernel(x_ref, g_ref, wu_ref, wd_ref, o_ref):
    x = x_ref[...]
    h = _rms_rows(x, g_ref[...]).astype(BF16)
    u = jnp.maximum(_dot(h, wu_ref[...]), 0.0)
    o_ref[...] = x + _dot((u * u).astype(BF16), wd_ref[...])


def _mlp(x2, gain, w_up, w_down):
    t = x2.shape[0]
    return pl.pallas_call(
        _mlp_kernel,
        grid=(t // TM_PROJ,),
        in_specs=[pl.BlockSpec((TM_PROJ, D_MODEL), lambda i: (i, 0)),
                  _const_spec((1, D_MODEL)),
                  _const_spec((D_MODEL, D_FF)),
                  _const_spec((D_FF, D_MODEL))],
        out_specs=pl.BlockSpec((TM_PROJ, D_MODEL), lambda i: (i, 0)),
        out_shape=jax.ShapeDtypeStruct((t, D_MODEL), F32),
        compiler_params=_params("parallel"),
        name="mlp",
    )(x2, gain, w_up, w_down)


def _head_block(nope, pe):
    z16 = jnp.zeros(pe.shape[:-1] + (HALF_ROPE,), pe.dtype)
    return jnp.concatenate([nope, pe[..., :HALF_ROPE], z16, pe[..., HALF_ROPE:], z16], axis=-1)


def _prep_layer(w_in, pool_w, pool_scale, dn_conv, dn_a_log, dn_dt_bias, dn_norm,
                q_a_norm, w_q_b, kv_a_norm, w_kv_b, q_norm, k_norm, w_out, w_up, w_down):
    offs = np.cumsum((0,) + IN_SPLITS)
    seg = [w_in[:, offs[i]:offs[i + 1]] for i in range(len(IN_SPLITS))]
    w_xa, w_qkv, w_z, w_b, w_a, w_cq, w_ckv, w_kpe = seg
    d = w_in.shape[0]
    z64 = jnp.zeros((d, MLA_NOPE), w_in.dtype)
    w_misc = jnp.concatenate([w_b, w_a, jnp.zeros((d, MLA_NOPE - 2 * DN_HEADS), w_in.dtype),
                              _head_block(z64, w_kpe)[:, MLA_NOPE:]], axis=1)
    w_cat = jnp.concatenate([w_xa, w_qkv, w_z, w_cq, w_ckv, w_misc], axis=1).astype(BF16)

    lane_pad = jnp.zeros((LANES - 2 * DN_HEADS,), F32)
    alog_l = jnp.concatenate([jnp.zeros((DN_HEADS,), F32), dn_a_log, lane_pad])[None]
    dtb_l = jnp.concatenate([jnp.zeros((DN_HEADS,), F32), dn_dt_bias, lane_pad])[None]

    wq = w_q_b.reshape(Q_LORA, MLA_HEADS, MLA_QK_DIM)
    wq = _head_block(wq[..., :MLA_NOPE], wq[..., MLA_NOPE:]).reshape(Q_LORA, MLA_HEADS * LANES).astype(BF16)
    wkv = w_kv_b.reshape(KV_LORA, MLA_HEADS, MLA_NOPE + MLA_V)
    wk = _head_block(wkv[..., :MLA_NOPE], jnp.zeros((KV_LORA, MLA_HEADS, MLA_ROPE), F32))
    wk = wk.reshape(KV_LORA, MLA_HEADS * LANES).astype(BF16)
    wvt = wkv[..., MLA_NOPE:].reshape(KV_LORA, MLA_WIDTH).T.astype(BF16)
    qg = _head_block(q_norm[:MLA_NOPE], q_norm[MLA_NOPE:])[None]
    kg = _head_block(k_norm[:MLA_NOPE], jnp.zeros((MLA_ROPE,), F32))[None]
    kpg = _head_block(jnp.zeros((MLA_NOPE,), F32), k_norm[MLA_NOPE:])[None]

    wp_bd = jax.scipy.linalg.block_diag(*[pool_w[g] for g in range(POOL_GROUPS)]).astype(BF16)
    return dict(w_cat=w_cat, alog_l=alog_l, dtb_l=dtb_l, conv_w=dn_conv, dn_norm=dn_norm[None],
                qag=q_a_norm[None], kvag=kv_a_norm[None], wq=wq, wk=wk, wvt=wvt, qg=qg, kg=kg, kpg=kpg,
                wp_bd=wp_bd, pool_scale=pool_scale[None], w_out=w_out.astype(BF16),
                w_up=w_up.astype(BF16), w_down=w_down.astype(BF16))


def kernel(x, positions, attn_norm, w_in, pool_w, pool_scale, dn_conv, dn_a_log, dn_dt_bias, dn_norm,
           mla_q_a_norm, mla_w_q_b, mla_kv_a_norm, mla_w_kv_b, mla_q_norm, mla_k_norm,
           w_out, mlp_norm, w_up, w_down):
    b, s, d = x.shape
    depth = w_in.shape[0]
    inv_freq = ROPE_THETA ** (-jnp.arange(0, MLA_ROPE, 2, dtype=F32) / MLA_ROPE)
    freq_lanes = _head_block(jnp.zeros((MLA_NOPE,), F32), jnp.concatenate([inv_freq, inv_freq]))[None]
    cos, sina, sinb = _rope_tables(positions, freq_lanes)

    for l in range(depth):
        p = _prep_layer(w_in[l], pool_w[l], pool_scale[l], dn_conv[l], dn_a_log[l], dn_dt_bias[l], dn_norm[l],
                        mla_q_a_norm[l], mla_w_q_b[l], mla_kv_a_norm[l], mla_w_kv_b[l], mla_q_norm[l],
                        mla_k_norm[l], w_out[l], w_up[l], w_down[l])
        xa, qkv, z, cq, ckv, misc = _inproj(x.reshape(b * s, d), attn_norm[l][None], p["w_cat"])
        r3 = lambda a: a.reshape(b, s, a.shape[-1])
        xa, qkv, z, cq, ckv, misc = map(r3, (xa, qkv, z, cq, ckv, misc))
        y_b = _deltanet(qkv, z, misc, p["conv_w"], p["alog_l"], p["dtb_l"], p["dn_norm"])
        q, k, vt = _mla_prep(cq, ckv, misc, cos, sina, sinb, p["qag"], p["kvag"], p["wq"], p["wk"], p["wvt"],
                             p["qg"], p["kg"], p["kpg"])
        y_c = _attention(q, k, vt)
        x = _outproj(x, xa, y_b, y_c, p["wp_bd"], p["pool_scale"], p["w_out"])
        x = _mlp(x.reshape(b * s, d), mlp_norm[l][None], p["w_up"], p["w_down"]).reshape(b, s, d)
    return x
```

```python
import functools
import math

import jax
import jax.numpy as jnp
import numpy as np
from jax import lax
from jax.experimental import pallas as pl
from jax.experimental.pallas import tpu as pltpu

F32 = jnp.float32
BF16 = jnp.bfloat16

D_MODEL = 1024
POOL_GROUPS = 4
POOL_GROUP_DIM = 64
POOL_WIDTH = POOL_GROUPS * POOL_GROUP_DIM
POOL_WINDOWS = (2, 4, 8, 16)
DN_HEADS = 4
DN_HEAD_DIM = 128
DN_WIDTH = DN_HEADS * DN_HEAD_DIM
DN_CONV = 4
DN_CHUNK = 64
MLA_HEADS = 4
MLA_NOPE = 64
MLA_ROPE = 32
MLA_QK_DIM = MLA_NOPE + MLA_ROPE
MLA_V = 64
MLA_WIDTH = MLA_HEADS * MLA_V
Q_LORA = 256
KV_LORA = 128
ROPE_THETA = 10000.0
D_FF = 4 * D_MODEL
EPS = 1e-6
IN_SPLITS = (POOL_WIDTH, 3 * DN_WIDTH, DN_WIDTH, DN_HEADS, DN_HEADS, Q_LORA, KV_LORA, MLA_ROPE)

LANES = 128
SUBLANES = 8
VMEM_LIMIT_BYTES = 56 * 1024 * 1024

TM_PROJ = 512
TM_PREP = 512
DN_TILE = 256
DN_PAIR = 2 * DN_CHUNK
TQ = 512
TK = 1024
HALF_ROPE = MLA_ROPE // 2
SCORE_BOUND_SLACK = 1.02
SCORE_BOUND_LIMIT = 30.0
PE1_LANE = MLA_NOPE
PE2_LANE = MLA_NOPE + 2 * HALF_ROPE


def _dot(a, b):
    return jnp.dot(a, b, preferred_element_type=F32)


def _dot_nt(a, b):
    return lax.dot_general(a, b, (((1,), (1,)), ((), ())), preferred_element_type=F32)


def _rms_rows(x, gain):
    return x * lax.rsqrt(jnp.mean(x * x, axis=-1, keepdims=True) + EPS) * gain


def _split3(x):
    x1 = x.astype(BF16)
    r1 = x - x1.astype(F32)
    x2 = r1.astype(BF16)
    r2 = r1 - x2.astype(F32)
    return x1, x2, r2.astype(BF16)


def _sigmoid(x):
    return 1.0 / (1.0 + jnp.exp(-x))


def _params(*sem):
    return pltpu.CompilerParams(dimension_semantics=sem, vmem_limit_bytes=VMEM_LIMIT_BYTES)


def _const_spec(shape):
    nd = len(shape)
    return pl.BlockSpec(shape, lambda *_: (0,) * nd, pipeline_mode=pl.Buffered(1))


IN_SEGS = (POOL_WIDTH, 3 * DN_WIDTH, DN_WIDTH, Q_LORA, KV_LORA, LANES)


def _inproj_kernel(x_ref, g_ref, w_ref, *o_refs):
    h = _rms_rows(x_ref[...], g_ref[...]).astype(BF16)
    off = 0
    for o_ref in o_refs:
        n = o_ref.shape[-1]
        o_ref[...] = _dot(h, w_ref[:, off:off + n])
        off += n


def _inproj(x2, gain, w_cat):
    t = x2.shape[0]
    n_all = sum(IN_SEGS)
    return pl.pallas_call(
        _inproj_kernel,
        grid=(t // TM_PROJ,),
        in_specs=[pl.BlockSpec((TM_PROJ, D_MODEL), lambda i: (i, 0)),
                  _const_spec((1, D_MODEL)),
                  _const_spec((D_MODEL, n_all))],
        out_specs=[pl.BlockSpec((TM_PROJ, n), lambda i: (i, 0)) for n in IN_SEGS],
        out_shape=[jax.ShapeDtypeStruct((t, n), F32) for n in IN_SEGS],
        compiler_params=_params("parallel"),
        name="inproj",
    )(x2, gain, w_cat)


CONV_HALO = SUBLANES


def _dn_kernel(qkv_ref, z_ref, misc_ref, cw_ref, alog_ref, dtb_ref, ng_ref, y_ref,
               state_sc, halo_sc, ext_sc):
    nb = qkv_ref.shape[0]
    L = DN_TILE
    C = DN_CHUNK
    nchunk = L // C
    D = DN_HEAD_DIM
    s_idx = pl.program_id(0)
    chains = [(b, h) for b in range(nb) for h in range(DN_HEADS)]

    @pl.when(s_idx == 0)
    def _():
        state_sc[...] = jnp.zeros_like(state_sc)
        halo_sc[...] = jnp.zeros_like(halo_sc)

    for b in range(nb):
        ext_sc[b, 0:CONV_HALO, :] = halo_sc[b]
        ext_sc[b, CONV_HALO:, :] = qkv_ref[b]
        halo_sc[b] = qkv_ref[b, L - CONV_HALO:L, :]

    def chunk_masks(n):
        ri = lax.broadcasted_iota(jnp.int32, (n, n), 0)
        ci = lax.broadcasted_iota(jnp.int32, (n, n), 1)
        same = (ri >> 6) == (ci >> 6)
        return same, same & (ci <= ri), same & (ci < ri), ri == ci

    same_l, causal_l, _, _ = chunk_masks(L)
    cum_mat = jnp.concatenate([causal_l.astype(BF16), same_l.astype(BF16)], axis=0)
    P = DN_PAIR
    nblk = L // P
    _, causal_bd, strict_bd, diag = chunk_masks(P)
    eye = diag.astype(F32)

    beta_all, gcum_all, glast_all, gcum_t = [], [], [], []
    for b in range(nb):
        misc = misc_ref[b]
        beta_all.append(_sigmoid(misc))
        sp_in = misc + dtb_ref[...]
        softplus = jnp.maximum(sp_in, 0.0) + jnp.log1p(jnp.exp(-jnp.abs(sp_in)))
        g_all = -jnp.exp(alog_ref[...]) * softplus
        g1, g2, g3 = _split3(g_all)
        cum = _dot(cum_mat, g1) + _dot(cum_mat, g2) + _dot(cum_mat, g3)
        gcum_all.append(cum[:L])
        glast_all.append(cum[L:])
        gcum_t.append(cum[:L].T)

    def conv_part(b, h, p):
        c0 = p * DN_WIDTH + h * D
        e = ext_sc[b, :, c0:c0 + D]
        cw = cw_ref[:, c0:c0 + D]
        acc = cw[DN_CONV - 1:DN_CONV] * e
        for j in range(1, DN_CONV):
            acc = acc + cw[DN_CONV - 1 - j:DN_CONV - j] * pltpu.roll(e, j, 0)
        acc = acc[CONV_HALO:]
        return acc * _sigmoid(acc)

    def stage1(b, h):
        a_mats, attns = [], []
        lane = DN_HEADS + h
        gc_col = gcum_all[b][:, lane:lane + 1]
        gl_col = glast_all[b][:, lane:lane + 1]
        gc_row = gcum_t[b][lane:lane + 1, :]
        beta = beta_all[b][:, h:h + 1]
        q = conv_part(b, h, 0)
        k = conv_part(b, h, 1)
        v = conv_part(b, h, 2)
        qn = q * lax.rsqrt(jnp.sum(q * q, axis=-1, keepdims=True) + EPS) * (D ** -0.5)
        kn = k * lax.rsqrt(jnp.sum(k * k, axis=-1, keepdims=True) + EPS)
        kb = kn * beta
        e_col = jnp.exp(gc_col)
        kn16 = kn.astype(BF16)
        kb16 = kb.astype(BF16)
        qn16 = qn.astype(BF16)
        for j in range(nblk):
            rows = slice(j * P, (j + 1) * P)
            decay = jnp.exp(jnp.where(causal_bd, gc_col[rows] - gc_row[:, rows], 0.0))
            a_mats.append(jnp.where(strict_bd, _dot_nt(kb16[rows], kn16[rows]) * decay, 0.0))
            attns.append(jnp.where(causal_bd, _dot_nt(qn16[rows], kn16[rows]) * decay, 0.0).astype(BF16))
        return dict(a=a_mats, attn=attns,
                    rhs=jnp.concatenate([v * beta, kb * e_col], axis=1).astype(BF16),
                    qd=(qn * e_col).astype(BF16),
                    kdt=(kn * jnp.exp(gl_col - gc_col)).T.astype(BF16),
                    gdec=jnp.exp(jnp.broadcast_to(gl_col, (L, D))))

    def stage2(group):
        a_mats = [a for ch in group for a in ch["a"]]
        xps = [(-a).astype(BF16) for a in a_mats]
        t_invs = [eye - a for a in a_mats]
        for _ in range(5):
            xps = [_dot(xp, xp).astype(BF16) for xp in xps]
            t_invs = [t + _dot(t.astype(BF16), xp) for t, xp in zip(t_invs, xps)]
        for i, ch in enumerate(group):
            ch["uw"] = [_dot(t_invs[i * nblk + j].astype(BF16), ch["rhs"][j * P:(j + 1) * P])
                        for j in range(nblk)]

    groups = []
    for b in range(nb):
        if groups:
            stage2(groups[-1])
        groups.append([stage1(b, h) for h in range(DN_HEADS)])
    stage2(groups[-1])
    chs = [ch for group in groups for ch in group]

    states = [state_sc[b, h] for b, h in chains]
    o_parts = [[] for _ in chains]
    zeros_c = jnp.zeros((C, D), BF16)
    for c in range(nchunk):
        r0 = c * C
        j, half = divmod(c, P // C)
        p0 = half * C
        rs = [_dot(jnp.concatenate([ch["uw"][j][p0:p0 + C, D:].astype(BF16), ch["qd"][r0:r0 + C]], axis=0),
                   states[i].astype(BF16)) for i, ch in enumerate(chs)]
        for i, ch in enumerate(chs):
            v_new = (ch["uw"][j][p0:p0 + C, :D] - rs[i][:C]).astype(BF16)
            v_blk = jnp.concatenate([zeros_c] * half + [v_new] + [zeros_c] * (P // C - 1 - half), axis=0)
            lhs = jnp.concatenate([ch["attn"][j][p0:p0 + C, :], ch["kdt"][:, j * P:(j + 1) * P]], axis=0)
            m2 = _dot(lhs, v_blk)
            o_parts[i].append(rs[i][C:] + m2[:C])
            states[i] = states[i] * ch["gdec"][r0:r0 + 1, :] + m2[C:]

    for i, (b, h) in enumerate(chains):
        state_sc[b, h] = states[i]
        o = jnp.concatenate(o_parts[i], axis=0)
        zh = z_ref[b, :, h * D:(h + 1) * D]
        y_ref[b, :, h * D:(h + 1) * D] = _rms_rows(o, ng_ref[...]) * (zh * _sigmoid(zh))


def _deltanet(qkv, z, misc, conv_w, alog_l, dtb_l, norm_gain):
    b, s, _ = qkv.shape
    L = DN_TILE
    return pl.pallas_call(
        _dn_kernel,
        grid=(s // L,),
        in_specs=[pl.BlockSpec((b, L, 3 * DN_WIDTH), lambda j: (0, j, 0)),
                  pl.BlockSpec((b, L, DN_WIDTH), lambda j: (0, j, 0)),
                  pl.BlockSpec((b, L, LANES), lambda j: (0, j, 0)),
                  _const_spec((DN_CONV, 3 * DN_WIDTH)),
                  _const_spec((1, LANES)),
                  _const_spec((1, LANES)),
                  _const_spec((1, DN_HEAD_DIM))],
        out_specs=pl.BlockSpec((b, L, DN_WIDTH), lambda j: (0, j, 0)),
        out_shape=jax.ShapeDtypeStruct((b, s, DN_WIDTH), F32),
        scratch_shapes=[pltpu.VMEM((b, DN_HEADS, DN_HEAD_DIM, DN_HEAD_DIM), F32),
                        pltpu.VMEM((b, CONV_HALO, 3 * DN_WIDTH), F32),
                        pltpu.VMEM((b, L + CONV_HALO, 3 * DN_WIDTH), F32)],
        compiler_params=_params("arbitrary"),
        name="deltanet",
    )(qkv, z, misc, conv_w, alog_l, dtb_l, norm_gain)


def _rope_kernel(pos_ref, freq_ref, cos_ref, sina_ref, sinb_ref):
    tm = pos_ref.shape[1]
    ang = pos_ref[0].astype(F32) * freq_ref[...]
    lane = lax.broadcasted_iota(jnp.int32, (tm, LANES), 1)
    pe1 = (lane >= PE1_LANE) & (lane < PE1_LANE + HALF_ROPE)
    pe2 = (lane >= PE2_LANE) & (lane < PE2_LANE + HALF_ROPE)
    cos = jnp.cos(ang)
    sin = jnp.sin(ang)
    cos_ref[0] = jnp.where(lane < MLA_NOPE, 1.0, jnp.where(pe1 | pe2, cos, 0.0))
    sina_ref[0] = jnp.where(pe2, sin, 0.0)
    sinb_ref[0] = jnp.where(pe1, -sin, 0.0)


def _rope_tables(positions, freq_lanes):
    b, s = positions.shape
    tm = TM_PREP
    spec = pl.BlockSpec((1, tm, LANES), lambda i, j: (i, j, 0))
    return pl.pallas_call(
        _rope_kernel,
        grid=(b, s // tm),
        in_specs=[pl.BlockSpec((1, tm, 1), lambda i, j: (i, j, 0)), _const_spec((1, LANES))],
        out_specs=[spec, spec, spec],
        out_shape=[jax.ShapeDtypeStruct((b, s, LANES), F32)] * 3,
        compiler_params=_params("parallel", "parallel"),
        name="rope_tables",
    )(positions.reshape(b, s, 1), freq_lanes)


def _rope(x, cos, sina, sinb):
    return x * cos + pltpu.roll(x, 2 * HALF_ROPE, 1) * sina + pltpu.roll(x, LANES - 2 * HALF_ROPE, 1) * sinb


def _mla_prep_kernel(cq_ref, ckv_ref, misc_ref, cos_ref, sina_ref, sinb_ref,
                     qag_ref, kvag_ref, wq_ref, wk_ref, wvt_ref, qg_ref, kg_ref, kpg_ref,
                     q_ref, k_ref, vt_ref, qsq_ref, ksq_ref):
    tm = cq_ref.shape[1]
    ones8 = jnp.ones((SUBLANES, LANES), BF16)

    def row_sq_norms(x16):
        xf = x16.astype(F32)
        return _dot_nt(ones8, (xf * xf).astype(BF16))

    cos = cos_ref[0]
    sina = sina_ref[0]
    sinb = sinb_ref[0]
    lane = lax.broadcasted_iota(jnp.int32, (tm, LANES), 1)
    is_nope = lane < MLA_NOPE

    cqn = _rms_rows(cq_ref[0], qag_ref[...]).astype(BF16)
    ckvn = _rms_rows(ckv_ref[0], kvag_ref[...]).astype(BF16)
    qf = _dot(cqn, wq_ref[...])
    kf = _dot(ckvn, wk_ref[...])
    vt_ref[0] = _dot_nt(wvt_ref[...], ckvn).astype(BF16)

    kp = jnp.where(is_nope, 0.0, misc_ref[0])
    kp_ms = jnp.sum(kp * kp, axis=-1, keepdims=True) * (1.0 / MLA_ROPE)
    kp = _rope(kp * lax.rsqrt(kp_ms + EPS) * kpg_ref[...], cos, sina, sinb)

    scale = MLA_QK_DIM ** -0.5 * math.log2(math.e)
    for h in range(MLA_HEADS):
        xq = qf[:, h * LANES:(h + 1) * LANES]
        sq = xq * xq
        ms_n = jnp.sum(jnp.where(is_nope, sq, 0.0), axis=-1, keepdims=True) * (1.0 / MLA_NOPE)
        ms_p = jnp.sum(jnp.where(is_nope, 0.0, sq), axis=-1, keepdims=True) * (1.0 / MLA_ROPE)
        inv = jnp.where(is_nope, lax.rsqrt(ms_n + EPS), lax.rsqrt(ms_p + EPS))
        qh = _rope(xq * inv * qg_ref[...], cos, sina, sinb) * scale
        q16 = qh.astype(BF16)
        q_ref[0, h] = q16
        qsq_ref[0, h] = row_sq_norms(q16)

        xk = kf[:, h * LANES:(h + 1) * LANES]
        ms_k = jnp.sum(xk * xk, axis=-1, keepdims=True) * (1.0 / MLA_NOPE)
        kh = xk * lax.rsqrt(ms_k + EPS) * kg_ref[...] + kp
        k16 = kh.astype(BF16)
        k_ref[0, h] = k16
        ksq_ref[0, h] = row_sq_norms(k16)


def _mla_prep(cq, ckv, misc, cos, sina, sinb, qag, kvag, wq, wk, wvt, qg, kg, kpg):
    b, s, _ = cq.shape
    tm = TM_PREP
    hl = MLA_HEADS * LANES

    def tok(n):
        return pl.BlockSpec((1, tm, n), lambda i, j: (i, j, 0))

    return pl.pallas_call(
        _mla_prep_kernel,
        grid=(b, s // tm),
        in_specs=[tok(Q_LORA), tok(KV_LORA), tok(LANES), tok(LANES), tok(LANES), tok(LANES),
                  _const_spec((1, Q_LORA)), _const_spec((1, KV_LORA)),
                  _const_spec((Q_LORA, hl)), _const_spec((KV_LORA, hl)), _const_spec((MLA_WIDTH, KV_LORA)),
                  _const_spec((1, LANES)), _const_spec((1, LANES)), _const_spec((1, LANES))],
        out_specs=[pl.BlockSpec((1, MLA_HEADS, tm, LANES), lambda i, j: (i, 0, j, 0)),
                   pl.BlockSpec((1, MLA_HEADS, tm, LANES), lambda i, j: (i, 0, j, 0)),
                   pl.BlockSpec((1, MLA_WIDTH, tm), lambda i, j: (i, 0, j)),
                   pl.BlockSpec((1, MLA_HEADS, SUBLANES, tm), lambda i, j: (i, 0, 0, j)),
                   pl.BlockSpec((1, MLA_HEADS, SUBLANES, tm), lambda i, j: (i, 0, 0, j))],
        out_shape=[jax.ShapeDtypeStruct((b, MLA_HEADS, s, LANES), BF16),
                   jax.ShapeDtypeStruct((b, MLA_HEADS, s, LANES), BF16),
                   jax.ShapeDtypeStruct((b, MLA_WIDTH, s), BF16),
                   jax.ShapeDtypeStruct((b, MLA_HEADS, SUBLANES, s), F32),
                   jax.ShapeDtypeStruct((b, MLA_HEADS, SUBLANES, s), F32)],
        compiler_params=_params("parallel", "parallel"),
        name="mla_prep",
    )(cq, ckv, misc, cos, sina, sinb, qag, kvag, wq, wk, wvt, qg, kg, kpg)


def _bounds_kernel(qsq_ref, ksq_ref, u_ref, flag_ref):
    worst = None
    for h in range(MLA_HEADS):
        kmax = jnp.max(ksq_ref[0, h], axis=-1, keepdims=True)
        u = jnp.sqrt(qsq_ref[0, h] * kmax) * SCORE_BOUND_SLACK
        u_ref[0, h] = u
        umax = jnp.max(u, axis=-1, keepdims=True)
        worst = umax if worst is None else jnp.maximum(worst, umax)
    flag_ref[0] = jnp.broadcast_to((worst <= SCORE_BOUND_LIMIT).astype(jnp.int32), flag_ref.shape[1:])


def _score_bounds(qsq, ksq):
    b, h, r, s = qsq.shape
    spec = pl.BlockSpec((1, h, r, s), lambda i: (i, 0, 0, 0))
    return pl.pallas_call(
        _bounds_kernel,
        grid=(b,),
        in_specs=[spec, spec],
        out_specs=[spec, pl.BlockSpec((1, SUBLANES, LANES), lambda i: (i, 0, 0))],
        out_shape=[jax.ShapeDtypeStruct((b, h, r, s), F32), jax.ShapeDtypeStruct((b, SUBLANES, LANES), jnp.int32)],
        compiler_params=_params("parallel"),
        name="score_bounds",
    )(qsq, ksq)


def _attn_block(q_ref, k_ref, vt_ref, u_ref, m_sc, l_sc, acc_sc, visible, bounded):
    scores = [_dot_nt(k_ref[0, 0], q_ref[0, 0])]
    for h in range(MLA_HEADS):
        if h + 1 < MLA_HEADS:
            scores.append(_dot_nt(k_ref[0, h + 1], q_ref[0, h + 1]))
        st = scores[h]
        if visible is not None:
            st = jnp.where(visible, st, -jnp.inf)
        vt = vt_ref[0, h * MLA_V:(h + 1) * MLA_V, :]
        if bounded:
            pt = jnp.exp2(st - u_ref[0, h, 0:1, :])
            l_sc[h] = l_sc[h] + jnp.sum(pt, axis=0, keepdims=True)
            acc_sc[h] = acc_sc[h] + _dot(vt, pt.astype(BF16))
        else:
            m_prev = m_sc[h]
            m_new = jnp.maximum(m_prev, jnp.max(st, axis=0, keepdims=True))
            alpha = jnp.exp2(m_prev - m_new)
            pt = jnp.exp2(st - m_new)
            l_sc[h] = alpha * l_sc[h] + jnp.sum(pt, axis=0, keepdims=True)
            acc_sc[h] = alpha * acc_sc[h] + _dot(vt, pt.astype(BF16))
            m_sc[h] = m_new


def _attn_kernel(qi_ref, kj_ref, last_ref, flag_ref, q_ref, k_ref, vt_ref, u_ref, o_ref, m_sc, l_sc, acc_sc):
    p = pl.program_id(1)
    qi = qi_ref[p]
    kj = kj_ref[p]
    bounded = flag_ref[pl.program_id(0)] == 1

    @pl.when(kj == 0)
    def _():
        m_sc[...] = jnp.full_like(m_sc, -jnp.inf)
        l_sc[...] = jnp.zeros_like(l_sc)
        acc_sc[...] = jnp.zeros_like(acc_sc)

    all_visible = kj * TK + (TK - 1) <= qi * TQ

    def run(visible_fn, use_bound):
        def body():
            _attn_block(q_ref, k_ref, vt_ref, u_ref, m_sc, l_sc, acc_sc, visible_fn(), use_bound)
        return body

    def causal_mask():
        kpos = kj * TK + lax.broadcasted_iota(jnp.int32, (TK, TQ), 0)
        qpos = qi * TQ + lax.broadcasted_iota(jnp.int32, (TK, TQ), 1)
        return kpos <= qpos

    partly = jnp.logical_not(all_visible)
    unbounded = jnp.logical_not(bounded)
    pl.when(all_visible & bounded)(run(lambda: None, True))
    pl.when(partly & bounded)(run(causal_mask, True))
    pl.when(all_visible & unbounded)(run(lambda: None, False))
    pl.when(partly & unbounded)(run(causal_mask, False))

    @pl.when(last_ref[p] == 1)
    def _():
        out_t = jnp.concatenate([acc_sc[h] / l_sc[h] for h in range(MLA_HEADS)], axis=0)
        o_ref[0] = out_t.T


def _attention(q, k, vt, u, flag):
    b, _, s, _ = q.shape
    nq = s // TQ
    pairs = [(i, j) for i in range(nq) for j in range((i * TQ + TQ - 1) // TK + 1)]
    qi = jnp.asarray(np.array([p[0] for p in pairs], np.int32))
    kj = jnp.asarray(np.array([p[1] for p in pairs], np.int32))
    last = jnp.asarray(np.array([int(p[1] == (p[0] * TQ + TQ - 1) // TK) for p in pairs], np.int32))
    grid_spec = pltpu.PrefetchScalarGridSpec(
        num_scalar_prefetch=4,
        grid=(b, len(pairs)),
        in_specs=[pl.BlockSpec((1, MLA_HEADS, TQ, LANES), lambda i, p, qi, kj, *_: (i, 0, qi[p], 0)),
                  pl.BlockSpec((1, MLA_HEADS, TK, LANES), lambda i, p, qi, kj, *_: (i, 0, kj[p], 0)),
                  pl.BlockSpec((1, MLA_WIDTH, TK), lambda i, p, qi, kj, *_: (i, 0, kj[p])),
                  pl.BlockSpec((1, MLA_HEADS, SUBLANES, TQ), lambda i, p, qi, kj, *_: (i, 0, 0, qi[p]))],
        out_specs=pl.BlockSpec((1, TQ, MLA_WIDTH), lambda i, p, qi, kj, *_: (i, qi[p], 0)),
        scratch_shapes=[pltpu.VMEM((MLA_HEADS, 1, TQ), F32),
                        pltpu.VMEM((MLA_HEADS, 1, TQ), F32),
                        pltpu.VMEM((MLA_HEADS, MLA_V, TQ), F32)],
    )
    return pl.pallas_call(
        _attn_kernel,
        grid_spec=grid_spec,
        out_shape=jax.ShapeDtypeStruct((b, s, MLA_WIDTH), F32),
        compiler_params=_params("parallel", "arbitrary"),
        name="mla_attention",
    )(qi, kj, last, flag, q, k, vt, u)


POOL_HALO = 16


def _outproj_kernel(x_ref, xa_ref, yb_ref, yc_ref, wp_ref, ps_ref, wo_ref, o_ref, halo_sc, ext_sc):
    ts = x_ref.shape[1]
    s_idx = pl.program_id(1)

    @pl.when(s_idx == 0)
    def _():
        halo_sc[...] = jnp.zeros_like(halo_sc)

    xa = xa_ref[0]
    ext_sc[0:POOL_HALO, :] = halo_sc[...]
    ext_sc[POOL_HALO:, :] = xa
    halo_sc[...] = xa_ref[0, ts - POOL_HALO:ts, :]

    e = ext_sc[...]
    sums = []
    step = 1
    for _ in POOL_WINDOWS:
        e = e + pltpu.roll(e, step, 0)
        step *= 2
        sums.append(e[POOL_HALO:])
    lane = lax.broadcasted_iota(jnp.int32, (ts, POOL_WIDTH), 1)
    grp = lane >> 6
    win = jnp.left_shift(2, grp)
    t = s_idx * ts + lax.broadcasted_iota(jnp.int32, (ts, POOL_WIDTH), 0)
    count = jnp.minimum(t + 1, win).astype(F32)
    pooled = jnp.where(grp == 0, sums[0], jnp.where(grp == 1, sums[1], jnp.where(grp == 2, sums[2], sums[3])))
    ya = _dot((pooled / count - xa).astype(BF16), wp_ref[...]) * ps_ref[...]

    acc = x_ref[0] + _dot(ya.astype(BF16), wo_ref[0:POOL_WIDTH, :])
    acc = acc + _dot(yb_ref[0].astype(BF16), wo_ref[POOL_WIDTH:POOL_WIDTH + DN_WIDTH, :])
    acc = acc + _dot(yc_ref[0].astype(BF16), wo_ref[POOL_WIDTH + DN_WIDTH:, :])
    o_ref[0] = acc


def _outproj(x, xa, yb, yc, wp_bd, pool_scale, w_out):
    b, s, _ = x.shape
    ts = TM_PROJ

    def tok(n):
        return pl.BlockSpec((1, ts, n), lambda i, j: (i, j, 0))

    return pl.pallas_call(
        _outproj_kernel,
        grid=(b, s // ts),
        in_specs=[tok(D_MODEL), tok(POOL_WIDTH), tok(DN_WIDTH), tok(MLA_WIDTH),
                  _const_spec((POOL_WIDTH, POOL_WIDTH)), _const_spec((1, POOL_WIDTH)),
                  _const_spec((D_MODEL, D_MODEL))],
        out_specs=tok(D_MODEL),
        out_shape=jax.ShapeDtypeStruct((b, s, D_MODEL), F32),
        scratch_shapes=[pltpu.VMEM((POOL_HALO, POOL_WIDTH), F32),
                        pltpu.VMEM((ts + POOL_HALO, POOL_WIDTH), F32)],
        compiler_params=_params("parallel", "arbitrary"),
        name="outproj_pool",
    )(x, xa, yb, yc, wp_bd, pool_scale, w_out)


def _mlp_kernel(x_ref, g_ref, wu_ref, wd_ref, o_ref):
    x = x_ref[...]
    h = _rms_rows(x, g_ref[...]).astype(BF16)
    u = jnp.maximum(_dot(h, wu_ref[...]), 0.0)
    o_ref[...] = x + _dot((u * u).astype(BF16), wd_ref[...])


def _mlp(x2, gain, w_up, w_down):
    t = x2.shape[0]
    return pl.pallas_call(
        _mlp_kernel,
        grid=(t // TM_PROJ,),
        in_specs=[pl.BlockSpec((TM_PROJ, D_MODEL), lambda i: (i, 0)),
                  _const_spec((1, D_MODEL)),
                  _const_spec((D_MODEL, D_FF)),
                  _const_spec((D_FF, D_MODEL))],
        out_specs=pl.BlockSpec((TM_PROJ, D_MODEL), lambda i: (i, 0)),
        out_shape=jax.ShapeDtypeStruct((t, D_MODEL), F32),
        compiler_params=_params("parallel"),
        name="mlp",
    )(x2, gain, w_up, w_down)


def _head_block(nope, pe):
    z16 = jnp.zeros(pe.shape[:-1] + (HALF_ROPE,), pe.dtype)
    return jnp.concatenate([nope, pe[..., :HALF_ROPE], z16, pe[..., HALF_ROPE:], z16], axis=-1)


def _prep_layer(w_in, pool_w, pool_scale, dn_conv, dn_a_log, dn_dt_bias, dn_norm,
                q_a_norm, w_q_b, kv_a_norm, w_kv_b, q_norm, k_norm, w_out, w_up, w_down):
    offs = np.cumsum((0,) + IN_SPLITS)
    seg = [w_in[:, offs[i]:offs[i + 1]] for i in range(len(IN_SPLITS))]
    w_xa, w_qkv, w_z, w_b, w_a, w_cq, w_ckv, w_kpe = seg
    d = w_in.shape[0]
    z64 = jnp.zeros((d, MLA_NOPE), w_in.dtype)
    w_misc = jnp.concatenate([w_b, w_a, jnp.zeros((d, MLA_NOPE - 2 * DN_HEADS), w_in.dtype),
                              _head_block(z64, w_kpe)[:, MLA_NOPE:]], axis=1)
    w_cat = jnp.concatenate([w_xa, w_qkv, w_z, w_cq, w_ckv, w_misc], axis=1).astype(BF16)

    lane_pad = jnp.zeros((LANES - 2 * DN_HEADS,), F32)
    alog_l = jnp.concatenate([jnp.zeros((DN_HEADS,), F32), dn_a_log, lane_pad])[None]
    dtb_l = jnp.concatenate([jnp.zeros((DN_HEADS,), F32), dn_dt_bias, lane_pad])[None]

    wq = w_q_b.reshape(Q_LORA, MLA_HEADS, MLA_QK_DIM)
    wq = _head_block(wq[..., :MLA_NOPE], wq[..., MLA_NOPE:]).reshape(Q_LORA, MLA_HEADS * LANES).astype(BF16)
    wkv = w_kv_b.reshape(KV_LORA, MLA_HEADS, MLA_NOPE + MLA_V)
    wk = _head_block(wkv[..., :MLA_NOPE], jnp.zeros((KV_LORA, MLA_HEADS, MLA_ROPE), F32))
    wk = wk.reshape(KV_LORA, MLA_HEADS * LANES).astype(BF16)
    wvt = wkv[..., MLA_NOPE:].reshape(KV_LORA, MLA_WIDTH).T.astype(BF16)
    qg = _head_block(q_norm[:MLA_NOPE], q_norm[MLA_NOPE:])[None]
    kg = _head_block(k_norm[:MLA_NOPE], jnp.zeros((MLA_ROPE,), F32))[None]
    kpg = _head_block(jnp.zeros((MLA_NOPE,), F32), k_norm[MLA_NOPE:])[None]

    wp_bd = jax.scipy.linalg.block_diag(*[pool_w[g] for g in range(POOL_GROUPS)]).astype(BF16)
    return dict(w_cat=w_cat, alog_l=alog_l, dtb_l=dtb_l, conv_w=dn_conv, dn_norm=dn_norm[None],
                qag=q_a_norm[None], kvag=kv_a_norm[None], wq=wq, wk=wk, wvt=wvt, qg=qg, kg=kg, kpg=kpg,
                wp_bd=wp_bd, pool_scale=pool_scale[None], w_out=w_out.astype(BF16),
                w_up=w_up.astype(BF16), w_down=w_down.astype(BF16))


def kernel(x, positions, attn_norm, w_in, pool_w, pool_scale, dn_conv, dn_a_log, dn_dt_bias, dn_norm,
           mla_q_a_norm, mla_w_q_b, mla_kv_a_norm, mla_w_kv_b, mla_q_norm, mla_k_norm,
           w_out, mlp_norm, w_up, w_down):
    b, s, d = x.shape
    depth = w_in.shape[0]
    inv_freq = ROPE_THETA ** (-jnp.arange(0, MLA_ROPE, 2, dtype=F32) / MLA_ROPE)
    freq_lanes = _head_block(jnp.zeros((MLA_NOPE,), F32), jnp.concatenate([inv_freq, inv_freq]))[None]
    cos, sina, sinb = _rope_tables(positions, freq_lanes)

    for l in range(depth):
        p = _prep_layer(w_in[l], pool_w[l], pool_scale[l], dn_conv[l], dn_a_log[l], dn_dt_bias[l], dn_norm[l],
                        mla_q_a_norm[l], mla_w_q_b[l], mla_kv_a_norm[l], mla_w_kv_b[l], mla_q_norm[l],
                        mla_k_norm[l], w_out[l], w_up[l], w_down[l])
        xa, qkv, z, cq, ckv, misc = _inproj(x.reshape(b * s, d), attn_norm[l][None], p["w_cat"])
        r3 = lambda a: a.reshape(b, s, a.shape[-1])
        xa, qkv, z, cq, ckv, misc = map(r3, (xa, qkv, z, cq, ckv, misc))
        y_b = _deltanet(qkv, z, misc, p["conv_w"], p["alog_l"], p["dtb_l"], p["dn_norm"])
        q, k, vt, qsq, ksq = _mla_prep(cq, ckv, misc, cos, sina, sinb, p["qag"], p["kvag"], p["wq"], p["wk"],
                                       p["wvt"], p["qg"], p["kg"], p["kpg"])
        u, flag = _score_bounds(qsq, ksq)
        y_c = _attention(q, k, vt, u, flag[:, 0, 0])
        x = _outproj(x, xa, y_b, y_c, p["wp_bd"], p["pool_scale"], p["w_out"])
        x = _mlp(x.reshape(b * s, d), mlp_norm[l][None], p["w_up"], p["w_down"]).reshape(b, s, d)
    return x
```

```python
import functools
import math

import jax
import jax.numpy as jnp
import numpy as np
from jax import lax
from jax.experimental import pallas as pl
from jax.experimental.pallas import tpu as pltpu

F32 = jnp.float32
BF16 = jnp.bfloat16

D_MODEL = 1024
POOL_GROUPS = 4
POOL_GROUP_DIM = 64
POOL_WIDTH = POOL_GROUPS * POOL_GROUP_DIM
POOL_WINDOWS = (2, 4, 8, 16)
DN_HEADS = 4
DN_HEAD_DIM = 128
DN_WIDTH = DN_HEADS * DN_HEAD_DIM
DN_CONV = 4
DN_CHUNK = 64
MLA_HEADS = 4
MLA_NOPE = 64
MLA_ROPE = 32
MLA_QK_DIM = MLA_NOPE + MLA_ROPE
MLA_V = 64
MLA_WIDTH = MLA_HEADS * MLA_V
Q_LORA = 256
KV_LORA = 128
ROPE_THETA = 10000.0
D_FF = 4 * D_MODEL
EPS = 1e-6
IN_SPLITS = (POOL_WIDTH, 3 * DN_WIDTH, DN_WIDTH, DN_HEADS, DN_HEADS, Q_LORA, KV_LORA, MLA_ROPE)

LANES = 128
SUBLANES = 8
MXU_COLS = 256
VMEM_LIMIT_BYTES = 56 * 1024 * 1024

TM_PROJ = 512
TM_PREP = 512
DN_TILE = 256
DN_PAIR = 2 * DN_CHUNK
TQ = 1024
TK = 1024
HALF_ROPE = MLA_ROPE // 2
SCORE_BOUND_SLACK = 1.02
SCORE_BOUND_LIMIT = 30.0
PE1_LANE = MLA_NOPE
PE2_LANE = MLA_NOPE + 2 * HALF_ROPE


def _dot(a, b):
    return jnp.dot(a, b, preferred_element_type=F32)


def _dot_nt(a, b):
    return lax.dot_general(a, b, (((1,), (1,)), ((), ())), preferred_element_type=F32)


def _rms_rows(x, gain):
    return x * lax.rsqrt(jnp.mean(x * x, axis=-1, keepdims=True) + EPS) * gain


def _split3(x):
    x1 = x.astype(BF16)
    r1 = x - x1.astype(F32)
    x2 = r1.astype(BF16)
    r2 = r1 - x2.astype(F32)
    return x1, x2, r2.astype(BF16)


def _sigmoid(x):
    return 1.0 / (1.0 + jnp.exp(-x))


def _params(*sem):
    return pltpu.CompilerParams(dimension_semantics=sem, vmem_limit_bytes=VMEM_LIMIT_BYTES)


def _const_spec(shape):
    nd = len(shape)
    return pl.BlockSpec(shape, lambda *_: (0,) * nd, pipeline_mode=pl.Buffered(1))


IN_SEGS = (3 * DN_WIDTH, POOL_WIDTH, DN_WIDTH, Q_LORA, KV_LORA, LANES)
CONV_HALO = SUBLANES


def _inproj_kernel(x_ref, g_ref, w_ref, cw_ref, qkv_ref, *rest):
    o_refs, (halo_sc, ext_sc) = rest[:-2], rest[-2:]
    tm = x_ref.shape[1]
    nqkv = 3 * DN_WIDTH
    D = DN_HEAD_DIM

    @pl.when(pl.program_id(1) == 0)
    def _():
        halo_sc[...] = jnp.zeros_like(halo_sc)

    h = _rms_rows(x_ref[0], g_ref[...]).astype(BF16)
    ext_sc[0:CONV_HALO, :] = halo_sc[...]

    def conv_cols(c0):
        e = ext_sc[:, c0:c0 + D]
        cw = cw_ref[:, c0:c0 + D]
        e1 = pltpu.roll(e, 1, 0)
        near = cw[3:4] * e + cw[2:3] * e1
        far = cw[1:2] * e + cw[0:1] * e1
        acc = (near + pltpu.roll(far, 2, 0))[CONV_HALO:]
        y = acc * _sigmoid(acc)
        if c0 < 2 * DN_WIDTH:
            y = y * lax.rsqrt(jnp.sum(y * y, axis=-1, keepdims=True) + EPS)
            if c0 < DN_WIDTH:
                y = y * (D ** -0.5)
        qkv_ref[0, :, c0:c0 + D] = y

    pieces = []
    off = nqkv
    for o_ref in o_refs:
        n = o_ref.shape[-1]
        pieces += [(o_ref, c, off + c, LANES) for c in range(0, n, LANES)]
        off += n
    per_dot = MXU_COLS // LANES
    others = [pieces[i:i + per_dot] for i in range(0, len(pieces), per_dot)]
    def qkv_cols(c0):
        ext_sc[CONV_HALO:, c0:c0 + MXU_COLS] = _dot(h, w_ref[:, c0:c0 + MXU_COLS])

    nsteps = nqkv // MXU_COLS
    qkv_cols(0)
    for i in range(nsteps):
        if i + 1 < nsteps:
            qkv_cols((i + 1) * MXU_COLS)
        for c in range(i * MXU_COLS, (i + 1) * MXU_COLS, D):
            conv_cols(c)
        for group in others[i::nsteps]:
            w0 = group[0][2]
            res = _dot(h, w_ref[:, w0:w0 + LANES * len(group)])
            for o_ref, c, woff, n in group:
                o_ref[0, :, c:c + n] = res[:, woff - w0:woff - w0 + n]
    halo_sc[...] = ext_sc[tm:tm + CONV_HALO, :]


def _inproj(x, gain, w_cat, conv_w):
    b, s, _ = x.shape
    tm = TM_PROJ
    n_all = sum(IN_SEGS)
    return pl.pallas_call(
        _inproj_kernel,
        grid=(b, s // tm),
        in_specs=[pl.BlockSpec((1, tm, D_MODEL), lambda i, j: (i, j, 0)),
                  _const_spec((1, D_MODEL)),
                  _const_spec((D_MODEL, n_all)),
                  _const_spec((DN_CONV, 3 * DN_WIDTH))],
        out_specs=[pl.BlockSpec((1, tm, n), lambda i, j: (i, j, 0)) for n in IN_SEGS],
        out_shape=[jax.ShapeDtypeStruct((b, s, n), F32) for n in IN_SEGS],
        scratch_shapes=[pltpu.VMEM((CONV_HALO, 3 * DN_WIDTH), F32),
                        pltpu.VMEM((tm + CONV_HALO, 3 * DN_WIDTH), F32)],
        compiler_params=_params("parallel", "arbitrary"),
        name="inproj",
    )(x, gain, w_cat, conv_w)


def _dn_kernel(qkv_ref, z_ref, misc_ref, alog_ref, dtb_ref, ng_ref, y_ref, state_sc):
    nb = qkv_ref.shape[0]
    L = DN_TILE
    C = DN_CHUNK
    nchunk = L // C
    D = DN_HEAD_DIM
    s_idx = pl.program_id(0)
    chains = [(b, h) for b in range(nb) for h in range(DN_HEADS)]

    @pl.when(s_idx == 0)
    def _():
        state_sc[...] = jnp.zeros_like(state_sc)

    def chunk_masks(n):
        ri = lax.broadcasted_iota(jnp.int32, (n, n), 0)
        ci = lax.broadcasted_iota(jnp.int32, (n, n), 1)
        same = (ri >> 6) == (ci >> 6)
        return same, same & (ci <= ri), same & (ci < ri), ri == ci

    same_l, causal_l, _, _ = chunk_masks(L)
    cum_mat = jnp.concatenate([causal_l.astype(BF16), same_l.astype(BF16)], axis=0)
    P = DN_PAIR
    nblk = L // P
    _, causal_bd, strict_bd, diag = chunk_masks(P)
    eye = diag.astype(F32)

    beta_all, gcum_all, glast_all, gcum_t = [], [], [], []
    for b in range(nb):
        misc = misc_ref[b]
        beta_all.append(_sigmoid(misc))
        sp_in = misc + dtb_ref[...]
        softplus = jnp.maximum(sp_in, 0.0) + jnp.log1p(jnp.exp(-jnp.abs(sp_in)))
        g_all = -jnp.exp(alog_ref[...]) * softplus
        g1, g2, g3 = _split3(g_all)
        cum = _dot(cum_mat, g1) + _dot(cum_mat, g2) + _dot(cum_mat, g3)
        gcum_all.append(cum[:L])
        glast_all.append(cum[L:])
        gcum_t.append(cum[:L].T)

    def stage1(b, h):
        a_mats, attns = [], []
        lane = DN_HEADS + h
        gc_col = gcum_all[b][:, lane:lane + 1]
        gl_col = glast_all[b][:, lane:lane + 1]
        gc_row = gcum_t[b][lane:lane + 1, :]
        beta = beta_all[b][:, h:h + 1]
        qn = qkv_ref[b, :, h * D:(h + 1) * D]
        kn = qkv_ref[b, :, DN_WIDTH + h * D:DN_WIDTH + (h + 1) * D]
        v = qkv_ref[b, :, 2 * DN_WIDTH + h * D:2 * DN_WIDTH + (h + 1) * D]
        kb = kn * beta
        e_col = jnp.exp(gc_col)
        kn16 = kn.astype(BF16)
        kb16 = kb.astype(BF16)
        qn16 = qn.astype(BF16)
        for j in range(nblk):
            rows = slice(j * P, (j + 1) * P)
            decay = jnp.exp(jnp.where(causal_bd, gc_col[rows] - gc_row[:, rows], 0.0))
            a_mats.append(jnp.where(strict_bd, _dot_nt(kb16[rows], kn16[rows]) * decay, 0.0))
            attns.append(jnp.where(causal_bd, _dot_nt(qn16[rows], kn16[rows]) * decay, 0.0).astype(BF16))
        return dict(a=a_mats, attn=attns,
                    rhs=jnp.concatenate([v * beta, kb * e_col], axis=1).astype(BF16),
                    qd=(qn * e_col).astype(BF16),
                    kdt=(kn * jnp.exp(gl_col - gc_col)).T.astype(BF16),
                    gdec=jnp.exp(jnp.broadcast_to(gl_col, (L, D))))

    def stage2(group):
        a_mats = [a for ch in group for a in ch["a"]]
        xps = [(-a).astype(BF16) for a in a_mats]
        t_invs = [eye - a for a in a_mats]
        for _ in range(5):
            xps = [_dot(xp, xp).astype(BF16) for xp in xps]
            t_invs = [t + _dot(t.astype(BF16), xp) for t, xp in zip(t_invs, xps)]
        for i, ch in enumerate(group):
            ch["uw"] = [_dot(t_invs[i * nblk + j].astype(BF16), ch["rhs"][j * P:(j + 1) * P])
                        for j in range(nblk)]

    groups = []
    for b in range(nb):
        if groups:
            stage2(groups[-1])
        groups.append([stage1(b, h) for h in range(DN_HEADS)])
    stage2(groups[-1])
    chs = [ch for group in groups for ch in group]

    states = [state_sc[b, h] for b, h in chains]
    o_parts = [[] for _ in chains]
    zeros_c = jnp.zeros((C, D), BF16)
    for c in range(nchunk):
        r0 = c * C
        j, half = divmod(c, P // C)
        p0 = half * C
        rs = [_dot(jnp.concatenate([ch["uw"][j][p0:p0 + C, D:].astype(BF16), ch["qd"][r0:r0 + C]], axis=0),
                   states[i].astype(BF16)) for i, ch in enumerate(chs)]
        for i, ch in enumerate(chs):
            v_new = (ch["uw"][j][p0:p0 + C, :D] - rs[i][:C]).astype(BF16)
            v_blk = jnp.concatenate([zeros_c] * half + [v_new] + [zeros_c] * (P // C - 1 - half), axis=0)
            lhs = jnp.concatenate([ch["attn"][j][p0:p0 + C, :], ch["kdt"][:, j * P:(j + 1) * P]], axis=0)
            m2 = _dot(lhs, v_blk)
            o_parts[i].append(rs[i][C:] + m2[:C])
            states[i] = states[i] * ch["gdec"][r0:r0 + 1, :] + m2[C:]

    for i, (b, h) in enumerate(chains):
        state_sc[b, h] = states[i]
        o = jnp.concatenate(o_parts[i], axis=0)
        zh = z_ref[b, :, h * D:(h + 1) * D]
        y_ref[b, :, h * D:(h + 1) * D] = _rms_rows(o, ng_ref[...]) * (zh * _sigmoid(zh))


def _deltanet(qkv, z, misc, alog_l, dtb_l, norm_gain):
    b, s, _ = qkv.shape
    L = DN_TILE
    return pl.pallas_call(
        _dn_kernel,
        grid=(s // L,),
        in_specs=[pl.BlockSpec((b, L, 3 * DN_WIDTH), lambda j: (0, j, 0)),
                  pl.BlockSpec((b, L, DN_WIDTH), lambda j: (0, j, 0)),
                  pl.BlockSpec((b, L, LANES), lambda j: (0, j, 0)),
                  _const_spec((1, LANES)),
                  _const_spec((1, LANES)),
                  _const_spec((1, DN_HEAD_DIM))],
        out_specs=pl.BlockSpec((b, L, DN_WIDTH), lambda j: (0, j, 0)),
        out_shape=jax.ShapeDtypeStruct((b, s, DN_WIDTH), F32),
        scratch_shapes=[pltpu.VMEM((b, DN_HEADS, DN_HEAD_DIM, DN_HEAD_DIM), F32)],
        compiler_params=_params("arbitrary"),
        name="deltanet",
    )(qkv, z, misc, alog_l, dtb_l, norm_gain)


def _rope_kernel(pos_ref, freq_ref, cos_ref, sina_ref, sinb_ref):
    tm = pos_ref.shape[1]
    ang = pos_ref[0].astype(F32) * freq_ref[...]
    lane = lax.broadcasted_iota(jnp.int32, (tm, LANES), 1)
    pe1 = (lane >= PE1_LANE) & (lane < PE1_LANE + HALF_ROPE)
    pe2 = (lane >= PE2_LANE) & (lane < PE2_LANE + HALF_ROPE)
    cos = jnp.cos(ang)
    sin = jnp.sin(ang)
    cos_ref[0] = jnp.where(lane < MLA_NOPE, 1.0, jnp.where(pe1 | pe2, cos, 0.0))
    sina_ref[0] = jnp.where(pe2, sin, 0.0)
    sinb_ref[0] = jnp.where(pe1, -sin, 0.0)


def _rope_tables(positions, freq_lanes):
    b, s = positions.shape
    tm = TM_PREP
    spec = pl.BlockSpec((1, tm, LANES), lambda i, j: (i, j, 0))
    return pl.pallas_call(
        _rope_kernel,
        grid=(b, s // tm),
        in_specs=[pl.BlockSpec((1, tm, 1), lambda i, j: (i, j, 0)), _const_spec((1, LANES))],
        out_specs=[spec, spec, spec],
        out_shape=[jax.ShapeDtypeStruct((b, s, LANES), F32)] * 3,
        compiler_params=_params("parallel", "parallel"),
        name="rope_tables",
    )(positions.reshape(b, s, 1), freq_lanes)


def _rope(x, cos, sina, sinb):
    return x * cos + pltpu.roll(x, 2 * HALF_ROPE, 1) * sina + pltpu.roll(x, LANES - 2 * HALF_ROPE, 1) * sinb


def _mla_prep_kernel(cq_ref, ckv_ref, misc_ref, cos_ref, sina_ref, sinb_ref,
                     qag_ref, kvag_ref, wq_ref, wk_ref, wvt_ref, qg_ref, kg_ref, kpg_ref,
                     q_ref, k_ref, vt_ref, qsq_ref, ksq_ref):
    tm = cq_ref.shape[1]
    ones8 = jnp.ones((SUBLANES, LANES), BF16)

    def row_sq_norms(x16):
        xf = x16.astype(F32)
        return _dot_nt(ones8, (xf * xf).astype(BF16))

    cos = cos_ref[0]
    sina = sina_ref[0]
    sinb = sinb_ref[0]
    lane = lax.broadcasted_iota(jnp.int32, (tm, LANES), 1)
    is_nope = lane < MLA_NOPE

    cqn = _rms_rows(cq_ref[0], qag_ref[...]).astype(BF16)
    ckvn = _rms_rows(ckv_ref[0], kvag_ref[...]).astype(BF16)
    qf = _dot(cqn, wq_ref[...])
    kf = _dot(ckvn, wk_ref[...])
    vt_ref[0] = _dot_nt(wvt_ref[...], ckvn).astype(BF16)

    kp = jnp.where(is_nope, 0.0, misc_ref[0])
    kp_ms = jnp.sum(kp * kp, axis=-1, keepdims=True) * (1.0 / MLA_ROPE)
    kp = _rope(kp * lax.rsqrt(kp_ms + EPS) * kpg_ref[...], cos, sina, sinb)

    scale = MLA_QK_DIM ** -0.5 * math.log2(math.e)
    for h in range(MLA_HEADS):
        xq = qf[:, h * LANES:(h + 1) * LANES]
        sq = xq * xq
        ms_n = jnp.sum(jnp.where(is_nope, sq, 0.0), axis=-1, keepdims=True) * (1.0 / MLA_NOPE)
        ms_p = jnp.sum(jnp.where(is_nope, 0.0, sq), axis=-1, keepdims=True) * (1.0 / MLA_ROPE)
        inv = jnp.where(is_nope, lax.rsqrt(ms_n + EPS), lax.rsqrt(ms_p + EPS))
        qh = _rope(xq * inv * qg_ref[...], cos, sina, sinb) * scale
        q16 = qh.astype(BF16)
        q_ref[0, h] = q16
        qsq_ref[0, h] = row_sq_norms(q16)

        xk = kf[:, h * LANES:(h + 1) * LANES]
        ms_k = jnp.sum(xk * xk, axis=-1, keepdims=True) * (1.0 / MLA_NOPE)
        kh = xk * lax.rsqrt(ms_k + EPS) * kg_ref[...] + kp
        k16 = kh.astype(BF16)
        k_ref[0, h] = k16
        ksq_ref[0, h] = row_sq_norms(k16)


def _mla_prep(cq, ckv, misc, cos, sina, sinb, qag, kvag, wq, wk, wvt, qg, kg, kpg):
    b, s, _ = cq.shape
    tm = TM_PREP
    hl = MLA_HEADS * LANES

    def tok(n):
        return pl.BlockSpec((1, tm, n), lambda i, j: (i, j, 0))

    return pl.pallas_call(
        _mla_prep_kernel,
        grid=(b, s // tm),
        in_specs=[tok(Q_LORA), tok(KV_LORA), tok(LANES), tok(LANES), tok(LANES), tok(LANES),
                  _const_spec((1, Q_LORA)), _const_spec((1, KV_LORA)),
                  _const_spec((Q_LORA, hl)), _const_spec((KV_LORA, hl)), _const_spec((MLA_WIDTH, KV_LORA)),
                  _const_spec((1, LANES)), _const_spec((1, LANES)), _const_spec((1, LANES))],
        out_specs=[pl.BlockSpec((1, MLA_HEADS, tm, LANES), lambda i, j: (i, 0, j, 0)),
                   pl.BlockSpec((1, MLA_HEADS, tm, LANES), lambda i, j: (i, 0, j, 0)),
                   pl.BlockSpec((1, MLA_WIDTH, tm), lambda i, j: (i, 0, j)),
                   pl.BlockSpec((1, MLA_HEADS, SUBLANES, tm), lambda i, j: (i, 0, 0, j)),
                   pl.BlockSpec((1, MLA_HEADS, SUBLANES, tm), lambda i, j: (i, 0, 0, j))],
        out_shape=[jax.ShapeDtypeStruct((b, MLA_HEADS, s, LANES), BF16),
                   jax.ShapeDtypeStruct((b, MLA_HEADS, s, LANES), BF16),
                   jax.ShapeDtypeStruct((b, MLA_WIDTH, s), BF16),
                   jax.ShapeDtypeStruct((b, MLA_HEADS, SUBLANES, s), F32),
                   jax.ShapeDtypeStruct((b, MLA_HEADS, SUBLANES, s), F32)],
        compiler_params=_params("parallel", "parallel"),
        name="mla_prep",
    )(cq, ckv, misc, cos, sina, sinb, qag, kvag, wq, wk, wvt, qg, kg, kpg)


def _bounds_kernel(qsq_ref, ksq_ref, u_ref, flag_ref):
    worst = None
    for h in range(MLA_HEADS):
        kmax = jnp.max(ksq_ref[0, h], axis=-1, keepdims=True)
        u = jnp.sqrt(qsq_ref[0, h] * kmax) * SCORE_BOUND_SLACK
        u_ref[0, h] = u
        umax = jnp.max(u, axis=-1, keepdims=True)
        worst = umax if worst is None else jnp.maximum(worst, umax)
    flag_ref[0] = jnp.broadcast_to((worst <= SCORE_BOUND_LIMIT).astype(jnp.int32), flag_ref.shape[1:])


def _score_bounds(qsq, ksq):
    b, h, r, s = qsq.shape
    spec = pl.BlockSpec((1, h, r, s), lambda i: (i, 0, 0, 0))
    return pl.pallas_call(
        _bounds_kernel,
        grid=(b,),
        in_specs=[spec, spec],
        out_specs=[spec, pl.BlockSpec((1, SUBLANES, LANES), lambda i: (i, 0, 0))],
        out_shape=[jax.ShapeDtypeStruct((b, h, r, s), F32), jax.ShapeDtypeStruct((b, SUBLANES, LANES), jnp.int32)],
        compiler_params=_params("parallel"),
        name="score_bounds",
    )(qsq, ksq)


def _attn_block(q_ref, k_ref, vt_ref, u_ref, m_sc, l_sc, acc_sc, visible, bounded):
    scores = [_dot_nt(k_ref[0, 0], q_ref[0, 0])]
    for h in range(MLA_HEADS):
        if h + 1 < MLA_HEADS:
            scores.append(_dot_nt(k_ref[0, h + 1], q_ref[0, h + 1]))
        st = scores[h]
        if visible is not None:
            st = jnp.where(visible, st, -jnp.inf)
        vt = vt_ref[0, h * MLA_V:(h + 1) * MLA_V, :]
        if bounded:
            pt = jnp.exp2(st - u_ref[0, h, 0:1, :])
            l_sc[h] = l_sc[h] + jnp.sum(pt, axis=0, keepdims=True)
            acc_sc[h] = acc_sc[h] + _dot(vt, pt.astype(BF16))
        else:
            m_prev = m_sc[h]
            m_new = jnp.maximum(m_prev, jnp.max(st, axis=0, keepdims=True))
            alpha = jnp.exp2(m_prev - m_new)
            pt = jnp.exp2(st - m_new)
            l_sc[h] = alpha * l_sc[h] + jnp.sum(pt, axis=0, keepdims=True)
            acc_sc[h] = alpha * acc_sc[h] + _dot(vt, pt.astype(BF16))
            m_sc[h] = m_new


def _attn_kernel(qi_ref, kj_ref, last_ref, flag_ref, q_ref, k_ref, vt_ref, u_ref, o_ref, m_sc, l_sc, acc_sc):
    p = pl.program_id(1)
    qi = qi_ref[p]
    kj = kj_ref[p]
    bounded = flag_ref[pl.program_id(0)] == 1

    @pl.when(kj == 0)
    def _():
        m_sc[...] = jnp.full_like(m_sc, -jnp.inf)
        l_sc[...] = jnp.zeros_like(l_sc)
        acc_sc[...] = jnp.zeros_like(acc_sc)

    all_visible = kj * TK + (TK - 1) <= qi * TQ

    def run(visible_fn, use_bound):
        def body():
            _attn_block(q_ref, k_ref, vt_ref, u_ref, m_sc, l_sc, acc_sc, visible_fn(), use_bound)
        return body

    def causal_mask():
        kpos = kj * TK + lax.broadcasted_iota(jnp.int32, (TK, TQ), 0)
        qpos = qi * TQ + lax.broadcasted_iota(jnp.int32, (TK, TQ), 1)
        return kpos <= qpos

    partly = jnp.logical_not(all_visible)
    unbounded = jnp.logical_not(bounded)
    pl.when(all_visible & bounded)(run(lambda: None, True))
    pl.when(partly & bounded)(run(causal_mask, True))
    pl.when(all_visible & unbounded)(run(lambda: None, False))
    pl.when(partly & unbounded)(run(causal_mask, False))

    @pl.when(last_ref[p] == 1)
    def _():
        out_t = jnp.concatenate([acc_sc[h] / l_sc[h] for h in range(MLA_HEADS)], axis=0)
        o_ref[0] = out_t.T


def _attention(q, k, vt, u, flag):
    b, _, s, _ = q.shape
    nq = s // TQ
    pairs = [(i, j) for i in range(nq) for j in range((i * TQ + TQ - 1) // TK + 1)]
    qi = jnp.asarray(np.array([p[0] for p in pairs], np.int32))
    kj = jnp.asarray(np.array([p[1] for p in pairs], np.int32))
    last = jnp.asarray(np.array([int(p[1] == (p[0] * TQ + TQ - 1) // TK) for p in pairs], np.int32))
    grid_spec = pltpu.PrefetchScalarGridSpec(
        num_scalar_prefetch=4,
        grid=(b, len(pairs)),
        in_specs=[pl.BlockSpec((1, MLA_HEADS, TQ, LANES), lambda i, p, qi, kj, *_: (i, 0, qi[p], 0)),
                  pl.BlockSpec((1, MLA_HEADS, TK, LANES), lambda i, p, qi, kj, *_: (i, 0, kj[p], 0)),
                  pl.BlockSpec((1, MLA_WIDTH, TK), lambda i, p, qi, kj, *_: (i, 0, kj[p])),
                  pl.BlockSpec((1, MLA_HEADS, SUBLANES, TQ), lambda i, p, qi, kj, *_: (i, 0, 0, qi[p]))],
        out_specs=pl.BlockSpec((1, TQ, MLA_WIDTH), lambda i, p, qi, kj, *_: (i, qi[p], 0)),
        scratch_shapes=[pltpu.VMEM((MLA_HEADS, 1, TQ), F32),
                        pltpu.VMEM((MLA_HEADS, 1, TQ), F32),
                        pltpu.VMEM((MLA_HEADS, MLA_V, TQ), F32)],
    )
    return pl.pallas_call(
        _attn_kernel,
        grid_spec=grid_spec,
        out_shape=jax.ShapeDtypeStruct((b, s, MLA_WIDTH), F32),
        compiler_params=_params("parallel", "arbitrary"),
        name="mla_attention",
    )(qi, kj, last, flag, q, k, vt, u)


POOL_HALO = 16


def _post_kernel(x_ref, xa_ref, yb_ref, yc_ref, wp_ref, ps_ref, wo_ref, g_ref, wu_ref, wd_ref, o_ref,
                 halo_sc, ext_sc):
    ts = x_ref.shape[1]
    s_idx = pl.program_id(1)

    @pl.when(s_idx == 0)
    def _():
        halo_sc[...] = jnp.zeros_like(halo_sc)

    xa = xa_ref[0]
    ext_sc[0:POOL_HALO, :] = halo_sc[...]
    ext_sc[POOL_HALO:, :] = xa
    halo_sc[...] = xa_ref[0, ts - POOL_HALO:ts, :]

    e = ext_sc[...]
    sums = []
    step = 1
    for _ in POOL_WINDOWS:
        e = e + pltpu.roll(e, step, 0)
        step *= 2
        sums.append(e[POOL_HALO:])
    lane = lax.broadcasted_iota(jnp.int32, (ts, POOL_WIDTH), 1)
    grp = lane >> 6
    win = jnp.left_shift(2, grp)
    t = s_idx * ts + lax.broadcasted_iota(jnp.int32, (ts, POOL_WIDTH), 0)
    count = jnp.minimum(t + 1, win).astype(F32)
    pooled = jnp.where(grp == 0, sums[0], jnp.where(grp == 1, sums[1], jnp.where(grp == 2, sums[2], sums[3])))
    ya = _dot((pooled / count - xa).astype(BF16), wp_ref[...]) * ps_ref[...]

    acc = x_ref[0] + _dot(ya.astype(BF16), wo_ref[0:POOL_WIDTH, :])
    acc = acc + _dot(yb_ref[0].astype(BF16), wo_ref[POOL_WIDTH:POOL_WIDTH + DN_WIDTH, :])
    x1 = acc + _dot(yc_ref[0].astype(BF16), wo_ref[POOL_WIDTH + DN_WIDTH:, :])

    h = _rms_rows(x1, g_ref[...]).astype(BF16)
    u = jnp.maximum(_dot(h, wu_ref[...]), 0.0)
    o_ref[0] = x1 + _dot((u * u).astype(BF16), wd_ref[...])


def _post(x, xa, yb, yc, wp_bd, pool_scale, w_out, gain, w_up, w_down):
    b, s, _ = x.shape
    ts = TM_PROJ

    def tok(n):
        return pl.BlockSpec((1, ts, n), lambda i, j: (i, j, 0))

    return pl.pallas_call(
        _post_kernel,
        grid=(b, s // ts),
        in_specs=[tok(D_MODEL), tok(POOL_WIDTH), tok(DN_WIDTH), tok(MLA_WIDTH),
                  _const_spec((POOL_WIDTH, POOL_WIDTH)), _const_spec((1, POOL_WIDTH)),
                  _const_spec((D_MODEL, D_MODEL)), _const_spec((1, D_MODEL)),
                  _const_spec((D_MODEL, D_FF)), _const_spec((D_FF, D_MODEL))],
        out_specs=tok(D_MODEL),
        out_shape=jax.ShapeDtypeStruct((b, s, D_MODEL), F32),
        scratch_shapes=[pltpu.VMEM((POOL_HALO, POOL_WIDTH), F32),
                        pltpu.VMEM((ts + POOL_HALO, POOL_WIDTH), F32)],
        compiler_params=_params("parallel", "arbitrary"),
        name="outproj_pool_mlp",
    )(x, xa, yb, yc, wp_bd, pool_scale, w_out, gain, w_up, w_down)


def _head_block(nope, pe):
    z16 = jnp.zeros(pe.shape[:-1] + (HALF_ROPE,), pe.dtype)
    return jnp.concatenate([nope, pe[..., :HALF_ROPE], z16, pe[..., HALF_ROPE:], z16], axis=-1)


def _prep_layer(w_in, pool_w, pool_scale, dn_conv, dn_a_log, dn_dt_bias, dn_norm,
                q_a_norm, w_q_b, kv_a_norm, w_kv_b, q_norm, k_norm, w_out, w_up, w_down):
    offs = np.cumsum((0,) + IN_SPLITS)
    seg = [w_in[:, offs[i]:offs[i + 1]] for i in range(len(IN_SPLITS))]
    w_xa, w_qkv, w_z, w_b, w_a, w_cq, w_ckv, w_kpe = seg
    d = w_in.shape[0]
    z64 = jnp.zeros((d, MLA_NOPE), w_in.dtype)
    w_misc = jnp.concatenate([w_b, w_a, jnp.zeros((d, MLA_NOPE - 2 * DN_HEADS), w_in.dtype),
                              _head_block(z64, w_kpe)[:, MLA_NOPE:]], axis=1)
    w_cat = jnp.concatenate([w_qkv, w_xa, w_z, w_cq, w_ckv, w_misc], axis=1).astype(BF16)

    lane_pad = jnp.zeros((LANES - 2 * DN_HEADS,), F32)
    alog_l = jnp.concatenate([jnp.zeros((DN_HEADS,), F32), dn_a_log, lane_pad])[None]
    dtb_l = jnp.concatenate([jnp.zeros((DN_HEADS,), F32), dn_dt_bias, lane_pad])[None]

    wq = w_q_b.reshape(Q_LORA, MLA_HEADS, MLA_QK_DIM)
    wq = _head_block(wq[..., :MLA_NOPE], wq[..., MLA_NOPE:]).reshape(Q_LORA, MLA_HEADS * LANES).astype(BF16)
    wkv = w_kv_b.reshape(KV_LORA, MLA_HEADS, MLA_NOPE + MLA_V)
    wk = _head_block(wkv[..., :MLA_NOPE], jnp.zeros((KV_LORA, MLA_HEADS, MLA_ROPE), F32))
    wk = wk.reshape(KV_LORA, MLA_HEADS * LANES).astype(BF16)
    wvt = wkv[..., MLA_NOPE:].reshape(KV_LORA, MLA_WIDTH).T.astype(BF16)
    qg = _head_block(q_norm[:MLA_NOPE], q_norm[MLA_NOPE:])[None]
    kg = _head_block(k_norm[:MLA_NOPE], jnp.zeros((MLA_ROPE,), F32))[None]
    kpg = _head_block(jnp.zeros((MLA_NOPE,), F32), k_norm[MLA_NOPE:])[None]

    wp_bd = jax.scipy.linalg.block_diag(*[pool_w[g] for g in range(POOL_GROUPS)]).astype(BF16)
    return dict(w_cat=w_cat, alog_l=alog_l, dtb_l=dtb_l, conv_w=dn_conv, dn_norm=dn_norm[None],
                qag=q_a_norm[None], kvag=kv_a_norm[None], wq=wq, wk=wk, wvt=wvt, qg=qg, kg=kg, kpg=kpg,
                wp_bd=wp_bd, pool_scale=pool_scale[None], w_out=w_out.astype(BF16),
                w_up=w_up.astype(BF16), w_down=w_down.astype(BF16))


def kernel(x, positions, attn_norm, w_in, pool_w, pool_scale, dn_conv, dn_a_log, dn_dt_bias, dn_norm,
           mla_q_a_norm, mla_w_q_b, mla_kv_a_norm, mla_w_kv_b, mla_q_norm, mla_k_norm,
           w_out, mlp_norm, w_up, w_down):
    b, s, d = x.shape
    depth = w_in.shape[0]
    inv_freq = ROPE_THETA ** (-jnp.arange(0, MLA_ROPE, 2, dtype=F32) / MLA_ROPE)
    freq_lanes = _head_block(jnp.zeros((MLA_NOPE,), F32), jnp.concatenate([inv_freq, inv_freq]))[None]
    cos, sina, sinb = _rope_tables(positions, freq_lanes)

    for l in range(depth):
        p = _prep_layer(w_in[l], pool_w[l], pool_scale[l], dn_conv[l], dn_a_log[l], dn_dt_bias[l], dn_norm[l],
                        mla_q_a_norm[l], mla_w_q_b[l], mla_kv_a_norm[l], mla_w_kv_b[l], mla_q_norm[l],
                        mla_k_norm[l], w_out[l], w_up[l], w_down[l])
        qkv, xa, z, cq, ckv, misc = _inproj(x, attn_norm[l][None], p["w_cat"], p["conv_w"])
        y_b = _deltanet(qkv, z, misc, p["alog_l"], p["dtb_l"], p["dn_norm"])
        q, k, vt, qsq, ksq = _mla_prep(cq, ckv, misc, cos, sina, sinb, p["qag"], p["kvag"], p["wq"], p["wk"],
                                       p["wvt"], p["qg"], p["kg"], p["kpg"])
        u, flag = _score_bounds(qsq, ksq)
        y_c = _attention(q, k, vt, u, flag[:, 0, 0])
        x = _post(x, xa, y_b, y_c, p["wp_bd"], p["pool_scale"], p["w_out"], mlp_norm[l][None], p["w_up"], p["w_down"])
    return x
```

```python
import functools
import math

import jax
import jax.numpy as jnp
import numpy as np
from jax import lax
from jax.experimental import pallas as pl
from jax.experimental.pallas import tpu as pltpu

F32 = jnp.float32
BF16 = jnp.bfloat16

D_MODEL = 1024
POOL_GROUPS = 4
POOL_GROUP_DIM = 64
POOL_WIDTH = POOL_GROUPS * POOL_GROUP_DIM
POOL_WINDOWS = (2, 4, 8, 16)
DN_HEADS = 4
DN_HEAD_DIM = 128
DN_WIDTH = DN_HEADS * DN_HEAD_DIM
DN_CONV = 4
DN_CHUNK = 64
MLA_HEADS = 4
MLA_NOPE = 64
MLA_ROPE = 32
MLA_QK_DIM = MLA_NOPE + MLA_ROPE
MLA_V = 64
MLA_WIDTH = MLA_HEADS * MLA_V
Q_LORA = 256
KV_LORA = 128
ROPE_THETA = 10000.0
D_FF = 4 * D_MODEL
EPS = 1e-6
IN_SPLITS = (POOL_WIDTH, 3 * DN_WIDTH, DN_WIDTH, DN_HEADS, DN_HEADS, Q_LORA, KV_LORA, MLA_ROPE)

LANES = 128
SUBLANES = 8
MXU_COLS = 256
VMEM_LIMIT_BYTES = 56 * 1024 * 1024

TM_PROJ = 512
TM_PREP = 512
DN_TILE = 256
DN_PAIR = 2 * DN_CHUNK
TQ = 1024
TK = 1024
HALF_ROPE = MLA_ROPE // 2
SCORE_BOUND_SLACK = 1.02
SCORE_BOUND_LIMIT = 30.0
PE1_LANE = MLA_NOPE
PE2_LANE = MLA_NOPE + 2 * HALF_ROPE


def _dot(a, b):
    return jnp.dot(a, b, preferred_element_type=F32)


def _dot_nt(a, b):
    return lax.dot_general(a, b, (((1,), (1,)), ((), ())), preferred_element_type=F32)


def _rms_rows(x, gain):
    return x * lax.rsqrt(jnp.mean(x * x, axis=-1, keepdims=True) + EPS) * gain


def _split3(x):
    x1 = x.astype(BF16)
    r1 = x - x1.astype(F32)
    x2 = r1.astype(BF16)
    r2 = r1 - x2.astype(F32)
    return x1, x2, r2.astype(BF16)


def _sigmoid(x):
    return 1.0 / (1.0 + jnp.exp(-x))


def _params(*sem):
    return pltpu.CompilerParams(dimension_semantics=sem, vmem_limit_bytes=VMEM_LIMIT_BYTES)


def _const_spec(shape):
    nd = len(shape)
    return pl.BlockSpec(shape, lambda *_: (0,) * nd, pipeline_mode=pl.Buffered(1))


IN_SEGS = (3 * DN_WIDTH, POOL_WIDTH, DN_WIDTH, Q_LORA, KV_LORA, LANES)
CONV_HALO = SUBLANES


def _inproj_kernel(x_ref, g_ref, w_ref, cw_ref, qkv_ref, *rest):
    o_refs, (halo_sc, ext_sc) = rest[:-2], rest[-2:]
    tm = x_ref.shape[1]
    nqkv = 3 * DN_WIDTH
    D = DN_HEAD_DIM

    @pl.when(pl.program_id(1) == 0)
    def _():
        halo_sc[...] = jnp.zeros_like(halo_sc)

    h = _rms_rows(x_ref[0], g_ref[...]).astype(BF16)
    ext_sc[0:CONV_HALO, :] = halo_sc[...]

    def conv_cols(c0):
        e = ext_sc[:, c0:c0 + D]
        cw = cw_ref[:, c0:c0 + D]
        e1 = pltpu.roll(e, 1, 0)
        near = cw[3:4] * e + cw[2:3] * e1
        far = cw[1:2] * e + cw[0:1] * e1
        acc = (near + pltpu.roll(far, 2, 0))[CONV_HALO:]
        y = acc * _sigmoid(acc)
        if c0 < 2 * DN_WIDTH:
            y = y * lax.rsqrt(jnp.sum(y * y, axis=-1, keepdims=True) + EPS)
            if c0 < DN_WIDTH:
                y = y * (D ** -0.5)
        qkv_ref[0, :, c0:c0 + D] = y

    pieces = []
    off = nqkv
    for o_ref in o_refs:
        n = o_ref.shape[-1]
        pieces += [(o_ref, c, off + c, LANES) for c in range(0, n, LANES)]
        off += n
    per_dot = MXU_COLS // LANES
    others = [pieces[i:i + per_dot] for i in range(0, len(pieces), per_dot)]
    def qkv_cols(c0):
        ext_sc[CONV_HALO:, c0:c0 + MXU_COLS] = _dot(h, w_ref[:, c0:c0 + MXU_COLS])

    nsteps = nqkv // MXU_COLS
    qkv_cols(0)
    for i in range(nsteps):
        if i + 1 < nsteps:
            qkv_cols((i + 1) * MXU_COLS)
        for c in range(i * MXU_COLS, (i + 1) * MXU_COLS, D):
            conv_cols(c)
        for group in others[i::nsteps]:
            w0 = group[0][2]
            res = _dot(h, w_ref[:, w0:w0 + LANES * len(group)])
            for o_ref, c, woff, n in group:
                o_ref[0, :, c:c + n] = res[:, woff - w0:woff - w0 + n]
    halo_sc[...] = ext_sc[tm:tm + CONV_HALO, :]


def _inproj(x, gain, w_cat, conv_w):
    b, s, _ = x.shape
    tm = TM_PROJ
    n_all = sum(IN_SEGS)
    return pl.pallas_call(
        _inproj_kernel,
        grid=(b, s // tm),
        in_specs=[pl.BlockSpec((1, tm, D_MODEL), lambda i, j: (i, j, 0)),
                  _const_spec((1, D_MODEL)),
                  _const_spec((D_MODEL, n_all)),
                  _const_spec((DN_CONV, 3 * DN_WIDTH))],
        out_specs=[pl.BlockSpec((1, tm, n), lambda i, j: (i, j, 0)) for n in IN_SEGS],
        out_shape=[jax.ShapeDtypeStruct((b, s, n), F32) for n in IN_SEGS],
        scratch_shapes=[pltpu.VMEM((CONV_HALO, 3 * DN_WIDTH), F32),
                        pltpu.VMEM((tm + CONV_HALO, 3 * DN_WIDTH), F32)],
        compiler_params=_params("parallel", "arbitrary"),
        name="inproj",
    )(x, gain, w_cat, conv_w)


def _dn_kernel(qkv_ref, z_ref, misc_ref, alog_ref, dtb_ref, ng_ref, y_ref, state_sc):
    nb = qkv_ref.shape[0]
    L = DN_TILE
    C = DN_CHUNK
    nchunk = L // C
    D = DN_HEAD_DIM
    s_idx = pl.program_id(0)
    chains = [(b, h) for b in range(nb) for h in range(DN_HEADS)]

    @pl.when(s_idx == 0)
    def _():
        state_sc[...] = jnp.zeros_like(state_sc)

    def chunk_masks(n):
        ri = lax.broadcasted_iota(jnp.int32, (n, n), 0)
        ci = lax.broadcasted_iota(jnp.int32, (n, n), 1)
        same = (ri >> 6) == (ci >> 6)
        return same, same & (ci <= ri), same & (ci < ri), ri == ci

    same_l, causal_l, _, _ = chunk_masks(L)
    cum_mat = jnp.concatenate([causal_l.astype(BF16), same_l.astype(BF16)], axis=0)
    P = DN_PAIR
    nblk = L // P
    _, causal_bd, strict_bd, diag = chunk_masks(P)
    eye = diag.astype(F32)

    beta_all, gcum_all, glast_all, gcum_t = [], [], [], []
    for b in range(nb):
        misc = misc_ref[b]
        beta_all.append(_sigmoid(misc))
        sp_in = misc + dtb_ref[...]
        softplus = jnp.maximum(sp_in, 0.0) + jnp.log1p(jnp.exp(-jnp.abs(sp_in)))
        g_all = -jnp.exp(alog_ref[...]) * softplus
        g1, g2, g3 = _split3(g_all)
        cum = _dot(cum_mat, g1) + _dot(cum_mat, g2) + _dot(cum_mat, g3)
        gcum_all.append(cum[:L])
        glast_all.append(cum[L:])
        gcum_t.append(cum[:L].T)

    def stage1(b, h):
        a_mats, attns = [], []
        lane = DN_HEADS + h
        gc_col = gcum_all[b][:, lane:lane + 1]
        gl_col = glast_all[b][:, lane:lane + 1]
        gc_row = gcum_t[b][lane:lane + 1, :]
        beta = beta_all[b][:, h:h + 1]
        qn = qkv_ref[b, :, h * D:(h + 1) * D]
        kn = qkv_ref[b, :, DN_WIDTH + h * D:DN_WIDTH + (h + 1) * D]
        v = qkv_ref[b, :, 2 * DN_WIDTH + h * D:2 * DN_WIDTH + (h + 1) * D]
        kb = kn * beta
        e_col = jnp.exp(gc_col)
        kn16 = kn.astype(BF16)
        kb16 = kb.astype(BF16)
        qn16 = qn.astype(BF16)
        for j in range(nblk):
            rows = slice(j * P, (j + 1) * P)
            decay = jnp.exp(jnp.where(causal_bd, gc_col[rows] - gc_row[:, rows], 0.0))
            a_mats.append(jnp.where(strict_bd, _dot_nt(kb16[rows], kn16[rows]) * decay, 0.0))
            attns.append(jnp.where(causal_bd, _dot_nt(qn16[rows], kn16[rows]) * decay, 0.0).astype(BF16))
        return dict(a=a_mats, attn=attns,
                    rhs=jnp.concatenate([v * beta, kb * e_col], axis=1).astype(BF16),
                    qd=(qn * e_col).astype(BF16),
                    kdt=(kn * jnp.exp(gl_col - gc_col)).T.astype(BF16),
                    gdec=jnp.exp(jnp.broadcast_to(gl_col, (L, D))))

    def stage2(group):
        a_mats = [a for ch in group for a in ch["a"]]
        xps = [(-a).astype(BF16) for a in a_mats]
        t_invs = [eye - a for a in a_mats]
        xps = [_dot(xp, xp).astype(BF16) for xp in xps]
        for _ in range(4):
            prods = [_dot(jnp.concatenate([t.astype(BF16), xp], axis=0), xp) for t, xp in zip(t_invs, xps)]
            t_invs = [t + pr[:P] for t, pr in zip(t_invs, prods)]
            xps = [pr[P:].astype(BF16) for pr in prods]
        t_invs = [t + _dot(t.astype(BF16), xp) for t, xp in zip(t_invs, xps)]
        for i, ch in enumerate(group):
            ch["uw"] = [_dot(t_invs[i * nblk + j].astype(BF16), ch["rhs"][j * P:(j + 1) * P])
                        for j in range(nblk)]

    groups = []
    for b in range(nb):
        if groups:
            stage2(groups[-1])
        groups.append([stage1(b, h) for h in range(DN_HEADS)])
    stage2(groups[-1])
    chs = [ch for group in groups for ch in group]

    states = [state_sc[b, h] for b, h in chains]
    o_parts = [[] for _ in chains]
    zeros_c = jnp.zeros((C, D), BF16)
    for c in range(nchunk):
        r0 = c * C
        j, half = divmod(c, P // C)
        p0 = half * C
        rs = [_dot(jnp.concatenate([ch["uw"][j][p0:p0 + C, D:].astype(BF16), ch["qd"][r0:r0 + C]], axis=0),
                   states[i].astype(BF16)) for i, ch in enumerate(chs)]
        for i, ch in enumerate(chs):
            v_new = (ch["uw"][j][p0:p0 + C, :D] - rs[i][:C]).astype(BF16)
            v_blk = jnp.concatenate([zeros_c] * half + [v_new] + [zeros_c] * (P // C - 1 - half), axis=0)
            lhs = jnp.concatenate([ch["attn"][j][p0:p0 + C, :], ch["kdt"][:, j * P:(j + 1) * P]], axis=0)
            m2 = _dot(lhs, v_blk)
            o_parts[i].append(rs[i][C:] + m2[:C])
            states[i] = states[i] * ch["gdec"][r0:r0 + 1, :] + m2[C:]

    for i, (b, h) in enumerate(chains):
        state_sc[b, h] = states[i]
        o = jnp.concatenate(o_parts[i], axis=0)
        zh = z_ref[b, :, h * D:(h + 1) * D]
        y_ref[b, :, h * D:(h + 1) * D] = _rms_rows(o, ng_ref[...]) * (zh * _sigmoid(zh))


def _deltanet(qkv, z, misc, alog_l, dtb_l, norm_gain):
    b, s, _ = qkv.shape
    L = DN_TILE
    return pl.pallas_call(
        _dn_kernel,
        grid=(s // L,),
        in_specs=[pl.BlockSpec((b, L, 3 * DN_WIDTH), lambda j: (0, j, 0)),
                  pl.BlockSpec((b, L, DN_WIDTH), lambda j: (0, j, 0)),
                  pl.BlockSpec((b, L, LANES), lambda j: (0, j, 0)),
                  _const_spec((1, LANES)),
                  _const_spec((1, LANES)),
                  _const_spec((1, DN_HEAD_DIM))],
        out_specs=pl.BlockSpec((b, L, DN_WIDTH), lambda j: (0, j, 0)),
        out_shape=jax.ShapeDtypeStruct((b, s, DN_WIDTH), F32),
        scratch_shapes=[pltpu.VMEM((b, DN_HEADS, DN_HEAD_DIM, DN_HEAD_DIM), F32)],
        compiler_params=_params("arbitrary"),
        name="deltanet",
    )(qkv, z, misc, alog_l, dtb_l, norm_gain)


def _rope_kernel(pos_ref, freq_ref, cos_ref, sina_ref, sinb_ref):
    tm = pos_ref.shape[1]
    ang = pos_ref[0].astype(F32) * freq_ref[...]
    lane = lax.broadcasted_iota(jnp.int32, (tm, LANES), 1)
    pe1 = (lane >= PE1_LANE) & (lane < PE1_LANE + HALF_ROPE)
    pe2 = (lane >= PE2_LANE) & (lane < PE2_LANE + HALF_ROPE)
    cos = jnp.cos(ang)
    sin = jnp.sin(ang)
    cos_ref[0] = jnp.where(lane < MLA_NOPE, 1.0, jnp.where(pe1 | pe2, cos, 0.0))
    sina_ref[0] = jnp.where(pe2, sin, 0.0)
    sinb_ref[0] = jnp.where(pe1, -sin, 0.0)


def _rope_tables(positions, freq_lanes):
    b, s = positions.shape
    tm = TM_PREP
    spec = pl.BlockSpec((1, tm, LANES), lambda i, j: (i, j, 0))
    return pl.pallas_call(
        _rope_kernel,
        grid=(b, s // tm),
        in_specs=[pl.BlockSpec((1, tm, 1), lambda i, j: (i, j, 0)), _const_spec((1, LANES))],
        out_specs=[spec, spec, spec],
        out_shape=[jax.ShapeDtypeStruct((b, s, LANES), F32)] * 3,
        compiler_params=_params("parallel", "parallel"),
        name="rope_tables",
    )(positions.reshape(b, s, 1), freq_lanes)


def _rope(x, cos, sina, sinb):
    return x * cos + pltpu.roll(x, 2 * HALF_ROPE, 1) * sina + pltpu.roll(x, LANES - 2 * HALF_ROPE, 1) * sinb


def _mla_prep_kernel(cq_ref, ckv_ref, misc_ref, cos_ref, sina_ref, sinb_ref,
                     qag_ref, kvag_ref, wq_ref, wk_ref, wvt_ref, qg_ref, kg_ref, kpg_ref,
                     q_ref, k_ref, vt_ref, qsq_ref, ksq_ref):
    tm = cq_ref.shape[1]
    ones8 = jnp.ones((SUBLANES, LANES), BF16)

    def row_sq_norms(x16):
        xf = x16.astype(F32)
        return _dot_nt(ones8, (xf * xf).astype(BF16))

    cos = cos_ref[0]
    sina = sina_ref[0]
    sinb = sinb_ref[0]
    lane = lax.broadcasted_iota(jnp.int32, (tm, LANES), 1)
    is_nope = lane < MLA_NOPE

    cqn = _rms_rows(cq_ref[0], qag_ref[...]).astype(BF16)
    ckvn = _rms_rows(ckv_ref[0], kvag_ref[...]).astype(BF16)
    qf = _dot(cqn, wq_ref[...])
    kf = _dot(ckvn, wk_ref[...])
    vt_ref[0] = _dot_nt(wvt_ref[...], ckvn).astype(BF16)

    kp = jnp.where(is_nope, 0.0, misc_ref[0])
    kp_ms = jnp.sum(kp * kp, axis=-1, keepdims=True) * (1.0 / MLA_ROPE)
    kp = _rope(kp * lax.rsqrt(kp_ms + EPS) * kpg_ref[...], cos, sina, sinb)

    scale = MLA_QK_DIM ** -0.5 * math.log2(math.e)
    for h in range(MLA_HEADS):
        xq = qf[:, h * LANES:(h + 1) * LANES]
        sq = xq * xq
        ms_n = jnp.sum(jnp.where(is_nope, sq, 0.0), axis=-1, keepdims=True) * (1.0 / MLA_NOPE)
        ms_p = jnp.sum(jnp.where(is_nope, 0.0, sq), axis=-1, keepdims=True) * (1.0 / MLA_ROPE)
        inv = jnp.where(is_nope, lax.rsqrt(ms_n + EPS), lax.rsqrt(ms_p + EPS))
        qh = _rope(xq * inv * qg_ref[...], cos, sina, sinb) * scale
        q16 = qh.astype(BF16)
        q_ref[0, h] = q16
        qsq_ref[0, h] = row_sq_norms(q16)

        xk = kf[:, h * LANES:(h + 1) * LANES]
        ms_k = jnp.sum(xk * xk, axis=-1, keepdims=True) * (1.0 / MLA_NOPE)
        kh = xk * lax.rsqrt(ms_k + EPS) * kg_ref[...] + kp
        k16 = kh.astype(BF16)
        k_ref[0, h] = k16
        ksq_ref[0, h] = row_sq_norms(k16)


def _mla_prep(cq, ckv, misc, cos, sina, sinb, qag, kvag, wq, wk, wvt, qg, kg, kpg):
    b, s, _ = cq.shape
    tm = TM_PREP
    hl = MLA_HEADS * LANES

    def tok(n):
        return pl.BlockSpec((1, tm, n), lambda i, j: (i, j, 0))

    return pl.pallas_call(
        _mla_prep_kernel,
        grid=(b, s // tm),
        in_specs=[tok(Q_LORA), tok(KV_LORA), tok(LANES), tok(LANES), tok(LANES), tok(LANES),
                  _const_spec((1, Q_LORA)), _const_spec((1, KV_LORA)),
                  _const_spec((Q_LORA, hl)), _const_spec((KV_LORA, hl)), _const_spec((MLA_WIDTH, KV_LORA)),
                  _const_spec((1, LANES)), _const_spec((1, LANES)), _const_spec((1, LANES))],
        out_specs=[pl.BlockSpec((1, MLA_HEADS, tm, LANES), lambda i, j: (i, 0, j, 0)),
                   pl.BlockSpec((1, MLA_HEADS, tm, LANES), lambda i, j: (i, 0, j, 0)),
                   pl.BlockSpec((1, MLA_WIDTH, tm), lambda i, j: (i, 0, j)),
                   pl.BlockSpec((1, MLA_HEADS, SUBLANES, tm), lambda i, j: (i, 0, 0, j)),
                   pl.BlockSpec((1, MLA_HEADS, SUBLANES, tm), lambda i, j: (i, 0, 0, j))],
        out_shape=[jax.ShapeDtypeStruct((b, MLA_HEADS, s, LANES), BF16),
                   jax.ShapeDtypeStruct((b, MLA_HEADS, s, LANES), BF16),
                   jax.ShapeDtypeStruct((b, MLA_WIDTH, s), BF16),
                   jax.ShapeDtypeStruct((b, MLA_HEADS, SUBLANES, s), F32),
                   jax.ShapeDtypeStruct((b, MLA_HEADS, SUBLANES, s), F32)],
        compiler_params=_params("parallel", "parallel"),
        name="mla_prep",
    )(cq, ckv, misc, cos, sina, sinb, qag, kvag, wq, wk, wvt, qg, kg, kpg)


def _bounds_kernel(qsq_ref, ksq_ref, u_ref, flag_ref):
    worst = None
    for h in range(MLA_HEADS):
        kmax = jnp.max(ksq_ref[0, h], axis=-1, keepdims=True)
        u = jnp.sqrt(qsq_ref[0, h] * kmax) * SCORE_BOUND_SLACK
        u_ref[0, h] = u
        umax = jnp.max(u, axis=-1, keepdims=True)
        worst = umax if worst is None else jnp.maximum(worst, umax)
    flag_ref[0] = jnp.broadcast_to((worst <= SCORE_BOUND_LIMIT).astype(jnp.int32), flag_ref.shape[1:])


def _score_bounds(qsq, ksq):
    b, h, r, s = qsq.shape
    spec = pl.BlockSpec((1, h, r, s), lambda i: (i, 0, 0, 0))
    return pl.pallas_call(
        _bounds_kernel,
        grid=(b,),
        in_specs=[spec, spec],
        out_specs=[spec, pl.BlockSpec((1, SUBLANES, LANES), lambda i: (i, 0, 0))],
        out_shape=[jax.ShapeDtypeStruct((b, h, r, s), F32), jax.ShapeDtypeStruct((b, SUBLANES, LANES), jnp.int32)],
        compiler_params=_params("parallel"),
        name="score_bounds",
    )(qsq, ksq)


def _attn_block(q_ref, k_ref, vt_ref, u_ref, m_sc, l_sc, acc_sc, visible, bounded):
    scores = [_dot_nt(k_ref[0, 0], q_ref[0, 0])]
    for h in range(MLA_HEADS):
        if h + 1 < MLA_HEADS:
            scores.append(_dot_nt(k_ref[0, h + 1], q_ref[0, h + 1]))
        st = scores[h]
        if visible is not None:
            st = jnp.where(visible, st, -jnp.inf)
        vt = vt_ref[0, h * MLA_V:(h + 1) * MLA_V, :]
        if bounded:
            pt = jnp.exp2(st - u_ref[0, h, 0:1, :])
            l_sc[h] = l_sc[h] + jnp.sum(pt, axis=0, keepdims=True)
            acc_sc[h] = acc_sc[h] + _dot(vt, pt.astype(BF16))
        else:
            m_prev = m_sc[h]
            m_new = jnp.maximum(m_prev, jnp.max(st, axis=0, keepdims=True))
            alpha = jnp.exp2(m_prev - m_new)
            pt = jnp.exp2(st - m_new)
            l_sc[h] = alpha * l_sc[h] + jnp.sum(pt, axis=0, keepdims=True)
            acc_sc[h] = alpha * acc_sc[h] + _dot(vt, pt.astype(BF16))
            m_sc[h] = m_new


def _attn_diag_block(q_ref, k_ref, vt_ref, u_ref, l_sc, acc_sc):
    hk = TK // 2
    r = lax.broadcasted_iota(jnp.int32, (hk, TQ), 0)
    c = lax.broadcasted_iota(jnp.int32, (hk, TQ), 1)
    vis_old = r <= c
    vis_new = vis_old[:, :hk]

    def scores(h):
        return (_dot_nt(k_ref[0, h, 0:hk, :], q_ref[0, h]),
                _dot_nt(k_ref[0, h, hk:, :], q_ref[0, h, hk:, :]))

    nxt = scores(0)
    for h in range(MLA_HEADS):
        s_old, s_new = nxt
        if h + 1 < MLA_HEADS:
            nxt = scores(h + 1)
        u = u_ref[0, h, 0:1, :]
        p_old = jnp.exp2(jnp.where(vis_old, s_old, -jnp.inf) - u)
        p_new = jnp.exp2(jnp.where(vis_new, s_new, -jnp.inf) - u[:, hk:])
        vt = vt_ref[0, h * MLA_V:(h + 1) * MLA_V, :]
        l_sc[h] = l_sc[h] + jnp.sum(p_old, axis=0, keepdims=True)
        acc_sc[h] = acc_sc[h] + _dot(vt[:, :hk], p_old.astype(BF16))
        l_sc[h, :, hk:] = l_sc[h, :, hk:] + jnp.sum(p_new, axis=0, keepdims=True)
        acc_sc[h, :, hk:] = acc_sc[h, :, hk:] + _dot(vt[:, hk:], p_new.astype(BF16))


def _attn_kernel(qi_ref, kj_ref, last_ref, flag_ref, q_ref, k_ref, vt_ref, u_ref, o_ref, m_sc, l_sc, acc_sc):
    p = pl.program_id(1)
    qi = qi_ref[p]
    kj = kj_ref[p]
    bounded = flag_ref[pl.program_id(0)] == 1

    @pl.when(kj == 0)
    def _():
        m_sc[...] = jnp.full_like(m_sc, -jnp.inf)
        l_sc[...] = jnp.zeros_like(l_sc)
        acc_sc[...] = jnp.zeros_like(acc_sc)

    all_visible = kj * TK + (TK - 1) <= qi * TQ

    def run(visible_fn, use_bound):
        def body():
            _attn_block(q_ref, k_ref, vt_ref, u_ref, m_sc, l_sc, acc_sc, visible_fn(), use_bound)
        return body

    def causal_mask():
        kpos = kj * TK + lax.broadcasted_iota(jnp.int32, (TK, TQ), 0)
        qpos = qi * TQ + lax.broadcasted_iota(jnp.int32, (TK, TQ), 1)
        return kpos <= qpos

    partly = jnp.logical_not(all_visible)
    unbounded = jnp.logical_not(bounded)
    pl.when(all_visible & bounded)(run(lambda: None, True))
    if TQ == TK:
        pl.when(partly & bounded)(lambda: _attn_diag_block(q_ref, k_ref, vt_ref, u_ref, l_sc, acc_sc))
    else:
        pl.when(partly & bounded)(run(causal_mask, True))
    pl.when(all_visible & unbounded)(run(lambda: None, False))
    pl.when(partly & unbounded)(run(causal_mask, False))

    @pl.when(last_ref[p] == 1)
    def _():
        out_t = jnp.concatenate([acc_sc[h] / l_sc[h] for h in range(MLA_HEADS)], axis=0)
        o_ref[0] = out_t.T


def _attention(q, k, vt, u, flag):
    b, _, s, _ = q.shape
    nq = s // TQ
    pairs = [(i, j) for i in range(nq) for j in range((i * TQ + TQ - 1) // TK + 1)]
    qi = jnp.asarray(np.array([p[0] for p in pairs], np.int32))
    kj = jnp.asarray(np.array([p[1] for p in pairs], np.int32))
    last = jnp.asarray(np.array([int(p[1] == (p[0] * TQ + TQ - 1) // TK) for p in pairs], np.int32))
    grid_spec = pltpu.PrefetchScalarGridSpec(
        num_scalar_prefetch=4,
        grid=(b, len(pairs)),
        in_specs=[pl.BlockSpec((1, MLA_HEADS, TQ, LANES), lambda i, p, qi, kj, *_: (i, 0, qi[p], 0)),
                  pl.BlockSpec((1, MLA_HEADS, TK, LANES), lambda i, p, qi, kj, *_: (i, 0, kj[p], 0)),
                  pl.BlockSpec((1, MLA_WIDTH, TK), lambda i, p, qi, kj, *_: (i, 0, kj[p])),
                  pl.BlockSpec((1, MLA_HEADS, SUBLANES, TQ), lambda i, p, qi, kj, *_: (i, 0, 0, qi[p]))],
        out_specs=pl.BlockSpec((1, TQ, MLA_WIDTH), lambda i, p, qi, kj, *_: (i, qi[p], 0)),
        scratch_shapes=[pltpu.VMEM((MLA_HEADS, 1, TQ), F32),
                        pltpu.VMEM((MLA_HEADS, 1, TQ), F32),
                        pltpu.VMEM((MLA_HEADS, MLA_V, TQ), F32)],
    )
    return pl.pallas_call(
        _attn_kernel,
        grid_spec=grid_spec,
        out_shape=jax.ShapeDtypeStruct((b, s, MLA_WIDTH), F32),
        compiler_params=_params("parallel", "arbitrary"),
        name="mla_attention",
    )(qi, kj, last, flag, q, k, vt, u)


POOL_HALO = 16


def _post_kernel(x_ref, xa_ref, yb_ref, yc_ref, wp_ref, ps_ref, wo_ref, g_ref, wu_ref, wd_ref, o_ref,
                 halo_sc, ext_sc):
    ts = x_ref.shape[1]
    s_idx = pl.program_id(1)

    @pl.when(s_idx == 0)
    def _():
        halo_sc[...] = jnp.zeros_like(halo_sc)

    xa = xa_ref[0]
    ext_sc[0:POOL_HALO, :] = halo_sc[...]
    ext_sc[POOL_HALO:, :] = xa
    halo_sc[...] = xa_ref[0, ts - POOL_HALO:ts, :]

    e = ext_sc[...]
    sums = []
    step = 1
    for _ in POOL_WINDOWS:
        e = e + pltpu.roll(e, step, 0)
        step *= 2
        sums.append(e[POOL_HALO:])
    lane = lax.broadcasted_iota(jnp.int32, (ts, POOL_WIDTH), 1)
    grp = lane >> 6
    win = jnp.left_shift(2, grp)
    t = s_idx * ts + lax.broadcasted_iota(jnp.int32, (ts, POOL_WIDTH), 0)
    count = jnp.minimum(t + 1, win).astype(F32)
    pooled = jnp.where(grp == 0, sums[0], jnp.where(grp == 1, sums[1], jnp.where(grp == 2, sums[2], sums[3])))
    ya = _dot((pooled / count - xa).astype(BF16), wp_ref[...]) * ps_ref[...]

    acc = x_ref[0] + _dot(ya.astype(BF16), wo_ref[0:POOL_WIDTH, :])
    acc = acc + _dot(yb_ref[0].astype(BF16), wo_ref[POOL_WIDTH:POOL_WIDTH + DN_WIDTH, :])
    x1 = acc + _dot(yc_ref[0].astype(BF16), wo_ref[POOL_WIDTH + DN_WIDTH:, :])

    h = _rms_rows(x1, g_ref[...]).astype(BF16)
    u = jnp.maximum(_dot(h, wu_ref[...]), 0.0)
    o_ref[0] = x1 + _dot((u * u).astype(BF16), wd_ref[...])


def _post(x, xa, yb, yc, wp_bd, pool_scale, w_out, gain, w_up, w_down):
    b, s, _ = x.shape
    ts = TM_PROJ

    def tok(n):
        return pl.BlockSpec((1, ts, n), lambda i, j: (i, j, 0))

    return pl.pallas_call(
        _post_kernel,
        grid=(b, s // ts),
        in_specs=[tok(D_MODEL), tok(POOL_WIDTH), tok(DN_WIDTH), tok(MLA_WIDTH),
                  _const_spec((POOL_WIDTH, POOL_WIDTH)), _const_spec((1, POOL_WIDTH)),
                  _const_spec((D_MODEL, D_MODEL)), _const_spec((1, D_MODEL)),
                  _const_spec((D_MODEL, D_FF)), _const_spec((D_FF, D_MODEL))],
        out_specs=tok(D_MODEL),
        out_shape=jax.ShapeDtypeStruct((b, s, D_MODEL), F32),
        scratch_shapes=[pltpu.VMEM((POOL_HALO, POOL_WIDTH), F32),
                        pltpu.VMEM((ts + POOL_HALO, POOL_WIDTH), F32)],
        compiler_params=_params("parallel", "arbitrary"),
        name="outproj_pool_mlp",
    )(x, xa, yb, yc, wp_bd, pool_scale, w_out, gain, w_up, w_down)


def _head_block(nope, pe):
    z16 = jnp.zeros(pe.shape[:-1] + (HALF_ROPE,), pe.dtype)
    return jnp.concatenate([nope, pe[..., :HALF_ROPE], z16, pe[..., HALF_ROPE:], z16], axis=-1)


def _prep_layer(w_in, pool_w, pool_scale, dn_conv, dn_a_log, dn_dt_bias, dn_norm,
                q_a_norm, w_q_b, kv_a_norm, w_kv_b, q_norm, k_norm, w_out, w_up, w_down):
    offs = np.cumsum((0,) + IN_SPLITS)
    seg = [w_in[:, offs[i]:offs[i + 1]] for i in range(len(IN_SPLITS))]
    w_xa, w_qkv, w_z, w_b, w_a, w_cq, w_ckv, w_kpe = seg
    d = w_in.shape[0]
    z64 = jnp.zeros((d, MLA_NOPE), w_in.dtype)
    w_misc = jnp.concatenate([w_b, w_a, jnp.zeros((d, MLA_NOPE - 2 * DN_HEADS), w_in.dtype),
                              _head_block(z64, w_kpe)[:, MLA_NOPE:]], axis=1)
    w_cat = jnp.concatenate([w_qkv, w_xa, w_z, w_cq, w_ckv, w_misc], axis=1).astype(BF16)

    lane_pad = jnp.zeros((LANES - 2 * DN_HEADS,), F32)
    alog_l = jnp.concatenate([jnp.zeros((DN_HEADS,), F32), dn_a_log, lane_pad])[None]
    dtb_l = jnp.concatenate([jnp.zeros((DN_HEADS,), F32), dn_dt_bias, lane_pad])[None]

    wq = w_q_b.reshape(Q_LORA, MLA_HEADS, MLA_QK_DIM)
    wq = _head_block(wq[..., :MLA_NOPE], wq[..., MLA_NOPE:]).reshape(Q_LORA, MLA_HEADS * LANES).astype(BF16)
    wkv = w_kv_b.reshape(KV_LORA, MLA_HEADS, MLA_NOPE + MLA_V)
    wk = _head_block(wkv[..., :MLA_NOPE], jnp.zeros((KV_LORA, MLA_HEADS, MLA_ROPE), F32))
    wk = wk.reshape(KV_LORA, MLA_HEADS * LANES).astype(BF16)
    wvt = wkv[..., MLA_NOPE:].reshape(KV_LORA, MLA_WIDTH).T.astype(BF16)
    qg = _head_block(q_norm[:MLA_NOPE], q_norm[MLA_NOPE:])[None]
    kg = _head_block(k_norm[:MLA_NOPE], jnp.zeros((MLA_ROPE,), F32))[None]
    kpg = _head_block(jnp.zeros((MLA_NOPE,), F32), k_norm[MLA_NOPE:])[None]

    wp_bd = jax.scipy.linalg.block_diag(*[pool_w[g] for g in range(POOL_GROUPS)]).astype(BF16)
    return dict(w_cat=w_cat, alog_l=alog_l, dtb_l=dtb_l, conv_w=dn_conv, dn_norm=dn_norm[None],
                qag=q_a_norm[None], kvag=kv_a_norm[None], wq=wq, wk=wk, wvt=wvt, qg=qg, kg=kg, kpg=kpg,
                wp_bd=wp_bd, pool_scale=pool_scale[None], w_out=w_out.astype(BF16),
                w_up=w_up.astype(BF16), w_down=w_down.astype(BF16))


def kernel(x, positions, attn_norm, w_in, pool_w, pool_scale, dn_conv, dn_a_log, dn_dt_bias, dn_norm,
           mla_q_a_norm, mla_w_q_b, mla_kv_a_norm, mla_w_kv_b, mla_q_norm, mla_k_norm,
           w_out, mlp_norm, w_up, w_down):
    b, s, d = x.shape
    depth = w_in.shape[0]
    inv_freq = ROPE_THETA ** (-jnp.arange(0, MLA_ROPE, 2, dtype=F32) / MLA_ROPE)
    freq_lanes = _head_block(jnp.zeros((MLA_NOPE,), F32), jnp.concatenate([inv_freq, inv_freq]))[None]
    cos, sina, sinb = _rope_tables(positions, freq_lanes)

    for l in range(depth):
        p = _prep_layer(w_in[l], pool_w[l], pool_scale[l], dn_conv[l], dn_a_log[l], dn_dt_bias[l], dn_norm[l],
                        mla_q_a_norm[l], mla_w_q_b[l], mla_kv_a_norm[l], mla_w_kv_b[l], mla_q_norm[l],
                        mla_k_norm[l], w_out[l], w_up[l], w_down[l])
        qkv, xa, z, cq, ckv, misc = _inproj(x, attn_norm[l][None], p["w_cat"], p["conv_w"])
        y_b = _deltanet(qkv, z, misc, p["alog_l"], p["dtb_l"], p["dn_norm"])
        q, k, vt, qsq, ksq = _mla_prep(cq, ckv, misc, cos, sina, sinb, p["qag"], p["kvag"], p["wq"], p["wk"],
                                       p["wvt"], p["qg"], p["kg"], p["kpg"])
        u, flag = _score_bounds(qsq, ksq)
        y_c = _attention(q, k, vt, u, flag[:, 0, 0])
        x = _post(x, xa, y_b, y_c, p["wp_bd"], p["pool_scale"], p["w_out"], mlp_norm[l][None], p["w_up"], p["w_down"])
    return x
```

```python
import functools
import math

import jax
import jax.numpy as jnp
import numpy as np
from jax import lax
from jax.experimental import pallas as pl
from jax.experimental.pallas import tpu as pltpu

F32 = jnp.float32
BF16 = jnp.bfloat16

D_MODEL = 1024
POOL_GROUPS = 4
POOL_GROUP_DIM = 64
POOL_WIDTH = POOL_GROUPS * POOL_GROUP_DIM
POOL_WINDOWS = (2, 4, 8, 16)
DN_HEADS = 4
DN_HEAD_DIM = 128
DN_WIDTH = DN_HEADS * DN_HEAD_DIM
DN_CONV = 4
DN_CHUNK = 64
MLA_HEADS = 4
MLA_NOPE = 64
MLA_ROPE = 32
MLA_QK_DIM = MLA_NOPE + MLA_ROPE
MLA_V = 64
MLA_WIDTH = MLA_HEADS * MLA_V
Q_LORA = 256
KV_LORA = 128
ROPE_THETA = 10000.0
D_FF = 4 * D_MODEL
EPS = 1e-6
IN_SPLITS = (POOL_WIDTH, 3 * DN_WIDTH, DN_WIDTH, DN_HEADS, DN_HEADS, Q_LORA, KV_LORA, MLA_ROPE)

LANES = 128
SUBLANES = 8
MXU_COLS = 256
VMEM_LIMIT_BYTES = 56 * 1024 * 1024

TM_PROJ = 512
TM_PREP = 1024
DN_TILE = 256
DN_PAIR = 2 * DN_CHUNK
TQ = 1024
TK = 1024
HALF_ROPE = MLA_ROPE // 2
SCORE_BOUND_SLACK = 1.02
SCORE_BOUND_LIMIT = 30.0
PE1_LANE = MLA_NOPE
PE2_LANE = MLA_NOPE + 2 * HALF_ROPE


def _dot(a, b):
    return jnp.dot(a, b, preferred_element_type=F32)


def _dot_nt(a, b):
    return lax.dot_general(a, b, (((1,), (1,)), ((), ())), preferred_element_type=F32)


def _rms_rows(x, gain):
    return x * lax.rsqrt(jnp.mean(x * x, axis=-1, keepdims=True) + EPS) * gain


def _split3(x):
    x1 = x.astype(BF16)
    r1 = x - x1.astype(F32)
    x2 = r1.astype(BF16)
    r2 = r1 - x2.astype(F32)
    return x1, x2, r2.astype(BF16)


def _sigmoid(x):
    return 1.0 / (1.0 + jnp.exp(-x))


def _params(*sem):
    return pltpu.CompilerParams(dimension_semantics=sem, vmem_limit_bytes=VMEM_LIMIT_BYTES)


def _const_spec(shape):
    nd = len(shape)
    return pl.BlockSpec(shape, lambda *_: (0,) * nd, pipeline_mode=pl.Buffered(1))


IN_SEGS = (3 * DN_WIDTH, POOL_WIDTH, DN_WIDTH, Q_LORA, KV_LORA, LANES)
CONV_HALO = SUBLANES


def _inproj_kernel(x_ref, g_ref, w_ref, cw_ref, qkv_ref, *rest):
    o_refs, (halo_sc, ext_sc) = rest[:-2], rest[-2:]
    tm = x_ref.shape[1]
    nqkv = 3 * DN_WIDTH
    D = DN_HEAD_DIM

    @pl.when(pl.program_id(1) == 0)
    def _():
        halo_sc[...] = jnp.zeros_like(halo_sc)

    h = _rms_rows(x_ref[0], g_ref[...]).astype(BF16)
    ext_sc[0:CONV_HALO, :] = halo_sc[...]

    def conv_cols(c0):
        e = ext_sc[:, c0:c0 + D]
        cw = cw_ref[:, c0:c0 + D]
        e1 = pltpu.roll(e, 1, 0)
        near = cw[3:4] * e + cw[2:3] * e1
        far = cw[1:2] * e + cw[0:1] * e1
        acc = (near + pltpu.roll(far, 2, 0))[CONV_HALO:]
        y = acc * _sigmoid(acc)
        if c0 < 2 * DN_WIDTH:
            y = y * lax.rsqrt(jnp.sum(y * y, axis=-1, keepdims=True) + EPS)
            if c0 < DN_WIDTH:
                y = y * (D ** -0.5)
        qkv_ref[0, :, c0:c0 + D] = y

    pieces = []
    off = nqkv
    for o_ref in o_refs:
        n = o_ref.shape[-1]
        pieces += [(o_ref, c, off + c, LANES) for c in range(0, n, LANES)]
        off += n
    per_dot = MXU_COLS // LANES
    others = [pieces[i:i + per_dot] for i in range(0, len(pieces), per_dot)]
    def qkv_cols(c0):
        ext_sc[CONV_HALO:, c0:c0 + MXU_COLS] = _dot(h, w_ref[:, c0:c0 + MXU_COLS])

    nsteps = nqkv // MXU_COLS
    qkv_cols(0)
    for i in range(nsteps):
        if i + 1 < nsteps:
            qkv_cols((i + 1) * MXU_COLS)
        for c in range(i * MXU_COLS, (i + 1) * MXU_COLS, D):
            conv_cols(c)
        for group in others[i::nsteps]:
            w0 = group[0][2]
            res = _dot(h, w_ref[:, w0:w0 + LANES * len(group)])
            for o_ref, c, woff, n in group:
                o_ref[0, :, c:c + n] = res[:, woff - w0:woff - w0 + n]
    halo_sc[...] = ext_sc[tm:tm + CONV_HALO, :]


def _inproj(x, gain, w_cat, conv_w):
    b, s, _ = x.shape
    tm = TM_PROJ
    n_all = sum(IN_SEGS)
    return pl.pallas_call(
        _inproj_kernel,
        grid=(b, s // tm),
        in_specs=[pl.BlockSpec((1, tm, D_MODEL), lambda i, j: (i, j, 0)),
                  _const_spec((1, D_MODEL)),
                  _const_spec((D_MODEL, n_all)),
                  _const_spec((DN_CONV, 3 * DN_WIDTH))],
        out_specs=[pl.BlockSpec((1, tm, n), lambda i, j: (i, j, 0)) for n in IN_SEGS],
        out_shape=[jax.ShapeDtypeStruct((b, s, n), F32) for n in IN_SEGS],
        scratch_shapes=[pltpu.VMEM((CONV_HALO, 3 * DN_WIDTH), F32),
                        pltpu.VMEM((tm + CONV_HALO, 3 * DN_WIDTH), F32)],
        compiler_params=_params("parallel", "arbitrary"),
        name="inproj",
    )(x, gain, w_cat, conv_w)


def _dn_kernel(qkv_ref, z_ref, misc_ref, alog_ref, dtb_ref, ng_ref, y_ref, state_sc):
    nb = qkv_ref.shape[0]
    L = DN_TILE
    C = DN_CHUNK
    nchunk = L // C
    D = DN_HEAD_DIM
    s_idx = pl.program_id(0)
    chains = [(b, h) for b in range(nb) for h in range(DN_HEADS)]

    @pl.when(s_idx == 0)
    def _():
        state_sc[...] = jnp.zeros_like(state_sc)

    def chunk_masks(n):
        ri = lax.broadcasted_iota(jnp.int32, (n, n), 0)
        ci = lax.broadcasted_iota(jnp.int32, (n, n), 1)
        same = (ri >> 6) == (ci >> 6)
        return same, same & (ci <= ri), same & (ci < ri), ri == ci

    same_l, causal_l, _, _ = chunk_masks(L)
    cum_mat = jnp.concatenate([causal_l.astype(BF16), same_l.astype(BF16)], axis=0)
    P = DN_PAIR
    nblk = L // P
    _, causal_bd, strict_bd, diag = chunk_masks(P)
    eye = diag.astype(F32)

    beta_all, gcum_all, glast_all, gcum_t = [], [], [], []
    for b in range(nb):
        misc = misc_ref[b]
        beta_all.append(_sigmoid(misc))
        sp_in = misc + dtb_ref[...]
        softplus = jnp.maximum(sp_in, 0.0) + jnp.log1p(jnp.exp(-jnp.abs(sp_in)))
        g_all = -jnp.exp(alog_ref[...]) * softplus
        g1, g2, g3 = _split3(g_all)
        cum = _dot(cum_mat, g1) + _dot(cum_mat, g2) + _dot(cum_mat, g3)
        gcum_all.append(cum[:L])
        glast_all.append(cum[L:])
        gcum_t.append(cum[:L].T)

    def stage1(b, h):
        a_mats, attns = [], []
        lane = DN_HEADS + h
        gc_col = gcum_all[b][:, lane:lane + 1]
        gl_col = glast_all[b][:, lane:lane + 1]
        gc_row = gcum_t[b][lane:lane + 1, :]
        beta = beta_all[b][:, h:h + 1]
        qn = qkv_ref[b, :, h * D:(h + 1) * D]
        kn = qkv_ref[b, :, DN_WIDTH + h * D:DN_WIDTH + (h + 1) * D]
        v = qkv_ref[b, :, 2 * DN_WIDTH + h * D:2 * DN_WIDTH + (h + 1) * D]
        kb = kn * beta
        e_col = jnp.exp(gc_col)
        kn16 = kn.astype(BF16)
        kb16 = kb.astype(BF16)
        qn16 = qn.astype(BF16)
        for j in range(nblk):
            rows = slice(j * P, (j + 1) * P)
            decay = jnp.exp(jnp.where(causal_bd, gc_col[rows] - gc_row[:, rows], 0.0))
            gram = _dot_nt(jnp.concatenate([kb16[rows], qn16[rows]], axis=0), kn16[rows])
            a_mats.append(jnp.where(strict_bd, gram[:P] * decay, 0.0))
            attns.append(jnp.where(causal_bd, gram[P:] * decay, 0.0).astype(BF16))
        return dict(a=a_mats, attn=attns,
                    rhs=jnp.concatenate([v * beta, kb * e_col], axis=1).astype(BF16),
                    qd=(qn * e_col).astype(BF16),
                    kdt=(kn * jnp.exp(gl_col - gc_col)).T.astype(BF16),
                    gdec=jnp.exp(jnp.broadcast_to(gl_col, (L, D))))

    def stage2(group):
        a_mats = [a for ch in group for a in ch["a"]]
        xps = [(-a).astype(BF16) for a in a_mats]
        t_invs = [eye - a for a in a_mats]
        xps = [_dot(xp, xp).astype(BF16) for xp in xps]
        for _ in range(4):
            prods = [_dot(jnp.concatenate([t.astype(BF16), xp], axis=0), xp) for t, xp in zip(t_invs, xps)]
            t_invs = [t + pr[:P] for t, pr in zip(t_invs, prods)]
            xps = [pr[P:].astype(BF16) for pr in prods]
        t_invs = [t + _dot(t.astype(BF16), xp) for t, xp in zip(t_invs, xps)]
        for i, ch in enumerate(group):
            ch["uw"] = [_dot(t_invs[i * nblk + j].astype(BF16), ch["rhs"][j * P:(j + 1) * P])
                        for j in range(nblk)]

    groups = []
    for b in range(nb):
        if groups:
            stage2(groups[-1])
        groups.append([stage1(b, h) for h in range(DN_HEADS)])
    stage2(groups[-1])
    chs = [ch for group in groups for ch in group]

    states = [state_sc[b, h] for b, h in chains]
    o_parts = [[] for _ in chains]
    zeros_c = jnp.zeros((C, D), BF16)
    for c in range(nchunk):
        r0 = c * C
        j, half = divmod(c, P // C)
        p0 = half * C
        rs = [_dot(jnp.concatenate([ch["uw"][j][p0:p0 + C, D:].astype(BF16), ch["qd"][r0:r0 + C]], axis=0),
                   states[i].astype(BF16)) for i, ch in enumerate(chs)]
        for i, ch in enumerate(chs):
            v_new = (ch["uw"][j][p0:p0 + C, :D] - rs[i][:C]).astype(BF16)
            v_blk = jnp.concatenate([zeros_c] * half + [v_new] + [zeros_c] * (P // C - 1 - half), axis=0)
            lhs = jnp.concatenate([ch["attn"][j][p0:p0 + C, :], ch["kdt"][:, j * P:(j + 1) * P]], axis=0)
            m2 = _dot(lhs, v_blk)
            o_parts[i].append(rs[i][C:] + m2[:C])
            states[i] = states[i] * ch["gdec"][r0:r0 + 1, :] + m2[C:]

    for i, (b, h) in enumerate(chains):
        state_sc[b, h] = states[i]
        o = jnp.concatenate(o_parts[i], axis=0)
        zh = z_ref[b, :, h * D:(h + 1) * D]
        y_ref[b, :, h * D:(h + 1) * D] = (_rms_rows(o, ng_ref[...]) * (zh * _sigmoid(zh))).astype(y_ref.dtype)


def _deltanet(qkv, z, misc, alog_l, dtb_l, norm_gain):
    b, s, _ = qkv.shape
    L = DN_TILE
    return pl.pallas_call(
        _dn_kernel,
        grid=(s // L,),
        in_specs=[pl.BlockSpec((b, L, 3 * DN_WIDTH), lambda j: (0, j, 0)),
                  pl.BlockSpec((b, L, DN_WIDTH), lambda j: (0, j, 0)),
                  pl.BlockSpec((b, L, LANES), lambda j: (0, j, 0)),
                  _const_spec((1, LANES)),
                  _const_spec((1, LANES)),
                  _const_spec((1, DN_HEAD_DIM))],
        out_specs=pl.BlockSpec((b, L, DN_WIDTH), lambda j: (0, j, 0)),
        out_shape=jax.ShapeDtypeStruct((b, s, DN_WIDTH), BF16),
        scratch_shapes=[pltpu.VMEM((b, DN_HEADS, DN_HEAD_DIM, DN_HEAD_DIM), F32)],
        compiler_params=_params("arbitrary"),
        name="deltanet",
    )(qkv, z, misc, alog_l, dtb_l, norm_gain)


def _rope_kernel(pos_ref, freq_ref, cos_ref, sina_ref, sinb_ref):
    tm = pos_ref.shape[1]
    ang = pos_ref[0].astype(F32) * freq_ref[...]
    lane = lax.broadcasted_iota(jnp.int32, (tm, LANES), 1)
    pe1 = (lane >= PE1_LANE) & (lane < PE1_LANE + HALF_ROPE)
    pe2 = (lane >= PE2_LANE) & (lane < PE2_LANE + HALF_ROPE)
    cos = jnp.cos(ang)
    sin = jnp.sin(ang)
    cos_ref[0] = jnp.where(lane < MLA_NOPE, 1.0, jnp.where(pe1 | pe2, cos, 0.0))
    sina_ref[0] = jnp.where(pe2, sin, 0.0)
    sinb_ref[0] = jnp.where(pe1, -sin, 0.0)


def _rope_tables(positions, freq_lanes):
    b, s = positions.shape
    tm = TM_PREP
    spec = pl.BlockSpec((1, tm, LANES), lambda i, j: (i, j, 0))
    return pl.pallas_call(
        _rope_kernel,
        grid=(b, s // tm),
        in_specs=[pl.BlockSpec((1, tm, 1), lambda i, j: (i, j, 0)), _const_spec((1, LANES))],
        out_specs=[spec, spec, spec],
        out_shape=[jax.ShapeDtypeStruct((b, s, LANES), F32)] * 3,
        compiler_params=_params("parallel", "parallel"),
        name="rope_tables",
    )(positions.reshape(b, s, 1), freq_lanes)


def _rope(x, cos, sina, sinb):
    return x * cos + pltpu.roll(x, 2 * HALF_ROPE, 1) * sina + pltpu.roll(x, LANES - 2 * HALF_ROPE, 1) * sinb


def _mla_prep_kernel(cq_ref, ckv_ref, misc_ref, cos_ref, sina_ref, sinb_ref,
                     qag_ref, kvag_ref, wq_ref, wk_ref, wvt_ref, qg_ref, kg_ref, kpg_ref,
                     q_ref, k_ref, vt_ref, qsq_ref, ksq_ref):
    tm = cq_ref.shape[1]
    ones8 = jnp.ones((SUBLANES, LANES), BF16)

    def row_sq_norms(x16):
        xf = x16.astype(F32)
        return _dot_nt(ones8, (xf * xf).astype(BF16))

    cos = cos_ref[0]
    sina = sina_ref[0]
    sinb = sinb_ref[0]
    lane = lax.broadcasted_iota(jnp.int32, (tm, LANES), 1)
    is_nope = lane < MLA_NOPE

    cqn = _rms_rows(cq_ref[0], qag_ref[...]).astype(BF16)
    ckvn = _rms_rows(ckv_ref[0], kvag_ref[...]).astype(BF16)
    qf = _dot(cqn, wq_ref[...])
    kf = _dot(ckvn, wk_ref[...])
    vt_ref[0] = _dot_nt(wvt_ref[...], ckvn).astype(BF16)

    kp = jnp.where(is_nope, 0.0, misc_ref[0])
    kp_ms = jnp.sum(kp * kp, axis=-1, keepdims=True) * (1.0 / MLA_ROPE)
    kp = _rope(kp * lax.rsqrt(kp_ms + EPS) * kpg_ref[...], cos, sina, sinb)

    scale = MLA_QK_DIM ** -0.5 * math.log2(math.e)
    for h in range(MLA_HEADS):
        xq = qf[:, h * LANES:(h + 1) * LANES]
        sq = xq * xq
        ms_n = jnp.sum(jnp.where(is_nope, sq, 0.0), axis=-1, keepdims=True) * (1.0 / MLA_NOPE)
        ms_p = jnp.sum(jnp.where(is_nope, 0.0, sq), axis=-1, keepdims=True) * (1.0 / MLA_ROPE)
        inv = jnp.where(is_nope, lax.rsqrt(ms_n + EPS), lax.rsqrt(ms_p + EPS))
        qh = _rope(xq * inv * qg_ref[...], cos, sina, sinb) * scale
        q16 = qh.astype(BF16)
        q_ref[0, h] = q16
        qsq_ref[0, h] = row_sq_norms(q16)

        xk = kf[:, h * LANES:(h + 1) * LANES]
        ms_k = jnp.sum(xk * xk, axis=-1, keepdims=True) * (1.0 / MLA_NOPE)
        kh = xk * lax.rsqrt(ms_k + EPS) * kg_ref[...] + kp
        k16 = kh.astype(BF16)
        k_ref[0, h] = k16
        ksq_ref[0, h] = row_sq_norms(k16)


def _mla_prep(cq, ckv, misc, cos, sina, sinb, qag, kvag, wq, wk, wvt, qg, kg, kpg):
    b, s, _ = cq.shape
    tm = TM_PREP
    hl = MLA_HEADS * LANES

    def tok(n):
        return pl.BlockSpec((1, tm, n), lambda i, j: (i, j, 0))

    return pl.pallas_call(
        _mla_prep_kernel,
        grid=(b, s // tm),
        in_specs=[tok(Q_LORA), tok(KV_LORA), tok(LANES), tok(LANES), tok(LANES), tok(LANES),
                  _const_spec((1, Q_LORA)), _const_spec((1, KV_LORA)),
                  _const_spec((Q_LORA, hl)), _const_spec((KV_LORA, hl)), _const_spec((MLA_WIDTH, KV_LORA)),
                  _const_spec((1, LANES)), _const_spec((1, LANES)), _const_spec((1, LANES))],
        out_specs=[pl.BlockSpec((1, MLA_HEADS, tm, LANES), lambda i, j: (i, 0, j, 0)),
                   pl.BlockSpec((1, MLA_HEADS, tm, LANES), lambda i, j: (i, 0, j, 0)),
                   pl.BlockSpec((1, MLA_WIDTH, tm), lambda i, j: (i, 0, j)),
                   pl.BlockSpec((1, MLA_HEADS, SUBLANES, tm), lambda i, j: (i, 0, 0, j)),
                   pl.BlockSpec((1, MLA_HEADS, SUBLANES, tm), lambda i, j: (i, 0, 0, j))],
        out_shape=[jax.ShapeDtypeStruct((b, MLA_HEADS, s, LANES), BF16),
                   jax.ShapeDtypeStruct((b, MLA_HEADS, s, LANES), BF16),
                   jax.ShapeDtypeStruct((b, MLA_WIDTH, s), BF16),
                   jax.ShapeDtypeStruct((b, MLA_HEADS, SUBLANES, s), F32),
                   jax.ShapeDtypeStruct((b, MLA_HEADS, SUBLANES, s), F32)],
        compiler_params=_params("parallel", "parallel"),
        name="mla_prep",
    )(cq, ckv, misc, cos, sina, sinb, qag, kvag, wq, wk, wvt, qg, kg, kpg)


def _bounds_kernel(qsq_ref, ksq_ref, u_ref, flag_ref):
    worst = None
    for h in range(MLA_HEADS):
        kmax = jnp.max(ksq_ref[0, h], axis=-1, keepdims=True)
        u = jnp.sqrt(qsq_ref[0, h] * kmax) * SCORE_BOUND_SLACK
        u_ref[0, h] = u
        umax = jnp.max(u, axis=-1, keepdims=True)
        worst = umax if worst is None else jnp.maximum(worst, umax)
    flag_ref[0] = jnp.broadcast_to((worst <= SCORE_BOUND_LIMIT).astype(jnp.int32), flag_ref.shape[1:])


def _score_bounds(qsq, ksq):
    b, h, r, s = qsq.shape
    spec = pl.BlockSpec((1, h, r, s), lambda i: (i, 0, 0, 0))
    return pl.pallas_call(
        _bounds_kernel,
        grid=(b,),
        in_specs=[spec, spec],
        out_specs=[spec, pl.BlockSpec((1, SUBLANES, LANES), lambda i: (i, 0, 0))],
        out_shape=[jax.ShapeDtypeStruct((b, h, r, s), F32), jax.ShapeDtypeStruct((b, SUBLANES, LANES), jnp.int32)],
        compiler_params=_params("parallel"),
        name="score_bounds",
    )(qsq, ksq)


def _attn_block(q_ref, k_ref, vt_ref, u_ref, m_sc, l_sc, acc_sc, visible, bounded):
    scores = [_dot_nt(k_ref[0, 0], q_ref[0, 0])]
    for h in range(MLA_HEADS):
        if h + 1 < MLA_HEADS:
            scores.append(_dot_nt(k_ref[0, h + 1], q_ref[0, h + 1]))
        st = scores[h]
        if visible is not None:
            st = jnp.where(visible, st, -jnp.inf)
        vt = vt_ref[0, h * MLA_V:(h + 1) * MLA_V, :]
        if bounded:
            pt = jnp.exp2(st - u_ref[0, h, 0:1, :])
            l_sc[h] = l_sc[h] + jnp.sum(pt, axis=0, keepdims=True)
            acc_sc[h] = acc_sc[h] + _dot(vt, pt.astype(BF16))
        else:
            m_prev = m_sc[h]
            m_new = jnp.maximum(m_prev, jnp.max(st, axis=0, keepdims=True))
            alpha = jnp.exp2(m_prev - m_new)
            pt = jnp.exp2(st - m_new)
            l_sc[h] = alpha * l_sc[h] + jnp.sum(pt, axis=0, keepdims=True)
            acc_sc[h] = alpha * acc_sc[h] + _dot(vt, pt.astype(BF16))
            m_sc[h] = m_new


def _attn_diag_block(q_ref, k_ref, vt_ref, u_ref, l_sc, acc_sc):
    hk = TK // 2
    r = lax.broadcasted_iota(jnp.int32, (hk, TQ), 0)
    c = lax.broadcasted_iota(jnp.int32, (hk, TQ), 1)
    vis_old = r <= c
    vis_new = vis_old[:, :hk]

    def scores(h):
        return (_dot_nt(k_ref[0, h, 0:hk, :], q_ref[0, h]),
                _dot_nt(k_ref[0, h, hk:, :], q_ref[0, h, hk:, :]))

    nxt = scores(0)
    for h in range(MLA_HEADS):
        s_old, s_new = nxt
        if h + 1 < MLA_HEADS:
            nxt = scores(h + 1)
        u = u_ref[0, h, 0:1, :]
        p_old = jnp.exp2(jnp.where(vis_old, s_old, -jnp.inf) - u)
        p_new = jnp.exp2(jnp.where(vis_new, s_new, -jnp.inf) - u[:, hk:])
        vt = vt_ref[0, h * MLA_V:(h + 1) * MLA_V, :]
        l_sc[h] = l_sc[h] + jnp.sum(p_old, axis=0, keepdims=True)
        acc_sc[h] = acc_sc[h] + _dot(vt[:, :hk], p_old.astype(BF16))
        l_sc[h, :, hk:] = l_sc[h, :, hk:] + jnp.sum(p_new, axis=0, keepdims=True)
        acc_sc[h, :, hk:] = acc_sc[h, :, hk:] + _dot(vt[:, hk:], p_new.astype(BF16))


def _attn_kernel(qi_ref, kj_ref, last_ref, flag_ref, q_ref, k_ref, vt_ref, u_ref, o_ref, m_sc, l_sc, acc_sc):
    p = pl.program_id(1)
    qi = qi_ref[p]
    kj = kj_ref[p]
    bounded = flag_ref[pl.program_id(0)] == 1

    @pl.when(kj == 0)
    def _():
        m_sc[...] = jnp.full_like(m_sc, -jnp.inf)
        l_sc[...] = jnp.zeros_like(l_sc)
        acc_sc[...] = jnp.zeros_like(acc_sc)

    all_visible = kj * TK + (TK - 1) <= qi * TQ

    def run(visible_fn, use_bound):
        def body():
            _attn_block(q_ref, k_ref, vt_ref, u_ref, m_sc, l_sc, acc_sc, visible_fn(), use_bound)
        return body

    def causal_mask():
        kpos = kj * TK + lax.broadcasted_iota(jnp.int32, (TK, TQ), 0)
        qpos = qi * TQ + lax.broadcasted_iota(jnp.int32, (TK, TQ), 1)
        return kpos <= qpos

    partly = jnp.logical_not(all_visible)
    unbounded = jnp.logical_not(bounded)
    pl.when(all_visible & bounded)(run(lambda: None, True))
    if TQ == TK:
        pl.when(partly & bounded)(lambda: _attn_diag_block(q_ref, k_ref, vt_ref, u_ref, l_sc, acc_sc))
    else:
        pl.when(partly & bounded)(run(causal_mask, True))
    pl.when(all_visible & unbounded)(run(lambda: None, False))
    pl.when(partly & unbounded)(run(causal_mask, False))

    @pl.when(last_ref[p] == 1)
    def _():
        out_t = jnp.concatenate([acc_sc[h] / l_sc[h] for h in range(MLA_HEADS)], axis=0)
        o_ref[0] = out_t.T.astype(o_ref.dtype)


def _attention(q, k, vt, u, flag):
    b, _, s, _ = q.shape
    nq = s // TQ
    pairs = [(i, j) for i in range(nq) for j in range((i * TQ + TQ - 1) // TK + 1)]
    qi = jnp.asarray(np.array([p[0] for p in pairs], np.int32))
    kj = jnp.asarray(np.array([p[1] for p in pairs], np.int32))
    last = jnp.asarray(np.array([int(p[1] == (p[0] * TQ + TQ - 1) // TK) for p in pairs], np.int32))
    grid_spec = pltpu.PrefetchScalarGridSpec(
        num_scalar_prefetch=4,
        grid=(b, len(pairs)),
        in_specs=[pl.BlockSpec((1, MLA_HEADS, TQ, LANES), lambda i, p, qi, kj, *_: (i, 0, qi[p], 0)),
                  pl.BlockSpec((1, MLA_HEADS, TK, LANES), lambda i, p, qi, kj, *_: (i, 0, kj[p], 0)),
                  pl.BlockSpec((1, MLA_WIDTH, TK), lambda i, p, qi, kj, *_: (i, 0, kj[p])),
                  pl.BlockSpec((1, MLA_HEADS, SUBLANES, TQ), lambda i, p, qi, kj, *_: (i, 0, 0, qi[p]))],
        out_specs=pl.BlockSpec((1, TQ, MLA_WIDTH), lambda i, p, qi, kj, *_: (i, qi[p], 0)),
        scratch_shapes=[pltpu.VMEM((MLA_HEADS, 1, TQ), F32),
                        pltpu.VMEM((MLA_HEADS, 1, TQ), F32),
                        pltpu.VMEM((MLA_HEADS, MLA_V, TQ), F32)],
    )
    return pl.pallas_call(
        _attn_kernel,
        grid_spec=grid_spec,
        out_shape=jax.ShapeDtypeStruct((b, s, MLA_WIDTH), BF16),
        compiler_params=_params("parallel", "arbitrary"),
        name="mla_attention",
    )(qi, kj, last, flag, q, k, vt, u)


POOL_HALO = 16


def _post_kernel(x_ref, xa_ref, yb_ref, yc_ref, wp_ref, ps_ref, wo_ref, g_ref, wu_ref, wd_ref, o_ref,
                 halo_sc, ext_sc):
    ts = x_ref.shape[1]
    s_idx = pl.program_id(1)

    @pl.when(s_idx == 0)
    def _():
        halo_sc[...] = jnp.zeros_like(halo_sc)

    xa = xa_ref[0]
    ext_sc[0:POOL_HALO, :] = halo_sc[...]
    ext_sc[POOL_HALO:, :] = xa
    halo_sc[...] = xa_ref[0, ts - POOL_HALO:ts, :]

    e = ext_sc[...]
    sums = []
    step = 1
    for _ in POOL_WINDOWS:
        e = e + pltpu.roll(e, step, 0)
        step *= 2
        sums.append(e[POOL_HALO:])
    lane = lax.broadcasted_iota(jnp.int32, (ts, POOL_WIDTH), 1)
    grp = lane >> 6
    win = jnp.left_shift(2, grp)
    t = s_idx * ts + lax.broadcasted_iota(jnp.int32, (ts, POOL_WIDTH), 0)
    count = jnp.minimum(t + 1, win).astype(F32)
    pooled = jnp.where(grp == 0, sums[0], jnp.where(grp == 1, sums[1], jnp.where(grp == 2, sums[2], sums[3])))
    ya = _dot((pooled / count - xa).astype(BF16), wp_ref[...]) * ps_ref[...]

    acc = x_ref[0] + _dot(ya.astype(BF16), wo_ref[0:POOL_WIDTH, :])
    acc = acc + _dot(yb_ref[0].astype(BF16), wo_ref[POOL_WIDTH:POOL_WIDTH + DN_WIDTH, :])
    x1 = acc + _dot(yc_ref[0].astype(BF16), wo_ref[POOL_WIDTH + DN_WIDTH:, :])

    h = _rms_rows(x1, g_ref[...]).astype(BF16)
    u = jnp.maximum(_dot(h, wu_ref[...]), 0.0)
    o_ref[0] = x1 + _dot((u * u).astype(BF16), wd_ref[...])


def _post(x, xa, yb, yc, wp_bd, pool_scale, w_out, gain, w_up, w_down):
    b, s, _ = x.shape
    ts = TM_PROJ

    def tok(n):
        return pl.BlockSpec((1, ts, n), lambda i, j: (i, j, 0))

    return pl.pallas_call(
        _post_kernel,
        grid=(b, s // ts),
        in_specs=[tok(D_MODEL), tok(POOL_WIDTH), tok(DN_WIDTH), tok(MLA_WIDTH),
                  _const_spec((POOL_WIDTH, POOL_WIDTH)), _const_spec((1, POOL_WIDTH)),
                  _const_spec((D_MODEL, D_MODEL)), _const_spec((1, D_MODEL)),
                  _const_spec((D_MODEL, D_FF)), _const_spec((D_FF, D_MODEL))],
        out_specs=tok(D_MODEL),
        out_shape=jax.ShapeDtypeStruct((b, s, D_MODEL), F32),
        scratch_shapes=[pltpu.VMEM((POOL_HALO, POOL_WIDTH), F32),
                        pltpu.VMEM((ts + POOL_HALO, POOL_WIDTH), F32)],
        compiler_params=_params("parallel", "arbitrary"),
        name="outproj_pool_mlp",
    )(x, xa, yb, yc, wp_bd, pool_scale, w_out, gain, w_up, w_down)


def _head_block(nope, pe):
    z16 = jnp.zeros(pe.shape[:-1] + (HALF_ROPE,), pe.dtype)
    return jnp.concatenate([nope, pe[..., :HALF_ROPE], z16, pe[..., HALF_ROPE:], z16], axis=-1)


def _prep_layer(w_in, pool_w, pool_scale, dn_conv, dn_a_log, dn_dt_bias, dn_norm,
                q_a_norm, w_q_b, kv_a_norm, w_kv_b, q_norm, k_norm, w_out, w_up, w_down):
    offs = np.cumsum((0,) + IN_SPLITS)
    seg = [w_in[:, offs[i]:offs[i + 1]] for i in range(len(IN_SPLITS))]
    w_xa, w_qkv, w_z, w_b, w_a, w_cq, w_ckv, w_kpe = seg
    d = w_in.shape[0]
    z64 = jnp.zeros((d, MLA_NOPE), w_in.dtype)
    w_misc = jnp.concatenate([w_b, w_a, jnp.zeros((d, MLA_NOPE - 2 * DN_HEADS), w_in.dtype),
                              _head_block(z64, w_kpe)[:, MLA_NOPE:]], axis=1)
    w_cat = jnp.concatenate([w_qkv, w_xa, w_z, w_cq, w_ckv, w_misc], axis=1).astype(BF16)

    lane_pad = jnp.zeros((LANES - 2 * DN_HEADS,), F32)
    alog_l = jnp.concatenate([jnp.zeros((DN_HEADS,), F32), dn_a_log, lane_pad])[None]
    dtb_l = jnp.concatenate([jnp.zeros((DN_HEADS,), F32), dn_dt_bias, lane_pad])[None]

    wq = w_q_b.reshape(Q_LORA, MLA_HEADS, MLA_QK_DIM)
    wq = _head_block(wq[..., :MLA_NOPE], wq[..., MLA_NOPE:]).reshape(Q_LORA, MLA_HEADS * LANES).astype(BF16)
    wkv = w_kv_b.reshape(KV_LORA, MLA_HEADS, MLA_NOPE + MLA_V)
    wk = _head_block(wkv[..., :MLA_NOPE], jnp.zeros((KV_LORA, MLA_HEADS, MLA_ROPE), F32))
    wk = wk.reshape(KV_LORA, MLA_HEADS * LANES).astype(BF16)
    wvt = wkv[..., MLA_NOPE:].reshape(KV_LORA, MLA_WIDTH).T.astype(BF16)
    qg = _head_block(q_norm[:MLA_NOPE], q_norm[MLA_NOPE:])[None]
    kg = _head_block(k_norm[:MLA_NOPE], jnp.zeros((MLA_ROPE,), F32))[None]
    kpg = _head_block(jnp.zeros((MLA_NOPE,), F32), k_norm[MLA_NOPE:])[None]

    wp_bd = jax.scipy.linalg.block_diag(*[pool_w[g] for g in range(POOL_GROUPS)]).astype(BF16)
    return dict(w_cat=w_cat, alog_l=alog_l, dtb_l=dtb_l, conv_w=dn_conv, dn_norm=dn_norm[None],
                qag=q_a_norm[None], kvag=kv_a_norm[None], wq=wq, wk=wk, wvt=wvt, qg=qg, kg=kg, kpg=kpg,
                wp_bd=wp_bd, pool_scale=pool_scale[None], w_out=w_out.astype(BF16),
                w_up=w_up.astype(BF16), w_down=w_down.astype(BF16))


def kernel(x, positions, attn_norm, w_in, pool_w, pool_scale, dn_conv, dn_a_log, dn_dt_bias, dn_norm,
           mla_q_a_norm, mla_w_q_b, mla_kv_a_norm, mla_w_kv_b, mla_q_norm, mla_k_norm,
           w_out, mlp_norm, w_up, w_down):
    b, s, d = x.shape
    depth = w_in.shape[0]
    inv_freq = ROPE_THETA ** (-jnp.arange(0, MLA_ROPE, 2, dtype=F32) / MLA_ROPE)
    freq_lanes = _head_block(jnp.zeros((MLA_NOPE,), F32), jnp.concatenate([inv_freq, inv_freq]))[None]
    cos, sina, sinb = _rope_tables(positions, freq_lanes)

    for l in range(depth):
        p = _prep_layer(w_in[l], pool_w[l], pool_scale[l], dn_conv[l], dn_a_log[l], dn_dt_bias[l], dn_norm[l],
                        mla_q_a_norm[l], mla_w_q_b[l], mla_kv_a_norm[l], mla_w_kv_b[l], mla_q_norm[l],
                        mla_k_norm[l], w_out[l], w_up[l], w_down[l])
        qkv, xa, z, cq, ckv, misc = _inproj(x, attn_norm[l][None], p["w_cat"], p["conv_w"])
        y_b = _deltanet(qkv, z, misc, p["alog_l"], p["dtb_l"], p["dn_norm"])
        q, k, vt, qsq, ksq = _mla_prep(cq, ckv, misc, cos, sina, sinb, p["qag"], p["kvag"], p["wq"], p["wk"],
                                       p["wvt"], p["qg"], p["kg"], p["kpg"])
        u, flag = _score_bounds(qsq, ksq)
        y_c = _attention(q, k, vt, u, flag[:, 0, 0])
        x = _post(x, xa, y_b, y_c, p["wp_bd"], p["pool_scale"], p["w_out"], mlp_norm[l][None], p["w_up"], p["w_down"])
    return x
```

```python
import functools
import math

import jax
import jax.numpy as jnp
import numpy as np
from jax import lax
from jax.experimental import pallas as pl
from jax.experimental.pallas import tpu as pltpu

F32 = jnp.float32
BF16 = jnp.bfloat16

D_MODEL = 1024
POOL_GROUPS = 4
POOL_GROUP_DIM = 64
POOL_WIDTH = POOL_GROUPS * POOL_GROUP_DIM
POOL_WINDOWS = (2, 4, 8, 16)
DN_HEADS = 4
DN_HEAD_DIM = 128
DN_WIDTH = DN_HEADS * DN_HEAD_DIM
DN_CONV = 4
DN_CHUNK = 64
MLA_HEADS = 4
MLA_NOPE = 64
MLA_ROPE = 32
MLA_QK_DIM = MLA_NOPE + MLA_ROPE
MLA_V = 64
MLA_WIDTH = MLA_HEADS * MLA_V
Q_LORA = 256
KV_LORA = 128
ROPE_THETA = 10000.0
D_FF = 4 * D_MODEL
EPS = 1e-6
IN_SPLITS = (POOL_WIDTH, 3 * DN_WIDTH, DN_WIDTH, DN_HEADS, DN_HEADS, Q_LORA, KV_LORA, MLA_ROPE)

LANES = 128
SUBLANES = 8
MXU_COLS = 256
VMEM_LIMIT_BYTES = 56 * 1024 * 1024

TM_PROJ = 512
TM_PREP = 1024
DN_TILE = 256
DN_PAIR = 2 * DN_CHUNK
TQ = 1024
TK = 1024
HALF_ROPE = MLA_ROPE // 2
SCORE_BOUND_SLACK = 1.02
SCORE_BOUND_LIMIT = 30.0
PE1_LANE = MLA_NOPE
PE2_LANE = MLA_NOPE + 2 * HALF_ROPE


def _dot(a, b):
    return jnp.dot(a, b, preferred_element_type=F32)


def _dot_nt(a, b):
    return lax.dot_general(a, b, (((1,), (1,)), ((), ())), preferred_element_type=F32)


def _rms_rows(x, gain):
    return x * lax.rsqrt(jnp.mean(x * x, axis=-1, keepdims=True) + EPS) * gain


def _split3(x):
    x1 = x.astype(BF16)
    r1 = x - x1.astype(F32)
    x2 = r1.astype(BF16)
    r2 = r1 - x2.astype(F32)
    return x1, x2, r2.astype(BF16)


def _sigmoid(x):
    return 1.0 / (1.0 + jnp.exp(-x))


def _params(*sem):
    return pltpu.CompilerParams(dimension_semantics=sem, vmem_limit_bytes=VMEM_LIMIT_BYTES)


def _const_spec(shape):
    nd = len(shape)
    return pl.BlockSpec(shape, lambda *_: (0,) * nd, pipeline_mode=pl.Buffered(1))


IN_SEGS = (3 * DN_WIDTH, POOL_WIDTH, DN_WIDTH, Q_LORA, KV_LORA, LANES)
CONV_HALO = SUBLANES


def _inproj_kernel(x_ref, g_ref, w_ref, cw_ref, qkv_ref, *rest):
    o_refs, (halo_sc, ext_sc) = rest[:-2], rest[-2:]
    tm = x_ref.shape[1]
    nqkv = 3 * DN_WIDTH
    D = DN_HEAD_DIM

    @pl.when(pl.program_id(1) == 0)
    def _():
        halo_sc[...] = jnp.zeros_like(halo_sc)

    h = _rms_rows(x_ref[0], g_ref[...]).astype(BF16)
    ext_sc[0:CONV_HALO, :] = halo_sc[...]

    def conv_cols(c0):
        e = ext_sc[:, c0:c0 + D]
        cw = cw_ref[:, c0:c0 + D]
        e1 = pltpu.roll(e, 1, 0)
        near = cw[3:4] * e + cw[2:3] * e1
        far = cw[1:2] * e + cw[0:1] * e1
        acc = (near + pltpu.roll(far, 2, 0))[CONV_HALO:]
        y = acc * _sigmoid(acc)
        if c0 < 2 * DN_WIDTH:
            y = y * lax.rsqrt(jnp.sum(y * y, axis=-1, keepdims=True) + EPS)
            if c0 < DN_WIDTH:
                y = y * (D ** -0.5)
        qkv_ref[0, :, c0:c0 + D] = y

    pieces = []
    off = nqkv
    for o_ref in o_refs:
        n = o_ref.shape[-1]
        pieces += [(o_ref, c, off + c, LANES) for c in range(0, n, LANES)]
        off += n
    per_dot = MXU_COLS // LANES
    others = [pieces[i:i + per_dot] for i in range(0, len(pieces), per_dot)]
    def qkv_cols(c0):
        ext_sc[CONV_HALO:, c0:c0 + MXU_COLS] = _dot(h, w_ref[:, c0:c0 + MXU_COLS])

    nsteps = nqkv // MXU_COLS
    qkv_cols(0)
    for i in range(nsteps):
        if i + 1 < nsteps:
            qkv_cols((i + 1) * MXU_COLS)
        for c in range(i * MXU_COLS, (i + 1) * MXU_COLS, D):
            conv_cols(c)
        for group in others[i::nsteps]:
            w0 = group[0][2]
            res = _dot(h, w_ref[:, w0:w0 + LANES * len(group)])
            for o_ref, c, woff, n in group:
                o_ref[0, :, c:c + n] = res[:, woff - w0:woff - w0 + n]
    halo_sc[...] = ext_sc[tm:tm + CONV_HALO, :]


def _inproj(x, gain, w_cat, conv_w):
    b, s, _ = x.shape
    tm = TM_PROJ
    n_all = sum(IN_SEGS)
    return pl.pallas_call(
        _inproj_kernel,
        grid=(b, s // tm),
        in_specs=[pl.BlockSpec((1, tm, D_MODEL), lambda i, j: (i, j, 0)),
                  _const_spec((1, D_MODEL)),
                  _const_spec((D_MODEL, n_all)),
                  _const_spec((DN_CONV, 3 * DN_WIDTH))],
        out_specs=[pl.BlockSpec((1, tm, n), lambda i, j: (i, j, 0)) for n in IN_SEGS],
        out_shape=[jax.ShapeDtypeStruct((b, s, n), F32) for n in IN_SEGS],
        scratch_shapes=[pltpu.VMEM((CONV_HALO, 3 * DN_WIDTH), F32),
                        pltpu.VMEM((tm + CONV_HALO, 3 * DN_WIDTH), F32)],
        compiler_params=_params("parallel", "arbitrary"),
        name="inproj",
    )(x, gain, w_cat, conv_w)


def _dn_kernel(qkv_ref, z_ref, misc_ref, alog_ref, dtb_ref, ng_ref, y_ref, state_sc):
    nb = qkv_ref.shape[0]
    L = DN_TILE
    C = DN_CHUNK
    nchunk = L // C
    D = DN_HEAD_DIM
    s_idx = pl.program_id(0)
    chains = [(b, h) for b in range(nb) for h in range(DN_HEADS)]

    @pl.when(s_idx == 0)
    def _():
        state_sc[...] = jnp.zeros_like(state_sc)

    def chunk_masks(n):
        ri = lax.broadcasted_iota(jnp.int32, (n, n), 0)
        ci = lax.broadcasted_iota(jnp.int32, (n, n), 1)
        same = (ri >> 6) == (ci >> 6)
        return same, same & (ci <= ri), same & (ci < ri), ri == ci

    same_l, causal_l, _, _ = chunk_masks(L)
    cum_mat = jnp.concatenate([causal_l.astype(BF16), same_l.astype(BF16)], axis=0)
    P = DN_PAIR
    nblk = L // P
    _, causal_bd, strict_bd, diag = chunk_masks(P)
    eye = diag.astype(F32)

    beta_all, gcum_all, glast_all, gcum_t = [], [], [], []
    for b in range(nb):
        misc = misc_ref[b]
        beta_all.append(_sigmoid(misc))
        sp_in = misc + dtb_ref[...]
        softplus = jnp.maximum(sp_in, 0.0) + jnp.log1p(jnp.exp(-jnp.abs(sp_in)))
        g_all = -jnp.exp(alog_ref[...]) * softplus
        g1, g2, g3 = _split3(g_all)
        cum = _dot(cum_mat, g1) + _dot(cum_mat, g2) + _dot(cum_mat, g3)
        gcum_all.append(cum[:L])
        glast_all.append(cum[L:])
        gcum_t.append(cum[:L].T)

    def stage1(b, h):
        a_mats, attns = [], []
        lane = DN_HEADS + h
        gc_col = gcum_all[b][:, lane:lane + 1]
        gl_col = glast_all[b][:, lane:lane + 1]
        gc_row = gcum_t[b][lane:lane + 1, :]
        beta = beta_all[b][:, h:h + 1]
        qn = qkv_ref[b, :, h * D:(h + 1) * D]
        kn = qkv_ref[b, :, DN_WIDTH + h * D:DN_WIDTH + (h + 1) * D]
        v = qkv_ref[b, :, 2 * DN_WIDTH + h * D:2 * DN_WIDTH + (h + 1) * D]
        kb = kn * beta
        e_col = jnp.exp(gc_col)
        kn16 = kn.astype(BF16)
        kb16 = kb.astype(BF16)
        qn16 = qn.astype(BF16)
        for j in range(nblk):
            rows = slice(j * P, (j + 1) * P)
            decay = jnp.exp(jnp.where(causal_bd, gc_col[rows] - gc_row[:, rows], 0.0))
            gram = _dot_nt(jnp.concatenate([kb16[rows], qn16[rows]], axis=0), kn16[rows])
            a_mats.append(jnp.where(strict_bd, gram[:P] * decay, 0.0))
            attns.append(jnp.where(causal_bd, gram[P:] * decay, 0.0).astype(BF16))
        return dict(a=a_mats, attn=attns,
                    rhs=jnp.concatenate([v * beta, kb * e_col], axis=1).astype(BF16),
                    qd=(qn * e_col).astype(BF16),
                    kdt=(kn * jnp.exp(gl_col - gc_col)).T.astype(BF16),
                    gdec=jnp.exp(jnp.broadcast_to(gl_col, (L, D))))

    def stage2(group):
        a_mats = [a for ch in group for a in ch["a"]]
        xps = [(-a).astype(BF16) for a in a_mats]
        t_invs = [eye - a for a in a_mats]
        xps = [_dot(xp, xp).astype(BF16) for xp in xps]
        for _ in range(4):
            prods = [_dot(jnp.concatenate([t.astype(BF16), xp], axis=0), xp) for t, xp in zip(t_invs, xps)]
            t_invs = [t + pr[:P] for t, pr in zip(t_invs, prods)]
            xps = [pr[P:].astype(BF16) for pr in prods]
        t_invs = [t + _dot(t.astype(BF16), xp) for t, xp in zip(t_invs, xps)]
        for i, ch in enumerate(group):
            ch["uw"] = [_dot(t_invs[i * nblk + j].astype(BF16), ch["rhs"][j * P:(j + 1) * P])
                        for j in range(nblk)]

    groups = []
    for b in range(nb):
        if groups:
            stage2(groups[-1])
        groups.append([stage1(b, h) for h in range(DN_HEADS)])
    stage2(groups[-1])
    chs = [ch for group in groups for ch in group]

    states = [state_sc[b, h] for b, h in chains]
    o_parts = [[] for _ in chains]
    zeros_c = jnp.zeros((C, D), BF16)
    for c in range(nchunk):
        r0 = c * C
        j, half = divmod(c, P // C)
        p0 = half * C
        rs = [_dot(jnp.concatenate([ch["uw"][j][p0:p0 + C, D:].astype(BF16), ch["qd"][r0:r0 + C]], axis=0),
                   states[i].astype(BF16)) for i, ch in enumerate(chs)]
        for i, ch in enumerate(chs):
            v_new = (ch["uw"][j][p0:p0 + C, :D] - rs[i][:C]).astype(BF16)
            v_blk = jnp.concatenate([zeros_c] * half + [v_new] + [zeros_c] * (P // C - 1 - half), axis=0)
            lhs = jnp.concatenate([ch["attn"][j][p0:p0 + C, :], ch["kdt"][:, j * P:(j + 1) * P]], axis=0)
            m2 = _dot(lhs, v_blk)
            o_parts[i].append(rs[i][C:] + m2[:C])
            states[i] = states[i] * ch["gdec"][r0:r0 + 1, :] + m2[C:]

    for i, (b, h) in enumerate(chains):
        state_sc[b, h] = states[i]
        o = jnp.concatenate(o_parts[i], axis=0)
        zh = z_ref[b, :, h * D:(h + 1) * D]
        y_ref[b, :, h * D:(h + 1) * D] = (_rms_rows(o, ng_ref[...]) * (zh * _sigmoid(zh))).astype(y_ref.dtype)


def _deltanet(qkv, z, misc, alog_l, dtb_l, norm_gain):
    b, s, _ = qkv.shape
    L = DN_TILE
    return pl.pallas_call(
        _dn_kernel,
        grid=(s // L,),
        in_specs=[pl.BlockSpec((b, L, 3 * DN_WIDTH), lambda j: (0, j, 0)),
                  pl.BlockSpec((b, L, DN_WIDTH), lambda j: (0, j, 0)),
                  pl.BlockSpec((b, L, LANES), lambda j: (0, j, 0)),
                  _const_spec((1, LANES)),
                  _const_spec((1, LANES)),
                  _const_spec((1, DN_HEAD_DIM))],
        out_specs=pl.BlockSpec((b, L, DN_WIDTH), lambda j: (0, j, 0)),
        out_shape=jax.ShapeDtypeStruct((b, s, DN_WIDTH), BF16),
        scratch_shapes=[pltpu.VMEM((b, DN_HEADS, DN_HEAD_DIM, DN_HEAD_DIM), F32)],
        compiler_params=_params("arbitrary"),
        name="deltanet",
    )(qkv, z, misc, alog_l, dtb_l, norm_gain)


def _rope_kernel(pos_ref, freq_ref, cos_ref, sina_ref, sinb_ref):
    tm = pos_ref.shape[1]
    ang = pos_ref[0].astype(F32) * freq_ref[...]
    lane = lax.broadcasted_iota(jnp.int32, (tm, LANES), 1)
    pe1 = (lane >= PE1_LANE) & (lane < PE1_LANE + HALF_ROPE)
    pe2 = (lane >= PE2_LANE) & (lane < PE2_LANE + HALF_ROPE)
    cos = jnp.cos(ang)
    sin = jnp.sin(ang)
    cos_ref[0] = jnp.where(lane < MLA_NOPE, 1.0, jnp.where(pe1 | pe2, cos, 0.0))
    sina_ref[0] = jnp.where(pe2, sin, 0.0)
    sinb_ref[0] = jnp.where(pe1, -sin, 0.0)


def _rope_tables(positions, freq_lanes):
    b, s = positions.shape
    tm = TM_PREP
    spec = pl.BlockSpec((1, tm, LANES), lambda i, j: (i, j, 0))
    return pl.pallas_call(
        _rope_kernel,
        grid=(b, s // tm),
        in_specs=[pl.BlockSpec((1, tm, 1), lambda i, j: (i, j, 0)), _const_spec((1, LANES))],
        out_specs=[spec, spec, spec],
        out_shape=[jax.ShapeDtypeStruct((b, s, LANES), F32)] * 3,
        compiler_params=_params("parallel", "parallel"),
        name="rope_tables",
    )(positions.reshape(b, s, 1), freq_lanes)


def _rope(x, cos, sina, sinb):
    return x * cos + pltpu.roll(x, 2 * HALF_ROPE, 1) * sina + pltpu.roll(x, LANES - 2 * HALF_ROPE, 1) * sinb


def _mla_prep_kernel(cq_ref, ckv_ref, misc_ref, cos_ref, sina_ref, sinb_ref,
                     qag_ref, kvag_ref, wq_ref, wk_ref, wvt_ref, qg_ref, kg_ref, kpg_ref,
                     q_ref, k_ref, vt_ref, qsq_ref, ksq_ref):
    tm = cq_ref.shape[1]
    ones8 = jnp.ones((SUBLANES, LANES), BF16)
    eye = (lax.broadcasted_iota(jnp.int32, (LANES, LANES), 0)
           == lax.broadcasted_iota(jnp.int32, (LANES, LANES), 1)).astype(BF16)

    def row_sq_norms(x16):
        xf = x16.astype(F32)
        return _dot_nt(ones8, (xf * xf).astype(BF16))

    cos = cos_ref[0]
    sina = sina_ref[0]
    sinb = sinb_ref[0]
    lane = lax.broadcasted_iota(jnp.int32, (tm, LANES), 1)
    is_nope = lane < MLA_NOPE

    cqn = _rms_rows(cq_ref[0], qag_ref[...]).astype(BF16)
    ckvn = _rms_rows(ckv_ref[0], kvag_ref[...]).astype(BF16)
    qf = _dot(cqn, wq_ref[...])
    kf = _dot(ckvn, wk_ref[...])
    vt_ref[0] = _dot_nt(wvt_ref[...], ckvn).astype(BF16)

    kp = jnp.where(is_nope, 0.0, misc_ref[0])
    kp_ms = jnp.sum(kp * kp, axis=-1, keepdims=True) * (1.0 / MLA_ROPE)
    kp = _rope(kp * lax.rsqrt(kp_ms + EPS) * kpg_ref[...], cos, sina, sinb)

    scale = MLA_QK_DIM ** -0.5 * math.log2(math.e)
    for h in range(MLA_HEADS):
        xq = qf[:, h * LANES:(h + 1) * LANES]
        sq = xq * xq
        ms_n = jnp.sum(jnp.where(is_nope, sq, 0.0), axis=-1, keepdims=True) * (1.0 / MLA_NOPE)
        ms_p = jnp.sum(jnp.where(is_nope, 0.0, sq), axis=-1, keepdims=True) * (1.0 / MLA_ROPE)
        inv = jnp.where(is_nope, lax.rsqrt(ms_n + EPS), lax.rsqrt(ms_p + EPS))
        qh = _rope(xq * inv * qg_ref[...], cos, sina, sinb) * scale
        qt = _dot_nt(eye, qh.astype(BF16))
        q_ref[0, h] = qt.astype(BF16)
        qsq_ref[0, h] = jnp.broadcast_to(jnp.sum(qt * qt, axis=0, keepdims=True), (SUBLANES, tm))

        xk = kf[:, h * LANES:(h + 1) * LANES]
        ms_k = jnp.sum(xk * xk, axis=-1, keepdims=True) * (1.0 / MLA_NOPE)
        kh = xk * lax.rsqrt(ms_k + EPS) * kg_ref[...] + kp
        k16 = kh.astype(BF16)
        k_ref[0, h] = k16
        ksq_ref[0, h] = row_sq_norms(k16)


def _mla_prep(cq, ckv, misc, cos, sina, sinb, qag, kvag, wq, wk, wvt, qg, kg, kpg):
    b, s, _ = cq.shape
    tm = TM_PREP
    hl = MLA_HEADS * LANES

    def tok(n):
        return pl.BlockSpec((1, tm, n), lambda i, j: (i, j, 0))

    return pl.pallas_call(
        _mla_prep_kernel,
        grid=(b, s // tm),
        in_specs=[tok(Q_LORA), tok(KV_LORA), tok(LANES), tok(LANES), tok(LANES), tok(LANES),
                  _const_spec((1, Q_LORA)), _const_spec((1, KV_LORA)),
                  _const_spec((Q_LORA, hl)), _const_spec((KV_LORA, hl)), _const_spec((MLA_WIDTH, KV_LORA)),
                  _const_spec((1, LANES)), _const_spec((1, LANES)), _const_spec((1, LANES))],
        out_specs=[pl.BlockSpec((1, MLA_HEADS, LANES, tm), lambda i, j: (i, 0, 0, j)),
                   pl.BlockSpec((1, MLA_HEADS, tm, LANES), lambda i, j: (i, 0, j, 0)),
                   pl.BlockSpec((1, MLA_WIDTH, tm), lambda i, j: (i, 0, j)),
                   pl.BlockSpec((1, MLA_HEADS, SUBLANES, tm), lambda i, j: (i, 0, 0, j)),
                   pl.BlockSpec((1, MLA_HEADS, SUBLANES, tm), lambda i, j: (i, 0, 0, j))],
        out_shape=[jax.ShapeDtypeStruct((b, MLA_HEADS, LANES, s), BF16),
                   jax.ShapeDtypeStruct((b, MLA_HEADS, s, LANES), BF16),
                   jax.ShapeDtypeStruct((b, MLA_WIDTH, s), BF16),
                   jax.ShapeDtypeStruct((b, MLA_HEADS, SUBLANES, s), F32),
                   jax.ShapeDtypeStruct((b, MLA_HEADS, SUBLANES, s), F32)],
        compiler_params=_params("parallel", "parallel"),
        name="mla_prep",
    )(cq, ckv, misc, cos, sina, sinb, qag, kvag, wq, wk, wvt, qg, kg, kpg)


def _bounds_kernel(qsq_ref, ksq_ref, u_ref, flag_ref):
    worst = None
    for h in range(MLA_HEADS):
        kmax = jnp.max(ksq_ref[0, h], axis=-1, keepdims=True)
        u = jnp.sqrt(qsq_ref[0, h] * kmax) * SCORE_BOUND_SLACK
        u_ref[0, h] = u
        umax = jnp.max(u, axis=-1, keepdims=True)
        worst = umax if worst is None else jnp.maximum(worst, umax)
    flag_ref[0] = jnp.broadcast_to((worst <= SCORE_BOUND_LIMIT).astype(jnp.int32), flag_ref.shape[1:])


def _score_bounds(qsq, ksq):
    b, h, r, s = qsq.shape
    spec = pl.BlockSpec((1, h, r, s), lambda i: (i, 0, 0, 0))
    return pl.pallas_call(
        _bounds_kernel,
        grid=(b,),
        in_specs=[spec, spec],
        out_specs=[spec, pl.BlockSpec((1, SUBLANES, LANES), lambda i: (i, 0, 0))],
        out_shape=[jax.ShapeDtypeStruct((b, h, r, s), F32), jax.ShapeDtypeStruct((b, SUBLANES, LANES), jnp.int32)],
        compiler_params=_params("parallel"),
        name="score_bounds",
    )(qsq, ksq)


def _attn_block(q_ref, k_ref, vt_ref, u_ref, m_sc, l_sc, acc_sc, visible, bounded):
    scores = [_dot(k_ref[0, 0], q_ref[0, 0])]
    for h in range(MLA_HEADS):
        if h + 1 < MLA_HEADS:
            scores.append(_dot(k_ref[0, h + 1], q_ref[0, h + 1]))
        st = scores[h]
        if visible is not None:
            st = jnp.where(visible, st, -jnp.inf)
        vt = vt_ref[0, h * MLA_V:(h + 1) * MLA_V, :]
        if bounded:
            pt = jnp.exp2(st - u_ref[0, h, 0:1, :])
            l_sc[h] = l_sc[h] + jnp.sum(pt, axis=0, keepdims=True)
            acc_sc[h] = acc_sc[h] + _dot(vt, pt.astype(BF16))
        else:
            m_prev = m_sc[h]
            m_new = jnp.maximum(m_prev, jnp.max(st, axis=0, keepdims=True))
            alpha = jnp.exp2(m_prev - m_new)
            pt = jnp.exp2(st - m_new)
            l_sc[h] = alpha * l_sc[h] + jnp.sum(pt, axis=0, keepdims=True)
            acc_sc[h] = alpha * acc_sc[h] + _dot(vt, pt.astype(BF16))
            m_sc[h] = m_new


def _attn_diag_block(q_ref, k_ref, vt_ref, u_ref, l_sc, acc_sc):
    hk = TK // 2
    r = lax.broadcasted_iota(jnp.int32, (hk, TQ), 0)
    c = lax.broadcasted_iota(jnp.int32, (hk, TQ), 1)
    vis_old = r <= c
    vis_new = vis_old[:, :hk]

    def scores(h):
        return (_dot(k_ref[0, h, 0:hk, :], q_ref[0, h]),
                _dot(k_ref[0, h, hk:, :], q_ref[0, h, :, hk:]))

    nxt = scores(0)
    for h in range(MLA_HEADS):
        s_old, s_new = nxt
        if h + 1 < MLA_HEADS:
            nxt = scores(h + 1)
        u = u_ref[0, h, 0:1, :]
        p_old = jnp.exp2(jnp.where(vis_old, s_old, -jnp.inf) - u)
        p_new = jnp.exp2(jnp.where(vis_new, s_new, -jnp.inf) - u[:, hk:])
        vt = vt_ref[0, h * MLA_V:(h + 1) * MLA_V, :]
        l_sc[h] = l_sc[h] + jnp.sum(p_old, axis=0, keepdims=True)
        acc_sc[h] = acc_sc[h] + _dot(vt[:, :hk], p_old.astype(BF16))
        l_sc[h, :, hk:] = l_sc[h, :, hk:] + jnp.sum(p_new, axis=0, keepdims=True)
        acc_sc[h, :, hk:] = acc_sc[h, :, hk:] + _dot(vt[:, hk:], p_new.astype(BF16))


def _attn_kernel(qi_ref, kj_ref, last_ref, flag_ref, q_ref, k_ref, vt_ref, u_ref, o_ref, m_sc, l_sc, acc_sc):
    p = pl.program_id(1)
    qi = qi_ref[p]
    kj = kj_ref[p]
    bounded = flag_ref[pl.program_id(0)] == 1

    @pl.when(kj == 0)
    def _():
        m_sc[...] = jnp.full_like(m_sc, -jnp.inf)
        l_sc[...] = jnp.zeros_like(l_sc)
        acc_sc[...] = jnp.zeros_like(acc_sc)

    all_visible = kj * TK + (TK - 1) <= qi * TQ

    def run(visible_fn, use_bound):
        def body():
            _attn_block(q_ref, k_ref, vt_ref, u_ref, m_sc, l_sc, acc_sc, visible_fn(), use_bound)
        return body

    def causal_mask():
        kpos = kj * TK + lax.broadcasted_iota(jnp.int32, (TK, TQ), 0)
        qpos = qi * TQ + lax.broadcasted_iota(jnp.int32, (TK, TQ), 1)
        return kpos <= qpos

    partly = jnp.logical_not(all_visible)
    unbounded = jnp.logical_not(bounded)
    pl.when(all_visible & bounded)(run(lambda: None, True))
    if TQ == TK:
        pl.when(partly & bounded)(lambda: _attn_diag_block(q_ref, k_ref, vt_ref, u_ref, l_sc, acc_sc))
    else:
        pl.when(partly & bounded)(run(causal_mask, True))
    pl.when(all_visible & unbounded)(run(lambda: None, False))
    pl.when(partly & unbounded)(run(causal_mask, False))

    @pl.when(last_ref[p] == 1)
    def _():
        out_t = jnp.concatenate([acc_sc[h] / l_sc[h] for h in range(MLA_HEADS)], axis=0)
        o_ref[0] = out_t.T.astype(o_ref.dtype)


def _attention(q, k, vt, u, flag):
    b, _, s, _ = k.shape
    nq = s // TQ
    pairs = [(i, j) for i in range(nq) for j in range((i * TQ + TQ - 1) // TK + 1)]
    qi = jnp.asarray(np.array([p[0] for p in pairs], np.int32))
    kj = jnp.asarray(np.array([p[1] for p in pairs], np.int32))
    last = jnp.asarray(np.array([int(p[1] == (p[0] * TQ + TQ - 1) // TK) for p in pairs], np.int32))
    grid_spec = pltpu.PrefetchScalarGridSpec(
        num_scalar_prefetch=4,
        grid=(b, len(pairs)),
        in_specs=[pl.BlockSpec((1, MLA_HEADS, LANES, TQ), lambda i, p, qi, kj, *_: (i, 0, 0, qi[p])),
                  pl.BlockSpec((1, MLA_HEADS, TK, LANES), lambda i, p, qi, kj, *_: (i, 0, kj[p], 0)),
                  pl.BlockSpec((1, MLA_WIDTH, TK), lambda i, p, qi, kj, *_: (i, 0, kj[p])),
                  pl.BlockSpec((1, MLA_HEADS, SUBLANES, TQ), lambda i, p, qi, kj, *_: (i, 0, 0, qi[p]))],
        out_specs=pl.BlockSpec((1, TQ, MLA_WIDTH), lambda i, p, qi, kj, *_: (i, qi[p], 0)),
        scratch_shapes=[pltpu.VMEM((MLA_HEADS, 1, TQ), F32),
                        pltpu.VMEM((MLA_HEADS, 1, TQ), F32),
                        pltpu.VMEM((MLA_HEADS, MLA_V, TQ), F32)],
    )
    return pl.pallas_call(
        _attn_kernel,
        grid_spec=grid_spec,
        out_shape=jax.ShapeDtypeStruct((b, s, MLA_WIDTH), BF16),
        compiler_params=_params("parallel", "arbitrary"),
        name="mla_attention",
    )(qi, kj, last, flag, q, k, vt, u)


POOL_HALO = 16


def _post_kernel(x_ref, xa_ref, yb_ref, yc_ref, wp_ref, ps_ref, wo_ref, g_ref, wu_ref, wd_ref, o_ref,
                 halo_sc, ext_sc):
    ts = x_ref.shape[1]
    s_idx = pl.program_id(1)

    @pl.when(s_idx == 0)
    def _():
        halo_sc[...] = jnp.zeros_like(halo_sc)

    xa = xa_ref[0]
    ext_sc[0:POOL_HALO, :] = halo_sc[...]
    ext_sc[POOL_HALO:, :] = xa
    halo_sc[...] = xa_ref[0, ts - POOL_HALO:ts, :]

    e = ext_sc[...]
    sums = []
    step = 1
    for _ in POOL_WINDOWS:
        e = e + pltpu.roll(e, step, 0)
        step *= 2
        sums.append(e[POOL_HALO:])
    lane = lax.broadcasted_iota(jnp.int32, (ts, POOL_WIDTH), 1)
    grp = lane >> 6
    win = jnp.left_shift(2, grp)
    t = s_idx * ts + lax.broadcasted_iota(jnp.int32, (ts, POOL_WIDTH), 0)
    count = jnp.minimum(t + 1, win).astype(F32)
    pooled = jnp.where(grp == 0, sums[0], jnp.where(grp == 1, sums[1], jnp.where(grp == 2, sums[2], sums[3])))
    ya = _dot((pooled / count - xa).astype(BF16), wp_ref[...]) * ps_ref[...]

    acc = x_ref[0] + _dot(ya.astype(BF16), wo_ref[0:POOL_WIDTH, :])
    acc = acc + _dot(yb_ref[0].astype(BF16), wo_ref[POOL_WIDTH:POOL_WIDTH + DN_WIDTH, :])
    x1 = acc + _dot(yc_ref[0].astype(BF16), wo_ref[POOL_WIDTH + DN_WIDTH:, :])

    h = _rms_rows(x1, g_ref[...]).astype(BF16)
    u = jnp.maximum(_dot(h, wu_ref[...]), 0.0)
    o_ref[0] = x1 + _dot((u * u).astype(BF16), wd_ref[...])


def _post(x, xa, yb, yc, wp_bd, pool_scale, w_out, gain, w_up, w_down):
    b, s, _ = x.shape
    ts = TM_PROJ

    def tok(n):
        return pl.BlockSpec((1, ts, n), lambda i, j: (i, j, 0))

    return pl.pallas_call(
        _post_kernel,
        grid=(b, s // ts),
        in_specs=[tok(D_MODEL), tok(POOL_WIDTH), tok(DN_WIDTH), tok(MLA_WIDTH),
                  _const_spec((POOL_WIDTH, POOL_WIDTH)), _const_spec((1, POOL_WIDTH)),
                  _const_spec((D_MODEL, D_MODEL)), _const_spec((1, D_MODEL)),
                  _const_spec((D_MODEL, D_FF)), _const_spec((D_FF, D_MODEL))],
        out_specs=tok(D_MODEL),
        out_shape=jax.ShapeDtypeStruct((b, s, D_MODEL), F32),
        scratch_shapes=[pltpu.VMEM((POOL_HALO, POOL_WIDTH), F32),
                        pltpu.VMEM((ts + POOL_HALO, POOL_WIDTH), F32)],
        compiler_params=_params("parallel", "arbitrary"),
        name="outproj_pool_mlp",
    )(x, xa, yb, yc, wp_bd, pool_scale, w_out, gain, w_up, w_down)


def _head_block(nope, pe):
    z16 = jnp.zeros(pe.shape[:-1] + (HALF_ROPE,), pe.dtype)
    return jnp.concatenate([nope, pe[..., :HALF_ROPE], z16, pe[..., HALF_ROPE:], z16], axis=-1)


def _prep_layer(w_in, pool_w, pool_scale, dn_conv, dn_a_log, dn_dt_bias, dn_norm,
                q_a_norm, w_q_b, kv_a_norm, w_kv_b, q_norm, k_norm, w_out, w_up, w_down):
    offs = np.cumsum((0,) + IN_SPLITS)
    seg = [w_in[:, offs[i]:offs[i + 1]] for i in range(len(IN_SPLITS))]
    w_xa, w_qkv, w_z, w_b, w_a, w_cq, w_ckv, w_kpe = seg
    d = w_in.shape[0]
    z64 = jnp.zeros((d, MLA_NOPE), w_in.dtype)
    w_misc = jnp.concatenate([w_b, w_a, jnp.zeros((d, MLA_NOPE - 2 * DN_HEADS), w_in.dtype),
                              _head_block(z64, w_kpe)[:, MLA_NOPE:]], axis=1)
    w_cat = jnp.concatenate([w_qkv, w_xa, w_z, w_cq, w_ckv, w_misc], axis=1).astype(BF16)

    lane_pad = jnp.zeros((LANES - 2 * DN_HEADS,), F32)
    alog_l = jnp.concatenate([jnp.zeros((DN_HEADS,), F32), dn_a_log, lane_pad])[None]
    dtb_l = jnp.concatenate([jnp.zeros((DN_HEADS,), F32), dn_dt_bias, lane_pad])[None]

    wq = w_q_b.reshape(Q_LORA, MLA_HEADS, MLA_QK_DIM)
    wq = _head_block(wq[..., :MLA_NOPE], wq[..., MLA_NOPE:]).reshape(Q_LORA, MLA_HEADS * LANES).astype(BF16)
    wkv = w_kv_b.reshape(KV_LORA, MLA_HEADS, MLA_NOPE + MLA_V)
    wk = _head_block(wkv[..., :MLA_NOPE], jnp.zeros((KV_LORA, MLA_HEADS, MLA_ROPE), F32))
    wk = wk.reshape(KV_LORA, MLA_HEADS * LANES).astype(BF16)
    wvt = wkv[..., MLA_NOPE:].reshape(KV_LORA, MLA_WIDTH).T.astype(BF16)
    qg = _head_block(q_norm[:MLA_NOPE], q_norm[MLA_NOPE:])[None]
    kg = _head_block(k_norm[:MLA_NOPE], jnp.zeros((MLA_ROPE,), F32))[None]
    kpg = _head_block(jnp.zeros((MLA_NOPE,), F32), k_norm[MLA_NOPE:])[None]

    wp_bd = jax.scipy.linalg.block_diag(*[pool_w[g] for g in range(POOL_GROUPS)]).astype(BF16)
    return dict(w_cat=w_cat, alog_l=alog_l, dtb_l=dtb_l, conv_w=dn_conv, dn_norm=dn_norm[None],
                qag=q_a_norm[None], kvag=kv_a_norm[None], wq=wq, wk=wk, wvt=wvt, qg=qg, kg=kg, kpg=kpg,
                wp_bd=wp_bd, pool_scale=pool_scale[None], w_out=w_out.astype(BF16),
                w_up=w_up.astype(BF16), w_down=w_down.astype(BF16))


def kernel(x, positions, attn_norm, w_in, pool_w, pool_scale, dn_conv, dn_a_log, dn_dt_bias, dn_norm,
           mla_q_a_norm, mla_w_q_b, mla_kv_a_norm, mla_w_kv_b, mla_q_norm, mla_k_norm,
           w_out, mlp_norm, w_up, w_down):
    b, s, d = x.shape
    depth = w_in.shape[0]
    inv_freq = ROPE_THETA ** (-jnp.arange(0, MLA_ROPE, 2, dtype=F32) / MLA_ROPE)
    freq_lanes = _head_block(jnp.zeros((MLA_NOPE,), F32), jnp.concatenate([inv_freq, inv_freq]))[None]
    cos, sina, sinb = _rope_tables(positions, freq_lanes)

    for l in range(depth):
        p = _prep_layer(w_in[l], pool_w[l], pool_scale[l], dn_conv[l], dn_a_log[l], dn_dt_bias[l], dn_norm[l],
                        mla_q_a_norm[l], mla_w_q_b[l], mla_kv_a_norm[l], mla_w_kv_b[l], mla_q_norm[l],
                        mla_k_norm[l], w_out[l], w_up[l], w_down[l])
        qkv, xa, z, cq, ckv, misc = _inproj(x, attn_norm[l][None], p["w_cat"], p["conv_w"])
        y_b = _deltanet(qkv, z, misc, p["alog_l"], p["dtb_l"], p["dn_norm"])
        q, k, vt, qsq, ksq = _mla_prep(cq, ckv, misc, cos, sina, sinb, p["qag"], p["kvag"], p["wq"], p["wk"],
                                       p["wvt"], p["qg"], p["kg"], p["kpg"])
        u, flag = _score_bounds(qsq, ksq)
        y_c = _attention(q, k, vt, u, flag[:, 0, 0])
        x = _post(x, xa, y_b, y_c, p["wp_bd"], p["pool_scale"], p["w_out"], mlp_norm[l][None], p["w_up"], p["w_down"])
    return x
```

```python
import math

import jax
import jax.numpy as jnp
import numpy as np
from jax import lax
from jax.experimental import pallas as pl
from jax.experimental.pallas import tpu as pltpu

F32 = jnp.float32
BF16 = jnp.bfloat16

D_MODEL = 1024
POOL_GROUPS = 4
POOL_GROUP_DIM = 64
POOL_WIDTH = POOL_GROUPS * POOL_GROUP_DIM
POOL_WINDOWS = (2, 4, 8, 16)
DN_HEADS = 4
DN_HEAD_DIM = 128
DN_WIDTH = DN_HEADS * DN_HEAD_DIM
DN_CONV = 4
DN_CHUNK = 64
MLA_HEADS = 4
MLA_NOPE = 64
MLA_ROPE = 32
MLA_QK_DIM = MLA_NOPE + MLA_ROPE
MLA_V = 64
MLA_WIDTH = MLA_HEADS * MLA_V
Q_LORA = 256
KV_LORA = 128
ROPE_THETA = 10000.0
D_FF = 4 * D_MODEL
EPS = 1e-6
IN_SPLITS = (POOL_WIDTH, 3 * DN_WIDTH, DN_WIDTH, DN_HEADS, DN_HEADS, Q_LORA, KV_LORA, MLA_ROPE)

LANES = 128
SUBLANES = 8
MXU_COLS = 256
VMEM_LIMIT_BYTES = 56 * 1024 * 1024

TM_PROJ = 512
TM_PREP = 1024
DN_TILE = 256
DN_PAIR = 2 * DN_CHUNK
TQ = 1024
TK = 1024
HALF_ROPE = MLA_ROPE // 2
SCORE_BOUND_SLACK = 1.02
SCORE_BOUND_LIMIT = 30.0
PE1_LANE = MLA_NOPE
PE2_LANE = MLA_NOPE + 2 * HALF_ROPE


def _dot(a, b):
    return jnp.dot(a, b, preferred_element_type=F32)


def _dot_nt(a, b):
    return lax.dot_general(a, b, (((1,), (1,)), ((), ())), preferred_element_type=F32)


def _rms_rows(x, gain):
    return x * lax.rsqrt(jnp.mean(x * x, axis=-1, keepdims=True) + EPS) * gain


def _split3(x):
    x1 = x.astype(BF16)
    r1 = x - x1.astype(F32)
    x2 = r1.astype(BF16)
    r2 = r1 - x2.astype(F32)
    return x1, x2, r2.astype(BF16)


def _sigmoid(x):
    return 1.0 / (1.0 + jnp.exp(-x))


def _params(*sem):
    return pltpu.CompilerParams(dimension_semantics=sem, vmem_limit_bytes=VMEM_LIMIT_BYTES)


def _const_spec(shape):
    nd = len(shape)
    return pl.BlockSpec(shape, lambda *_: (0,) * nd, pipeline_mode=pl.Buffered(1))


IN_SEGS = (3 * DN_WIDTH, POOL_WIDTH, DN_WIDTH, Q_LORA, KV_LORA, LANES)
CONV_HALO = SUBLANES


def _inproj_kernel(x_ref, g_ref, w_ref, cw_ref, qkv_ref, *rest):
    o_refs, (halo_sc, ext_sc) = rest[:-2], rest[-2:]
    tm = x_ref.shape[1]
    nqkv = 3 * DN_WIDTH
    D = DN_HEAD_DIM

    @pl.when(pl.program_id(1) == 0)
    def _():
        halo_sc[...] = jnp.zeros_like(halo_sc)

    h = _rms_rows(x_ref[0], g_ref[...]).astype(BF16)
    ext_sc[0:CONV_HALO, :] = halo_sc[...]

    def conv_cols(c0):
        e = ext_sc[:, c0:c0 + D]
        cw = cw_ref[:, c0:c0 + D]
        e1 = pltpu.roll(e, 1, 0)
        near = cw[3:4] * e + cw[2:3] * e1
        far = cw[1:2] * e + cw[0:1] * e1
        acc = (near + pltpu.roll(far, 2, 0))[CONV_HALO:]
        y = acc * _sigmoid(acc)
        if c0 < 2 * DN_WIDTH:
            y = y * lax.rsqrt(jnp.sum(y * y, axis=-1, keepdims=True) + EPS)
            if c0 < DN_WIDTH:
                y = y * (D ** -0.5)
        qkv_ref[0, :, c0:c0 + D] = y

    pieces = []
    off = nqkv
    for o_ref in o_refs:
        n = o_ref.shape[-1]
        pieces += [(o_ref, c, off + c, LANES) for c in range(0, n, LANES)]
        off += n
    per_dot = MXU_COLS // LANES
    others = [pieces[i:i + per_dot] for i in range(0, len(pieces), per_dot)]
    def qkv_cols(c0):
        ext_sc[CONV_HALO:, c0:c0 + MXU_COLS] = _dot(h, w_ref[:, c0:c0 + MXU_COLS])

    nsteps = nqkv // MXU_COLS
    qkv_cols(0)
    for i in range(nsteps):
        if i + 1 < nsteps:
            qkv_cols((i + 1) * MXU_COLS)
        for c in range(i * MXU_COLS, (i + 1) * MXU_COLS, D):
            conv_cols(c)
        for group in others[i::nsteps]:
            w0 = group[0][2]
            res = _dot(h, w_ref[:, w0:w0 + LANES * len(group)])
            for o_ref, c, woff, n in group:
                o_ref[0, :, c:c + n] = res[:, woff - w0:woff - w0 + n]
    halo_sc[...] = ext_sc[tm:tm + CONV_HALO, :]


def _inproj(x, gain, w_cat, conv_w):
    b, s, _ = x.shape
    tm = TM_PROJ
    n_all = sum(IN_SEGS)
    return pl.pallas_call(
        _inproj_kernel,
        grid=(b, s // tm),
        in_specs=[pl.BlockSpec((1, tm, D_MODEL), lambda i, j: (i, j, 0)),
                  _const_spec((1, D_MODEL)),
                  _const_spec((D_MODEL, n_all)),
                  _const_spec((DN_CONV, 3 * DN_WIDTH))],
        out_specs=[pl.BlockSpec((1, tm, n), lambda i, j: (i, j, 0)) for n in IN_SEGS],
        out_shape=[jax.ShapeDtypeStruct((b, s, n), F32) for n in IN_SEGS],
        scratch_shapes=[pltpu.VMEM((CONV_HALO, 3 * DN_WIDTH), F32),
                        pltpu.VMEM((tm + CONV_HALO, 3 * DN_WIDTH), F32)],
        compiler_params=_params("parallel", "arbitrary"),
        name="inproj",
    )(x, gain, w_cat, conv_w)


def _dn_kernel(qkv_ref, z_ref, misc_ref, alog_ref, dtb_ref, ng_ref, y_ref, state_sc):
    nb = qkv_ref.shape[0]
    L = DN_TILE
    C = DN_CHUNK
    nchunk = L // C
    D = DN_HEAD_DIM
    s_idx = pl.program_id(0)
    chains = [(b, h) for b in range(nb) for h in range(DN_HEADS)]

    @pl.when(s_idx == 0)
    def _():
        state_sc[...] = jnp.zeros_like(state_sc)

    def chunk_masks(n):
        ri = lax.broadcasted_iota(jnp.int32, (n, n), 0)
        ci = lax.broadcasted_iota(jnp.int32, (n, n), 1)
        shift = DN_CHUNK.bit_length() - 1
        same = (ri >> shift) == (ci >> shift)
        return same, same & (ci <= ri), same & (ci < ri), ri == ci

    same_l, causal_l, _, _ = chunk_masks(L)
    cum_mat = jnp.concatenate([causal_l.astype(BF16), same_l.astype(BF16)], axis=0)
    P = DN_PAIR
    nblk = L // P
    _, causal_bd, strict_bd, diag = chunk_masks(P)
    eye = diag.astype(F32)

    beta_all, gcum_all, glast_all, gcum_t = [], [], [], []
    for b in range(nb):
        misc = misc_ref[b]
        beta_all.append(_sigmoid(misc))
        sp_in = misc + dtb_ref[...]
        softplus = jnp.maximum(sp_in, 0.0) + jnp.log1p(jnp.exp(-jnp.abs(sp_in)))
        g_all = -jnp.exp(alog_ref[...]) * softplus
        g1, g2, g3 = _split3(g_all)
        cum = _dot(cum_mat, g1) + _dot(cum_mat, g2) + _dot(cum_mat, g3)
        gcum_all.append(cum[:L])
        glast_all.append(cum[L:])
        gcum_t.append(cum[:L].T)

    def stage1(b, h):
        a_mats, attns = [], []
        lane = DN_HEADS + h
        gc_col = gcum_all[b][:, lane:lane + 1]
        gl_col = glast_all[b][:, lane:lane + 1]
        gc_row = gcum_t[b][lane:lane + 1, :]
        beta = beta_all[b][:, h:h + 1]
        qn = qkv_ref[b, :, h * D:(h + 1) * D]
        kn = qkv_ref[b, :, DN_WIDTH + h * D:DN_WIDTH + (h + 1) * D]
        v = qkv_ref[b, :, 2 * DN_WIDTH + h * D:2 * DN_WIDTH + (h + 1) * D]
        kb = kn * beta
        e_col = jnp.exp(gc_col)
        kn16 = kn.astype(BF16)
        kb16 = kb.astype(BF16)
        qn16 = qn.astype(BF16)
        for j in range(nblk):
            rows = slice(j * P, (j + 1) * P)
            decay = jnp.exp(jnp.where(causal_bd, gc_col[rows] - gc_row[:, rows], 0.0))
            gram = _dot_nt(jnp.concatenate([kb16[rows], qn16[rows]], axis=0), kn16[rows])
            a_mats.append(jnp.where(strict_bd, gram[:P] * decay, 0.0))
            attns.append(jnp.where(causal_bd, gram[P:] * decay, 0.0).astype(BF16))
        return dict(a=a_mats, attn=attns,
                    rhs=jnp.concatenate([v * beta, kb * e_col], axis=1).astype(BF16),
                    qd=(qn * e_col).astype(BF16),
                    kdt=(kn * jnp.exp(gl_col - gc_col)).T.astype(BF16),
                    gdec=jnp.exp(jnp.broadcast_to(gl_col, (L, D))))

    def stage2(group):
        a_mats = [a for ch in group for a in ch["a"]]
        xps = [(-a).astype(BF16) for a in a_mats]
        t_invs = [eye - a for a in a_mats]
        xps = [_dot(xp, xp).astype(BF16) for xp in xps]
        for _ in range(4):
            prods = [_dot(jnp.concatenate([t.astype(BF16), xp], axis=0), xp) for t, xp in zip(t_invs, xps)]
            t_invs = [t + pr[:P] for t, pr in zip(t_invs, prods)]
            xps = [pr[P:].astype(BF16) for pr in prods]
        t_invs = [t + _dot(t.astype(BF16), xp) for t, xp in zip(t_invs, xps)]
        for i, ch in enumerate(group):
            ch["uw"] = [_dot(t_invs[i * nblk + j].astype(BF16), ch["rhs"][j * P:(j + 1) * P])
                        for j in range(nblk)]

    chs = [stage1(b, h) for b, h in chains]
    stage2(chs)

    states = [state_sc[b, h] for b, h in chains]
    o_parts = [[] for _ in chains]
    zeros_c = jnp.zeros((C, D), BF16)
    for c in range(nchunk):
        r0 = c * C
        j, half = divmod(c, P // C)
        p0 = half * C
        rs = [_dot(jnp.concatenate([ch["uw"][j][p0:p0 + C, D:].astype(BF16), ch["qd"][r0:r0 + C]], axis=0),
                   states[i].astype(BF16)) for i, ch in enumerate(chs)]
        for i, ch in enumerate(chs):
            v_new = (ch["uw"][j][p0:p0 + C, :D] - rs[i][:C]).astype(BF16)
            v_blk = jnp.concatenate([zeros_c] * half + [v_new] + [zeros_c] * (P // C - 1 - half), axis=0)
            lhs = jnp.concatenate([ch["attn"][j][p0:p0 + C, :], ch["kdt"][:, j * P:(j + 1) * P]], axis=0)
            m2 = _dot(lhs, v_blk)
            o_parts[i].append(rs[i][C:] + m2[:C])
            states[i] = states[i] * ch["gdec"][r0:r0 + 1, :] + m2[C:]

    for i, (b, h) in enumerate(chains):
        state_sc[b, h] = states[i]
        o = jnp.concatenate(o_parts[i], axis=0)
        zh = z_ref[b, :, h * D:(h + 1) * D]
        y_ref[b, :, h * D:(h + 1) * D] = (_rms_rows(o, ng_ref[...]) * (zh * _sigmoid(zh))).astype(y_ref.dtype)


def _deltanet(qkv, z, misc, alog_l, dtb_l, norm_gain):
    b, s, _ = qkv.shape
    L = DN_TILE
    return pl.pallas_call(
        _dn_kernel,
        grid=(s // L,),
        in_specs=[pl.BlockSpec((b, L, 3 * DN_WIDTH), lambda j: (0, j, 0)),
                  pl.BlockSpec((b, L, DN_WIDTH), lambda j: (0, j, 0)),
                  pl.BlockSpec((b, L, LANES), lambda j: (0, j, 0)),
                  _const_spec((1, LANES)),
                  _const_spec((1, LANES)),
                  _const_spec((1, DN_HEAD_DIM))],
        out_specs=pl.BlockSpec((b, L, DN_WIDTH), lambda j: (0, j, 0)),
        out_shape=jax.ShapeDtypeStruct((b, s, DN_WIDTH), BF16),
        scratch_shapes=[pltpu.VMEM((b, DN_HEADS, DN_HEAD_DIM, DN_HEAD_DIM), F32)],
        compiler_params=_params("arbitrary"),
        name="deltanet",
    )(qkv, z, misc, alog_l, dtb_l, norm_gain)


def _rope_kernel(pos_ref, freq_ref, cos_ref, sina_ref, sinb_ref):
    tm = pos_ref.shape[1]
    ang = pos_ref[0].astype(F32) * freq_ref[...]
    lane = lax.broadcasted_iota(jnp.int32, (tm, LANES), 1)
    pe1 = (lane >= PE1_LANE) & (lane < PE1_LANE + HALF_ROPE)
    pe2 = (lane >= PE2_LANE) & (lane < PE2_LANE + HALF_ROPE)
    cos = jnp.cos(ang)
    sin = jnp.sin(ang)
    cos_ref[0] = jnp.where(lane < MLA_NOPE, 1.0, jnp.where(pe1 | pe2, cos, 0.0))
    sina_ref[0] = jnp.where(pe2, sin, 0.0)
    sinb_ref[0] = jnp.where(pe1, -sin, 0.0)


def _rope_tables(positions, freq_lanes):
    b, s = positions.shape
    tm = TM_PREP
    spec = pl.BlockSpec((1, tm, LANES), lambda i, j: (i, j, 0))
    return pl.pallas_call(
        _rope_kernel,
        grid=(b, s // tm),
        in_specs=[pl.BlockSpec((1, tm, 1), lambda i, j: (i, j, 0)), _const_spec((1, LANES))],
        out_specs=[spec, spec, spec],
        out_shape=[jax.ShapeDtypeStruct((b, s, LANES), F32)] * 3,
        compiler_params=_params("parallel", "parallel"),
        name="rope_tables",
    )(positions.reshape(b, s, 1), freq_lanes)


def _rope(x, cos, sina, sinb):
    return x * cos + pltpu.roll(x, 2 * HALF_ROPE, 1) * sina + pltpu.roll(x, LANES - 2 * HALF_ROPE, 1) * sinb


def _mla_prep_kernel(cq_ref, ckv_ref, misc_ref, cos_ref, sina_ref, sinb_ref,
                     qag_ref, kvag_ref, wq_ref, wk_ref, wvt_ref, qg_ref, kg_ref, kpg_ref,
                     q_ref, k_ref, vt_ref, qsq_ref, ksq_ref):
    tm = cq_ref.shape[1]
    ones8 = jnp.ones((SUBLANES, LANES), BF16)
    eye = (lax.broadcasted_iota(jnp.int32, (LANES, LANES), 0)
           == lax.broadcasted_iota(jnp.int32, (LANES, LANES), 1)).astype(BF16)

    def row_sq_norms(x16):
        xf = x16.astype(F32)
        return _dot_nt(ones8, (xf * xf).astype(BF16))

    cos = cos_ref[0]
    sina = sina_ref[0]
    sinb = sinb_ref[0]
    lane = lax.broadcasted_iota(jnp.int32, (tm, LANES), 1)
    is_nope = lane < MLA_NOPE

    cqn = _rms_rows(cq_ref[0], qag_ref[...]).astype(BF16)
    ckvn = _rms_rows(ckv_ref[0], kvag_ref[...]).astype(BF16)
    qf = _dot(cqn, wq_ref[...])
    kf = _dot(ckvn, wk_ref[...])
    vt_ref[0] = _dot_nt(wvt_ref[...], ckvn).astype(BF16)

    kp = jnp.where(is_nope, 0.0, misc_ref[0])
    kp_ms = jnp.sum(kp * kp, axis=-1, keepdims=True) * (1.0 / MLA_ROPE)
    kp = _rope(kp * lax.rsqrt(kp_ms + EPS) * kpg_ref[...], cos, sina, sinb)

    scale = MLA_QK_DIM ** -0.5 * math.log2(math.e)
    for h in range(MLA_HEADS):
        xq = qf[:, h * LANES:(h + 1) * LANES]
        sq = xq * xq
        ms_n = jnp.sum(jnp.where(is_nope, sq, 0.0), axis=-1, keepdims=True) * (1.0 / MLA_NOPE)
        ms_p = jnp.sum(jnp.where(is_nope, 0.0, sq), axis=-1, keepdims=True) * (1.0 / MLA_ROPE)
        inv = jnp.where(is_nope, lax.rsqrt(ms_n + EPS), lax.rsqrt(ms_p + EPS))
        qh = _rope(xq * inv * qg_ref[...], cos, sina, sinb) * scale
        qt = _dot_nt(eye, qh.astype(BF16))
        q_ref[0, h] = qt.astype(BF16)
        qsq_ref[0, h] = jnp.broadcast_to(jnp.sum(qt * qt, axis=0, keepdims=True), (SUBLANES, tm))

        xk = kf[:, h * LANES:(h + 1) * LANES]
        ms_k = jnp.sum(xk * xk, axis=-1, keepdims=True) * (1.0 / MLA_NOPE)
        kh = xk * lax.rsqrt(ms_k + EPS) * kg_ref[...] + kp
        k16 = kh.astype(BF16)
        k_ref[0, h] = k16
        ksq_ref[0, h] = row_sq_norms(k16)


def _mla_prep(cq, ckv, misc, cos, sina, sinb, qag, kvag, wq, wk, wvt, qg, kg, kpg):
    b, s, _ = cq.shape
    tm = TM_PREP
    hl = MLA_HEADS * LANES

    def tok(n):
        return pl.BlockSpec((1, tm, n), lambda i, j: (i, j, 0))

    return pl.pallas_call(
        _mla_prep_kernel,
        grid=(b, s // tm),
        in_specs=[tok(Q_LORA), tok(KV_LORA), tok(LANES), tok(LANES), tok(LANES), tok(LANES),
                  _const_spec((1, Q_LORA)), _const_spec((1, KV_LORA)),
                  _const_spec((Q_LORA, hl)), _const_spec((KV_LORA, hl)), _const_spec((MLA_WIDTH, KV_LORA)),
                  _const_spec((1, LANES)), _const_spec((1, LANES)), _const_spec((1, LANES))],
        out_specs=[pl.BlockSpec((1, MLA_HEADS, LANES, tm), lambda i, j: (i, 0, 0, j)),
                   pl.BlockSpec((1, MLA_HEADS, tm, LANES), lambda i, j: (i, 0, j, 0)),
                   pl.BlockSpec((1, MLA_WIDTH, tm), lambda i, j: (i, 0, j)),
                   pl.BlockSpec((1, MLA_HEADS, SUBLANES, tm), lambda i, j: (i, 0, 0, j)),
                   pl.BlockSpec((1, MLA_HEADS, SUBLANES, tm), lambda i, j: (i, 0, 0, j))],
        out_shape=[jax.ShapeDtypeStruct((b, MLA_HEADS, LANES, s), BF16),
                   jax.ShapeDtypeStruct((b, MLA_HEADS, s, LANES), BF16),
                   jax.ShapeDtypeStruct((b, MLA_WIDTH, s), BF16),
                   jax.ShapeDtypeStruct((b, MLA_HEADS, SUBLANES, s), F32),
                   jax.ShapeDtypeStruct((b, MLA_HEADS, SUBLANES, s), F32)],
        compiler_params=_params("parallel", "parallel"),
        name="mla_prep",
    )(cq, ckv, misc, cos, sina, sinb, qag, kvag, wq, wk, wvt, qg, kg, kpg)


def _bounds_kernel(qsq_ref, ksq_ref, u_ref, flag_ref):
    worst = None
    for h in range(MLA_HEADS):
        kmax = jnp.max(ksq_ref[0, h], axis=-1, keepdims=True)
        u = jnp.sqrt(qsq_ref[0, h] * kmax) * SCORE_BOUND_SLACK
        u_ref[0, h] = u
        umax = jnp.max(u, axis=-1, keepdims=True)
        worst = umax if worst is None else jnp.maximum(worst, umax)
    flag_ref[0] = jnp.broadcast_to((worst <= SCORE_BOUND_LIMIT).astype(jnp.int32), flag_ref.shape[1:])


def _score_bounds(qsq, ksq):
    b, h, r, s = qsq.shape
    spec = pl.BlockSpec((1, h, r, s), lambda i: (i, 0, 0, 0))
    return pl.pallas_call(
        _bounds_kernel,
        grid=(b,),
        in_specs=[spec, spec],
        out_specs=[spec, pl.BlockSpec((1, SUBLANES, LANES), lambda i: (i, 0, 0))],
        out_shape=[jax.ShapeDtypeStruct((b, h, r, s), F32), jax.ShapeDtypeStruct((b, SUBLANES, LANES), jnp.int32)],
        compiler_params=_params("parallel"),
        name="score_bounds",
    )(qsq, ksq)


def _attn_block(q_ref, k_ref, vt_ref, u_ref, m_sc, l_sc, acc_sc, visible, bounded):
    scores = [_dot(k_ref[0, 0], q_ref[0, 0])]
    for h in range(MLA_HEADS):
        if h + 1 < MLA_HEADS:
            scores.append(_dot(k_ref[0, h + 1], q_ref[0, h + 1]))
        st = scores[h]
        if visible is not None:
            st = jnp.where(visible, st, -jnp.inf)
        vt = vt_ref[0, h * MLA_V:(h + 1) * MLA_V, :]
        if bounded:
            pt = jnp.exp2(st - u_ref[0, h, 0:1, :])
            l_sc[h] = l_sc[h] + jnp.sum(pt, axis=0, keepdims=True)
            acc_sc[h] = acc_sc[h] + _dot(vt, pt.astype(BF16))
        else:
            m_prev = m_sc[h]
            m_new = jnp.maximum(m_prev, jnp.max(st, axis=0, keepdims=True))
            alpha = jnp.exp2(m_prev - m_new)
            pt = jnp.exp2(st - m_new)
            l_sc[h] = alpha * l_sc[h] + jnp.sum(pt, axis=0, keepdims=True)
            acc_sc[h] = alpha * acc_sc[h] + _dot(vt, pt.astype(BF16))
            m_sc[h] = m_new


def _attn_diag_block(q_ref, k_ref, vt_ref, u_ref, l_sc, acc_sc):
    hk = TK // 2
    r = lax.broadcasted_iota(jnp.int32, (hk, TQ), 0)
    c = lax.broadcasted_iota(jnp.int32, (hk, TQ), 1)
    vis_old = r <= c
    vis_new = vis_old[:, :hk]

    def scores(h):
        return (_dot(k_ref[0, h, 0:hk, :], q_ref[0, h]),
                _dot(k_ref[0, h, hk:, :], q_ref[0, h, :, hk:]))

    nxt = scores(0)
    for h in range(MLA_HEADS):
        s_old, s_new = nxt
        if h + 1 < MLA_HEADS:
            nxt = scores(h + 1)
        u = u_ref[0, h, 0:1, :]
        p_old = jnp.exp2(jnp.where(vis_old, s_old, -jnp.inf) - u)
        p_new = jnp.exp2(jnp.where(vis_new, s_new, -jnp.inf) - u[:, hk:])
        vt = vt_ref[0, h * MLA_V:(h + 1) * MLA_V, :]
        l_sc[h] = l_sc[h] + jnp.sum(p_old, axis=0, keepdims=True)
        acc_sc[h] = acc_sc[h] + _dot(vt[:, :hk], p_old.astype(BF16))
        l_sc[h, :, hk:] = l_sc[h, :, hk:] + jnp.sum(p_new, axis=0, keepdims=True)
        acc_sc[h, :, hk:] = acc_sc[h, :, hk:] + _dot(vt[:, hk:], p_new.astype(BF16))


def _attn_kernel(qi_ref, kj_ref, last_ref, flag_ref, q_ref, k_ref, vt_ref, u_ref, o_ref, m_sc, l_sc, acc_sc):
    p = pl.program_id(1)
    qi = qi_ref[p]
    kj = kj_ref[p]
    bounded = flag_ref[pl.program_id(0)] == 1

    @pl.when(kj == 0)
    def _():
        m_sc[...] = jnp.full_like(m_sc, -jnp.inf)
        l_sc[...] = jnp.zeros_like(l_sc)
        acc_sc[...] = jnp.zeros_like(acc_sc)

    all_visible = kj * TK + (TK - 1) <= qi * TQ

    def run(visible_fn, use_bound):
        def body():
            _attn_block(q_ref, k_ref, vt_ref, u_ref, m_sc, l_sc, acc_sc, visible_fn(), use_bound)
        return body

    def causal_mask():
        kpos = kj * TK + lax.broadcasted_iota(jnp.int32, (TK, TQ), 0)
        qpos = qi * TQ + lax.broadcasted_iota(jnp.int32, (TK, TQ), 1)
        return kpos <= qpos

    partly = jnp.logical_not(all_visible)
    unbounded = jnp.logical_not(bounded)
    pl.when(all_visible & bounded)(run(lambda: None, True))
    if TQ == TK:
        pl.when(partly & bounded)(lambda: _attn_diag_block(q_ref, k_ref, vt_ref, u_ref, l_sc, acc_sc))
    else:
        pl.when(partly & bounded)(run(causal_mask, True))
    pl.when(all_visible & unbounded)(run(lambda: None, False))
    pl.when(partly & unbounded)(run(causal_mask, False))

    @pl.when(last_ref[p] == 1)
    def _():
        out_t = jnp.concatenate([acc_sc[h] / l_sc[h] for h in range(MLA_HEADS)], axis=0)
        o_ref[0] = out_t.T.astype(o_ref.dtype)


def _attention(q, k, vt, u, flag):
    b, _, s, _ = k.shape
    nq = s // TQ
    pairs = [(i, j) for i in range(nq) for j in range((i * TQ + TQ - 1) // TK + 1)]
    qi = jnp.asarray(np.array([p[0] for p in pairs], np.int32))
    kj = jnp.asarray(np.array([p[1] for p in pairs], np.int32))
    last = jnp.asarray(np.array([int(p[1] == (p[0] * TQ + TQ - 1) // TK) for p in pairs], np.int32))
    grid_spec = pltpu.PrefetchScalarGridSpec(
        num_scalar_prefetch=4,
        grid=(b, len(pairs)),
        in_specs=[pl.BlockSpec((1, MLA_HEADS, LANES, TQ), lambda i, p, qi, kj, *_: (i, 0, 0, qi[p])),
                  pl.BlockSpec((1, MLA_HEADS, TK, LANES), lambda i, p, qi, kj, *_: (i, 0, kj[p], 0)),
                  pl.BlockSpec((1, MLA_WIDTH, TK), lambda i, p, qi, kj, *_: (i, 0, kj[p])),
                  pl.BlockSpec((1, MLA_HEADS, SUBLANES, TQ), lambda i, p, qi, kj, *_: (i, 0, 0, qi[p]))],
        out_specs=pl.BlockSpec((1, TQ, MLA_WIDTH), lambda i, p, qi, kj, *_: (i, qi[p], 0)),
        scratch_shapes=[pltpu.VMEM((MLA_HEADS, 1, TQ), F32),
                        pltpu.VMEM((MLA_HEADS, 1, TQ), F32),
                        pltpu.VMEM((MLA_HEADS, MLA_V, TQ), F32)],
    )
    return pl.pallas_call(
        _attn_kernel,
        grid_spec=grid_spec,
        out_shape=jax.ShapeDtypeStruct((b, s, MLA_WIDTH), BF16),
        compiler_params=_params("parallel", "arbitrary"),
        name="mla_attention",
    )(qi, kj, last, flag, q, k, vt, u)


POOL_HALO = 16


def _post_kernel(x_ref, xa_ref, yb_ref, yc_ref, wp_ref, ps_ref, wo_ref, g_ref, wu_ref, wd_ref, o_ref,
                 halo_sc, ext_sc):
    ts = x_ref.shape[1]
    s_idx = pl.program_id(1)

    @pl.when(s_idx == 0)
    def _():
        halo_sc[...] = jnp.zeros_like(halo_sc)

    xa = xa_ref[0]
    ext_sc[0:POOL_HALO, :] = halo_sc[...]
    ext_sc[POOL_HALO:, :] = xa
    halo_sc[...] = xa_ref[0, ts - POOL_HALO:ts, :]

    assert POOL_WINDOWS == tuple(2 << g for g in range(POOL_GROUPS)) and POOL_WINDOWS[-1] <= POOL_HALO
    e = ext_sc[...]
    sums = []
    for g in range(POOL_GROUPS):
        e = e + pltpu.roll(e, 1 << g, 0)
        sums.append(e[POOL_HALO:])
    lane = lax.broadcasted_iota(jnp.int32, (ts, POOL_WIDTH), 1)
    grp = lane >> (POOL_GROUP_DIM.bit_length() - 1)
    win = jnp.left_shift(2, grp)
    t = s_idx * ts + lax.broadcasted_iota(jnp.int32, (ts, POOL_WIDTH), 0)
    count = jnp.minimum(t + 1, win).astype(F32)
    pooled = sums[POOL_GROUPS - 1]
    for g in range(POOL_GROUPS - 2, -1, -1):
        pooled = jnp.where(grp == g, sums[g], pooled)
    ya = _dot((pooled / count - xa).astype(BF16), wp_ref[...]) * ps_ref[...]

    acc = x_ref[0] + _dot(ya.astype(BF16), wo_ref[0:POOL_WIDTH, :])
    acc = acc + _dot(yb_ref[0].astype(BF16), wo_ref[POOL_WIDTH:POOL_WIDTH + DN_WIDTH, :])
    x1 = acc + _dot(yc_ref[0].astype(BF16), wo_ref[POOL_WIDTH + DN_WIDTH:, :])

    h = _rms_rows(x1, g_ref[...]).astype(BF16)
    u = jnp.maximum(_dot(h, wu_ref[...]), 0.0)
    o_ref[0] = x1 + _dot((u * u).astype(BF16), wd_ref[...])


def _post(x, xa, yb, yc, wp_bd, pool_scale, w_out, gain, w_up, w_down):
    b, s, _ = x.shape
    ts = TM_PROJ

    def tok(n):
        return pl.BlockSpec((1, ts, n), lambda i, j: (i, j, 0))

    return pl.pallas_call(
        _post_kernel,
        grid=(b, s // ts),
        in_specs=[tok(D_MODEL), tok(POOL_WIDTH), tok(DN_WIDTH), tok(MLA_WIDTH),
                  _const_spec((POOL_WIDTH, POOL_WIDTH)), _const_spec((1, POOL_WIDTH)),
                  _const_spec((D_MODEL, D_MODEL)), _const_spec((1, D_MODEL)),
                  _const_spec((D_MODEL, D_FF)), _const_spec((D_FF, D_MODEL))],
        out_specs=tok(D_MODEL),
        out_shape=jax.ShapeDtypeStruct((b, s, D_MODEL), F32),
        scratch_shapes=[pltpu.VMEM((POOL_HALO, POOL_WIDTH), F32),
                        pltpu.VMEM((ts + POOL_HALO, POOL_WIDTH), F32)],
        compiler_params=_params("parallel", "arbitrary"),
        name="outproj_pool_mlp",
    )(x, xa, yb, yc, wp_bd, pool_scale, w_out, gain, w_up, w_down)


def _head_block(nope, pe):
    z16 = jnp.zeros(pe.shape[:-1] + (HALF_ROPE,), pe.dtype)
    return jnp.concatenate([nope, pe[..., :HALF_ROPE], z16, pe[..., HALF_ROPE:], z16], axis=-1)


def _prep_layer(w_in, pool_w, pool_scale, dn_conv, dn_a_log, dn_dt_bias, dn_norm,
                q_a_norm, w_q_b, kv_a_norm, w_kv_b, q_norm, k_norm, w_out, w_up, w_down):
    offs = np.cumsum((0,) + IN_SPLITS)
    seg = [w_in[:, offs[i]:offs[i + 1]] for i in range(len(IN_SPLITS))]
    w_xa, w_qkv, w_z, w_b, w_a, w_cq, w_ckv, w_kpe = seg
    d = w_in.shape[0]
    z64 = jnp.zeros((d, MLA_NOPE), w_in.dtype)
    w_misc = jnp.concatenate([w_b, w_a, jnp.zeros((d, MLA_NOPE - 2 * DN_HEADS), w_in.dtype),
                              _head_block(z64, w_kpe)[:, MLA_NOPE:]], axis=1)
    w_cat = jnp.concatenate([w_qkv, w_xa, w_z, w_cq, w_ckv, w_misc], axis=1).astype(BF16)

    lane_pad = jnp.zeros((LANES - 2 * DN_HEADS,), F32)
    alog_l = jnp.concatenate([jnp.zeros((DN_HEADS,), F32), dn_a_log, lane_pad])[None]
    dtb_l = jnp.concatenate([jnp.zeros((DN_HEADS,), F32), dn_dt_bias, lane_pad])[None]

    wq = w_q_b.reshape(Q_LORA, MLA_HEADS, MLA_QK_DIM)
    wq = _head_block(wq[..., :MLA_NOPE], wq[..., MLA_NOPE:]).reshape(Q_LORA, MLA_HEADS * LANES).astype(BF16)
    wkv = w_kv_b.reshape(KV_LORA, MLA_HEADS, MLA_NOPE + MLA_V)
    wk = _head_block(wkv[..., :MLA_NOPE], jnp.zeros((KV_LORA, MLA_HEADS, MLA_ROPE), F32))
    wk = wk.reshape(KV_LORA, MLA_HEADS * LANES).astype(BF16)
    wvt = wkv[..., MLA_NOPE:].reshape(KV_LORA, MLA_WIDTH).T.astype(BF16)
    qg = _head_block(q_norm[:MLA_NOPE], q_norm[MLA_NOPE:])[None]
    kg = _head_block(k_norm[:MLA_NOPE], jnp.zeros((MLA_ROPE,), F32))[None]
    kpg = _head_block(jnp.zeros((MLA_NOPE,), F32), k_norm[MLA_NOPE:])[None]

    wp_bd = jax.scipy.linalg.block_diag(*[pool_w[g] for g in range(POOL_GROUPS)]).astype(BF16)
    return dict(w_cat=w_cat, alog_l=alog_l, dtb_l=dtb_l, conv_w=dn_conv, dn_norm=dn_norm[None],
                qag=q_a_norm[None], kvag=kv_a_norm[None], wq=wq, wk=wk, wvt=wvt, qg=qg, kg=kg, kpg=kpg,
                wp_bd=wp_bd, pool_scale=pool_scale[None], w_out=w_out.astype(BF16),
                w_up=w_up.astype(BF16), w_down=w_down.astype(BF16))


def kernel(x, positions, attn_norm, w_in, pool_w, pool_scale, dn_conv, dn_a_log, dn_dt_bias, dn_norm,
           mla_q_a_norm, mla_w_q_b, mla_kv_a_norm, mla_w_kv_b, mla_q_norm, mla_k_norm,
           w_out, mlp_norm, w_up, w_down):
    b, s, d = x.shape
    depth = w_in.shape[0]
    inv_freq = ROPE_THETA ** (-jnp.arange(0, MLA_ROPE, 2, dtype=F32) / MLA_ROPE)
    freq_lanes = _head_block(jnp.zeros((MLA_NOPE,), F32), jnp.concatenate([inv_freq, inv_freq]))[None]
    cos, sina, sinb = _rope_tables(positions, freq_lanes)

    for l in range(depth):
        p = _prep_layer(w_in[l], pool_w[l], pool_scale[l], dn_conv[l], dn_a_log[l], dn_dt_bias[l], dn_norm[l],
                        mla_q_a_norm[l], mla_w_q_b[l], mla_kv_a_norm[l], mla_w_kv_b[l], mla_q_norm[l],
                        mla_k_norm[l], w_out[l], w_up[l], w_down[l])
        qkv, xa, z, cq, ckv, misc = _inproj(x, attn_norm[l][None], p["w_cat"], p["conv_w"])
        y_b = _deltanet(qkv, z, misc, p["alog_l"], p["dtb_l"], p["dn_norm"])
        q, k, vt, qsq, ksq = _mla_prep(cq, ckv, misc, cos, sina, sinb, p["qag"], p["kvag"], p["wq"], p["wk"],
                                       p["wvt"], p["qg"], p["kg"], p["kpg"])
        u, flag = _score_bounds(qsq, ksq)
        y_c = _attention(q, k, vt, u, flag[:, 0, 0])
        x = _post(x, xa, y_b, y_c, p["wp_bd"], p["pool_scale"], p["w_out"], mlp_norm[l][None], p["w_up"], p["w_down"])
    return x
```

```python
import functools
import math

import jax
import jax.numpy as jnp
import numpy as np
from jax import lax
from jax.experimental import pallas as pl
from jax.experimental.pallas import tpu as pltpu

F32 = jnp.float32
BF16 = jnp.bfloat16

D_MODEL = 1024
POOL_GROUPS = 4
POOL_GROUP_DIM = 64
POOL_WIDTH = POOL_GROUPS * POOL_GROUP_DIM
POOL_WINDOWS = (2, 4, 8, 16)
DN_HEADS = 4
DN_HEAD_DIM = 128
DN_WIDTH = DN_HEADS * DN_HEAD_DIM
DN_CONV = 4
DN_CHUNK = 64
MLA_HEADS = 4
MLA_NOPE = 64
MLA_ROPE = 32
MLA_QK_DIM = MLA_NOPE + MLA_ROPE
MLA_V = 64
MLA_WIDTH = MLA_HEADS * MLA_V
Q_LORA = 256
KV_LORA = 128
ROPE_THETA = 10000.0
D_FF = 4 * D_MODEL
EPS = 1e-6
IN_SPLITS = (POOL_WIDTH, 3 * DN_WIDTH, DN_WIDTH, DN_HEADS, DN_HEADS, Q_LORA, KV_LORA, MLA_ROPE)

LANES = 128
SUBLANES = 8
MXU_COLS = 256
VMEM_LIMIT_BYTES = 56 * 1024 * 1024

TM_PROJ = 512
TM_PREP = 1024
DN_TILE = 256
DN_PAIR = 2 * DN_CHUNK
TQ = 1024
TK = 1024
HALF_ROPE = MLA_ROPE // 2
VT_ROWS = MLA_V + 16
SCORE_BOUND_SLACK = 1.02
SCORE_BOUND_LIMIT = 30.0
PE1_LANE = MLA_NOPE
PE2_LANE = MLA_NOPE + 2 * HALF_ROPE


def _dot(a, b):
    return jnp.dot(a, b, preferred_element_type=F32)


def _dot_nt(a, b):
    return lax.dot_general(a, b, (((1,), (1,)), ((), ())), preferred_element_type=F32)


def _rms_rows(x, gain):
    return x * lax.rsqrt(jnp.mean(x * x, axis=-1, keepdims=True) + EPS) * gain


def _split3(x):
    x1 = x.astype(BF16)
    r1 = x - x1.astype(F32)
    x2 = r1.astype(BF16)
    r2 = r1 - x2.astype(F32)
    return x1, x2, r2.astype(BF16)


def _sigmoid(x):
    return 1.0 / (1.0 + jnp.exp(-x))


def _params(*sem):
    return pltpu.CompilerParams(dimension_semantics=sem, vmem_limit_bytes=VMEM_LIMIT_BYTES)


def _const_spec(shape):
    nd = len(shape)
    return pl.BlockSpec(shape, lambda *_: (0,) * nd, pipeline_mode=pl.Buffered(1))


IN_SEGS = (3 * DN_WIDTH, POOL_WIDTH, DN_WIDTH, Q_LORA, KV_LORA, LANES)
CONV_HALO = SUBLANES


def _inproj_kernel(x_ref, g_ref, w_ref, cw_ref, qkv_ref, *rest):
    o_refs, (halo_sc, ext_sc) = rest[:-2], rest[-2:]
    tm = x_ref.shape[1]
    nqkv = 3 * DN_WIDTH
    D = DN_HEAD_DIM

    @pl.when(pl.program_id(1) == 0)
    def _():
        halo_sc[...] = jnp.zeros_like(halo_sc)

    h = _rms_rows(x_ref[0], g_ref[...]).astype(BF16)
    ext_sc[0:CONV_HALO, :] = halo_sc[...]

    def conv_cols(c0):
        e = ext_sc[:, c0:c0 + D]
        cw = cw_ref[:, c0:c0 + D]
        e1 = pltpu.roll(e, 1, 0)
        near = cw[3:4] * e + cw[2:3] * e1
        far = cw[1:2] * e + cw[0:1] * e1
        acc = (near + pltpu.roll(far, 2, 0))[CONV_HALO:]
        y = acc * _sigmoid(acc)
        if c0 < 2 * DN_WIDTH:
            y = y * lax.rsqrt(jnp.sum(y * y, axis=-1, keepdims=True) + EPS)
            if c0 < DN_WIDTH:
                y = y * (D ** -0.5)
        qkv_ref[0, :, c0:c0 + D] = y

    pieces = []
    off = nqkv
    for o_ref in o_refs:
        n = o_ref.shape[-1]
        pieces += [(o_ref, c, off + c, LANES) for c in range(0, n, LANES)]
        off += n
    per_dot = MXU_COLS // LANES
    others = [pieces[i:i + per_dot] for i in range(0, len(pieces), per_dot)]
    def qkv_cols(c0):
        ext_sc[CONV_HALO:, c0:c0 + MXU_COLS] = _dot(h, w_ref[:, c0:c0 + MXU_COLS])

    nsteps = nqkv // MXU_COLS
    qkv_cols(0)
    for i in range(nsteps):
        if i + 1 < nsteps:
            qkv_cols((i + 1) * MXU_COLS)
        for c in range(i * MXU_COLS, (i + 1) * MXU_COLS, D):
            conv_cols(c)
        for group in others[i::nsteps]:
            w0 = group[0][2]
            res = _dot(h, w_ref[:, w0:w0 + LANES * len(group)])
            for o_ref, c, woff, n in group:
                o_ref[0, :, c:c + n] = res[:, woff - w0:woff - w0 + n]
    halo_sc[...] = ext_sc[tm:tm + CONV_HALO, :]


def _inproj(x, gain, w_cat, conv_w):
    b, s, _ = x.shape
    tm = TM_PROJ
    n_all = sum(IN_SEGS)
    return pl.pallas_call(
        _inproj_kernel,
        grid=(b, s // tm),
        in_specs=[pl.BlockSpec((1, tm, D_MODEL), lambda i, j: (i, j, 0)),
                  _const_spec((1, D_MODEL)),
                  _const_spec((D_MODEL, n_all)),
                  _const_spec((DN_CONV, 3 * DN_WIDTH))],
        out_specs=[pl.BlockSpec((1, tm, n), lambda i, j: (i, j, 0)) for n in IN_SEGS],
        out_shape=[jax.ShapeDtypeStruct((b, s, n), F32) for n in IN_SEGS],
        scratch_shapes=[pltpu.VMEM((CONV_HALO, 3 * DN_WIDTH), F32),
                        pltpu.VMEM((tm + CONV_HALO, 3 * DN_WIDTH), F32)],
        compiler_params=_params("parallel", "arbitrary"),
        name="inproj",
    )(x, gain, w_cat, conv_w)


def _dn_kernel(qkv_ref, z_ref, misc_ref, alog_ref, dtb_ref, ng_ref, y_ref, state_sc):
    nb = qkv_ref.shape[0]
    L = DN_TILE
    C = DN_CHUNK
    nchunk = L // C
    D = DN_HEAD_DIM
    s_idx = pl.program_id(0)
    chains = [(b, h) for b in range(nb) for h in range(DN_HEADS)]

    @pl.when(s_idx == 0)
    def _():
        state_sc[...] = jnp.zeros_like(state_sc)

    def chunk_masks(n):
        ri = lax.broadcasted_iota(jnp.int32, (n, n), 0)
        ci = lax.broadcasted_iota(jnp.int32, (n, n), 1)
        shift = DN_CHUNK.bit_length() - 1
        same = (ri >> shift) == (ci >> shift)
        return same, same & (ci <= ri), same & (ci < ri), ri == ci

    same_l, causal_l, _, _ = chunk_masks(L)
    cum_mat = jnp.concatenate([causal_l.astype(BF16), same_l.astype(BF16)], axis=0)
    P = DN_PAIR
    nblk = L // P
    _, causal_bd, strict_bd, diag = chunk_masks(P)
    eye = diag.astype(F32)

    beta_all, gcum_all, glast_all, gcum_t = [], [], [], []
    for b in range(nb):
        misc = misc_ref[b]
        beta_all.append(_sigmoid(misc))
        sp_in = misc + dtb_ref[...]
        softplus = jnp.maximum(sp_in, 0.0) + jnp.log1p(jnp.exp(-jnp.abs(sp_in)))
        g_all = -jnp.exp(alog_ref[...]) * softplus
        g1, g2, g3 = _split3(g_all)
        cum = _dot(cum_mat, g1) + _dot(cum_mat, g2) + _dot(cum_mat, g3)
        gcum_all.append(cum[:L])
        glast_all.append(cum[L:])
        gcum_t.append(cum[:L].T)

    def stage1(b, h):
        a_mats, attns = [], []
        lane = DN_HEADS + h
        gc_col = gcum_all[b][:, lane:lane + 1]
        gl_col = glast_all[b][:, lane:lane + 1]
        gc_row = gcum_t[b][lane:lane + 1, :]
        beta = beta_all[b][:, h:h + 1]
        qn = qkv_ref[b, :, h * D:(h + 1) * D]
        kn = qkv_ref[b, :, DN_WIDTH + h * D:DN_WIDTH + (h + 1) * D]
        v = qkv_ref[b, :, 2 * DN_WIDTH + h * D:2 * DN_WIDTH + (h + 1) * D]
        kb = kn * beta
        e_col = jnp.exp(gc_col)
        kn16 = kn.astype(BF16)
        kb16 = kb.astype(BF16)
        qn16 = qn.astype(BF16)
        for j in range(nblk):
            rows = slice(j * P, (j + 1) * P)
            decay = jnp.exp(jnp.where(causal_bd, gc_col[rows] - gc_row[:, rows], 0.0))
            gram = _dot_nt(jnp.concatenate([kb16[rows], qn16[rows]], axis=0), kn16[rows])
            a_mats.append(jnp.where(strict_bd, gram[:P] * decay, 0.0))
            attns.append(jnp.where(causal_bd, gram[P:] * decay, 0.0).astype(BF16))
        return dict(a=a_mats, attn=attns,
                    rhs=jnp.concatenate([v * beta, kb * e_col], axis=1).astype(BF16),
                    qd=(qn * e_col).astype(BF16),
                    kdt=(kn * jnp.exp(gl_col - gc_col)).T.astype(BF16),
                    gdec=jnp.exp(jnp.broadcast_to(gl_col, (L, D))))

    def stage2(group):
        a_mats = [a for ch in group for a in ch["a"]]
        xps = [(-a).astype(BF16) for a in a_mats]
        t_invs = [eye - a for a in a_mats]
        xps = [_dot(xp, xp).astype(BF16) for xp in xps]
        for _ in range(4):
            prods = [_dot(jnp.concatenate([t.astype(BF16), xp], axis=0), xp) for t, xp in zip(t_invs, xps)]
            t_invs = [t + pr[:P] for t, pr in zip(t_invs, prods)]
            xps = [pr[P:].astype(BF16) for pr in prods]
        t_invs = [t + _dot(t.astype(BF16), xp) for t, xp in zip(t_invs, xps)]
        for i, ch in enumerate(group):
            ch["uw"] = [_dot(t_invs[i * nblk + j].astype(BF16), ch["rhs"][j * P:(j + 1) * P])
                        for j in range(nblk)]

    chs = [stage1(b, h) for b, h in chains]
    stage2(chs)

    states = [state_sc[b, h] for b, h in chains]
    o_parts = [[] for _ in chains]
    zeros_c = jnp.zeros((C, D), BF16)
    for c in range(nchunk):
        r0 = c * C
        j, half = divmod(c, P // C)
        p0 = half * C
        rs = [_dot(jnp.concatenate([ch["uw"][j][p0:p0 + C, D:].astype(BF16), ch["qd"][r0:r0 + C]], axis=0),
                   states[i].astype(BF16)) for i, ch in enumerate(chs)]
        for i, ch in enumerate(chs):
            v_new = (ch["uw"][j][p0:p0 + C, :D] - rs[i][:C]).astype(BF16)
            v_blk = jnp.concatenate([zeros_c] * half + [v_new] + [zeros_c] * (P // C - 1 - half), axis=0)
            lhs = jnp.concatenate([ch["attn"][j][p0:p0 + C, :], ch["kdt"][:, j * P:(j + 1) * P]], axis=0)
            m2 = _dot(lhs, v_blk)
            o_parts[i].append(rs[i][C:] + m2[:C])
            states[i] = states[i] * ch["gdec"][r0:r0 + 1, :] + m2[C:]

    for i, (b, h) in enumerate(chains):
        state_sc[b, h] = states[i]
        o = jnp.concatenate(o_parts[i], axis=0)
        zh = z_ref[b, :, h * D:(h + 1) * D]
        y_ref[b, :, h * D:(h + 1) * D] = (_rms_rows(o, ng_ref[...]) * (zh * _sigmoid(zh))).astype(y_ref.dtype)


def _deltanet(qkv, z, misc, alog_l, dtb_l, norm_gain):
    b, s, _ = qkv.shape
    L = DN_TILE
    return pl.pallas_call(
        _dn_kernel,
        grid=(s // L,),
        in_specs=[pl.BlockSpec((b, L, 3 * DN_WIDTH), lambda j: (0, j, 0)),
                  pl.BlockSpec((b, L, DN_WIDTH), lambda j: (0, j, 0)),
                  pl.BlockSpec((b, L, LANES), lambda j: (0, j, 0)),
                  _const_spec((1, LANES)),
                  _const_spec((1, LANES)),
                  _const_spec((1, DN_HEAD_DIM))],
        out_specs=pl.BlockSpec((b, L, DN_WIDTH), lambda j: (0, j, 0)),
        out_shape=jax.ShapeDtypeStruct((b, s, DN_WIDTH), BF16),
        scratch_shapes=[pltpu.VMEM((b, DN_HEADS, DN_HEAD_DIM, DN_HEAD_DIM), F32)],
        compiler_params=_params("arbitrary"),
        name="deltanet",
    )(qkv, z, misc, alog_l, dtb_l, norm_gain)


def _rope_kernel(pos_ref, freq_ref, cos_ref, sina_ref, sinb_ref):
    tm = pos_ref.shape[1]
    half = tm // 2
    fold = LANES // 2
    lane = lax.broadcasted_iota(jnp.int32, (half, LANES), 1)
    upper = lane >= fold
    freq = freq_ref[...]
    freq2 = freq + pltpu.roll(freq, fold, 1)
    pos = jnp.where(upper, pos_ref[0, 0:half].astype(F32), pos_ref[0, half:tm].astype(F32))
    ang = pos * freq2
    cos2 = jnp.cos(ang)
    sin2 = jnp.sin(ang)
    pe1 = (lane >= PE1_LANE) & (lane < PE1_LANE + HALF_ROPE)
    pe2 = (lane >= PE2_LANE) & (lane < PE2_LANE + HALF_ROPE)
    for rows, cos, sin in ((slice(0, half), cos2, sin2),
                           (slice(half, tm), pltpu.roll(cos2, fold, 1), pltpu.roll(sin2, fold, 1))):
        cos_ref[0, rows] = jnp.where(lane < MLA_NOPE, 1.0, jnp.where(pe1 | pe2, cos, 0.0))
        sina_ref[0, rows] = jnp.where(pe2, sin, 0.0)
        sinb_ref[0, rows] = jnp.where(pe1, -sin, 0.0)


def _rope_tables(positions, freq_lanes):
    b, s = positions.shape
    tm = TM_PREP
    spec = pl.BlockSpec((1, tm, LANES), lambda i, j: (i, j, 0))
    return pl.pallas_call(
        _rope_kernel,
        grid=(b, s // tm),
        in_specs=[pl.BlockSpec((1, tm, 1), lambda i, j: (i, j, 0)), _const_spec((1, LANES))],
        out_specs=[spec, spec, spec],
        out_shape=[jax.ShapeDtypeStruct((b, s, LANES), F32)] * 3,
        compiler_params=_params("parallel", "parallel"),
        name="rope_tables",
    )(positions.reshape(b, s, 1), freq_lanes)


def _rope(x, cos, sina, sinb):
    return x * cos + pltpu.roll(x, 2 * HALF_ROPE, 1) * sina + pltpu.roll(x, LANES - 2 * HALF_ROPE, 1) * sinb


def _mla_prep_kernel(cq_ref, ckv_ref, misc_ref, cos_ref, sina_ref, sinb_ref,
                     qag_ref, kvag_ref, wq_ref, wk_ref, wvt_ref, qg_ref, kg_ref, kpg_ref,
                     q_ref, k_ref, vt_ref, qsq_ref, ksq_ref):
    tm = cq_ref.shape[1]
    ones8 = jnp.ones((SUBLANES, LANES), BF16)
    eye = (lax.broadcasted_iota(jnp.int32, (LANES, LANES), 0)
           == lax.broadcasted_iota(jnp.int32, (LANES, LANES), 1)).astype(BF16)

    def row_sq_norms(x16):
        xf = x16.astype(F32)
        return _dot_nt(ones8, (xf * xf).astype(BF16))

    cos = cos_ref[0]
    sina = sina_ref[0]
    sinb = sinb_ref[0]
    lane = lax.broadcasted_iota(jnp.int32, (tm, LANES), 1)
    is_nope = lane < MLA_NOPE

    cqn = _rms_rows(cq_ref[0], qag_ref[...]).astype(BF16)
    ckvn = _rms_rows(ckv_ref[0], kvag_ref[...]).astype(BF16)
    qf = _dot(cqn, wq_ref[...])
    kf = _dot(ckvn, wk_ref[...])
    vt = _dot_nt(wvt_ref[...], ckvn)
    vrow = lax.broadcasted_iota(jnp.int32, vt.shape, 0)
    ones_row = functools.reduce(jnp.logical_or, [vrow == h * VT_ROWS + MLA_V for h in range(MLA_HEADS)])
    vt_ref[0] = jnp.where(ones_row, 1.0, vt).astype(BF16)

    kp = jnp.where(is_nope, 0.0, misc_ref[0])
    kp_ms = jnp.sum(kp * kp, axis=-1, keepdims=True) * (1.0 / MLA_ROPE)
    kp = _rope(kp * lax.rsqrt(kp_ms + EPS) * kpg_ref[...], cos, sina, sinb)

    scale = MLA_QK_DIM ** -0.5 * math.log2(math.e)
    for h in range(MLA_HEADS):
        xq = qf[:, h * LANES:(h + 1) * LANES]
        sq = xq * xq
        ms_n = jnp.sum(jnp.where(is_nope, sq, 0.0), axis=-1, keepdims=True) * (1.0 / MLA_NOPE)
        ms_p = jnp.sum(jnp.where(is_nope, 0.0, sq), axis=-1, keepdims=True) * (1.0 / MLA_ROPE)
        inv = jnp.where(is_nope, lax.rsqrt(ms_n + EPS), lax.rsqrt(ms_p + EPS))
        qh = _rope(xq * inv * qg_ref[...], cos, sina, sinb) * scale
        qt = _dot_nt(eye, qh.astype(BF16))
        q_ref[0, h] = qt.astype(BF16)
        qsq_ref[0, h] = jnp.broadcast_to(jnp.sum(qt * qt, axis=0, keepdims=True), (SUBLANES, tm))

        xk = kf[:, h * LANES:(h + 1) * LANES]
        ms_k = jnp.sum(xk * xk, axis=-1, keepdims=True) * (1.0 / MLA_NOPE)
        kh = xk * lax.rsqrt(ms_k + EPS) * kg_ref[...] + kp
        k16 = kh.astype(BF16)
        k_ref[0, h] = k16
        ksq_ref[0, h] = row_sq_norms(k16)


def _mla_prep(cq, ckv, misc, cos, sina, sinb, qag, kvag, wq, wk, wvt, qg, kg, kpg):
    b, s, _ = cq.shape
    tm = TM_PREP
    hl = MLA_HEADS * LANES

    def tok(n):
        return pl.BlockSpec((1, tm, n), lambda i, j: (i, j, 0))

    return pl.pallas_call(
        _mla_prep_kernel,
        grid=(b, s // tm),
        in_specs=[tok(Q_LORA), tok(KV_LORA), tok(LANES), tok(LANES), tok(LANES), tok(LANES),
                  _const_spec((1, Q_LORA)), _const_spec((1, KV_LORA)),
                  _const_spec((Q_LORA, hl)), _const_spec((KV_LORA, hl)), _const_spec((MLA_HEADS * VT_ROWS, KV_LORA)),
                  _const_spec((1, LANES)), _const_spec((1, LANES)), _const_spec((1, LANES))],
        out_specs=[pl.BlockSpec((1, MLA_HEADS, LANES, tm), lambda i, j: (i, 0, 0, j)),
                   pl.BlockSpec((1, MLA_HEADS, tm, LANES), lambda i, j: (i, 0, j, 0)),
                   pl.BlockSpec((1, MLA_HEADS * VT_ROWS, tm), lambda i, j: (i, 0, j)),
                   pl.BlockSpec((1, MLA_HEADS, SUBLANES, tm), lambda i, j: (i, 0, 0, j)),
                   pl.BlockSpec((1, MLA_HEADS, SUBLANES, tm), lambda i, j: (i, 0, 0, j))],
        out_shape=[jax.ShapeDtypeStruct((b, MLA_HEADS, LANES, s), BF16),
                   jax.ShapeDtypeStruct((b, MLA_HEADS, s, LANES), BF16),
                   jax.ShapeDtypeStruct((b, MLA_HEADS * VT_ROWS, s), BF16),
                   jax.ShapeDtypeStruct((b, MLA_HEADS, SUBLANES, s), F32),
                   jax.ShapeDtypeStruct((b, MLA_HEADS, SUBLANES, s), F32)],
        compiler_params=_params("parallel", "parallel"),
        name="mla_prep",
    )(cq, ckv, misc, cos, sina, sinb, qag, kvag, wq, wk, wvt, qg, kg, kpg)


def _bounds_kernel(qsq_ref, ksq_ref, u_ref, flag_ref):
    worst = None
    for h in range(MLA_HEADS):
        kmax = jnp.max(ksq_ref[0, h], axis=-1, keepdims=True)
        u = jnp.sqrt(qsq_ref[0, h] * kmax) * SCORE_BOUND_SLACK
        u_ref[0, h] = u
        umax = jnp.max(u, axis=-1, keepdims=True)
        worst = umax if worst is None else jnp.maximum(worst, umax)
    flag_ref[0] = jnp.broadcast_to((worst <= SCORE_BOUND_LIMIT).astype(jnp.int32), flag_ref.shape[1:])


def _score_bounds(qsq, ksq):
    b, h, r, s = qsq.shape
    spec = pl.BlockSpec((1, h, r, s), lambda i: (i, 0, 0, 0))
    return pl.pallas_call(
        _bounds_kernel,
        grid=(b,),
        in_specs=[spec, spec],
        out_specs=[spec, pl.BlockSpec((1, SUBLANES, LANES), lambda i: (i, 0, 0))],
        out_shape=[jax.ShapeDtypeStruct((b, h, r, s), F32), jax.ShapeDtypeStruct((b, SUBLANES, LANES), jnp.int32)],
        compiler_params=_params("parallel"),
        name="score_bounds",
    )(qsq, ksq)


def _attn_block(q_ref, k_ref, vt_ref, u_ref, m_sc, acc_sc, visible, bounded):
    scores = [_dot(k_ref[0, 0], q_ref[0, 0])]
    for h in range(MLA_HEADS):
        if h + 1 < MLA_HEADS:
            scores.append(_dot(k_ref[0, h + 1], q_ref[0, h + 1]))
        st = scores[h]
        if visible is not None:
            st = jnp.where(visible, st, -jnp.inf)
        vt = vt_ref[0, h * VT_ROWS:(h + 1) * VT_ROWS, :]
        if bounded:
            pt = jnp.exp2(st - u_ref[0, h, 0:1, :])
            acc_sc[h] = acc_sc[h] + _dot(vt, pt.astype(BF16))
        else:
            m_prev = m_sc[h]
            m_new = jnp.maximum(m_prev, jnp.max(st, axis=0, keepdims=True))
            alpha = jnp.exp2(m_prev - m_new)
            pt = jnp.exp2(st - m_new)
            acc_sc[h] = alpha * acc_sc[h] + _dot(vt, pt.astype(BF16))
            m_sc[h] = m_new


def _attn_diag_block(q_ref, k_ref, vt_ref, u_ref, acc_sc):
    hk = TK // 2
    r = lax.broadcasted_iota(jnp.int32, (hk, TQ), 0)
    c = lax.broadcasted_iota(jnp.int32, (hk, TQ), 1)
    vis_old = r <= c
    vis_new = vis_old[:, :hk]

    def scores(h):
        return (_dot(k_ref[0, h, 0:hk, :], q_ref[0, h]),
                _dot(k_ref[0, h, hk:, :], q_ref[0, h, :, hk:]))

    nxt = scores(0)
    for h in range(MLA_HEADS):
        s_old, s_new = nxt
        if h + 1 < MLA_HEADS:
            nxt = scores(h + 1)
        u = u_ref[0, h, 0:1, :]
        p_old = jnp.exp2(jnp.where(vis_old, s_old, -jnp.inf) - u)
        p_new = jnp.exp2(jnp.where(vis_new, s_new, -jnp.inf) - u[:, hk:])
        vt = vt_ref[0, h * VT_ROWS:(h + 1) * VT_ROWS, :]
        acc_sc[h] = acc_sc[h] + _dot(vt[:, :hk], p_old.astype(BF16))
        acc_sc[h, :, hk:] = acc_sc[h, :, hk:] + _dot(vt[:, hk:], p_new.astype(BF16))


def _attn_kernel(qi_ref, kj_ref, last_ref, flag_ref, q_ref, k_ref, vt_ref, u_ref, o_ref, m_sc, acc_sc):
    p = pl.program_id(1)
    qi = qi_ref[p]
    kj = kj_ref[p]
    bounded = flag_ref[pl.program_id(0)] == 1

    @pl.when(kj == 0)
    def _():
        m_sc[...] = jnp.full_like(m_sc, -jnp.inf)
        acc_sc[...] = jnp.zeros_like(acc_sc)

    all_visible = kj * TK + (TK - 1) <= qi * TQ

    def run(visible_fn, use_bound):
        def body():
            _attn_block(q_ref, k_ref, vt_ref, u_ref, m_sc, acc_sc, visible_fn(), use_bound)
        return body

    def causal_mask():
        kpos = kj * TK + lax.broadcasted_iota(jnp.int32, (TK, TQ), 0)
        qpos = qi * TQ + lax.broadcasted_iota(jnp.int32, (TK, TQ), 1)
        return kpos <= qpos

    partly = jnp.logical_not(all_visible)
    unbounded = jnp.logical_not(bounded)
    pl.when(all_visible & bounded)(run(lambda: None, True))
    if TQ == TK:
        pl.when(partly & bounded)(lambda: _attn_diag_block(q_ref, k_ref, vt_ref, u_ref, acc_sc))
    else:
        pl.when(partly & bounded)(run(causal_mask, True))
    pl.when(all_visible & unbounded)(run(lambda: None, False))
    pl.when(partly & unbounded)(run(causal_mask, False))

    @pl.when(last_ref[p] == 1)
    def _():
        out_t = jnp.concatenate([acc_sc[h, :MLA_V] / acc_sc[h, MLA_V:MLA_V + 1] for h in range(MLA_HEADS)], axis=0)
        o_ref[0] = out_t.T.astype(o_ref.dtype)


def _attention(q, k, vt, u, flag):
    b, _, s, _ = k.shape
    nq = s // TQ
    pairs = [(i, j) for i in range(nq) for j in range((i * TQ + TQ - 1) // TK + 1)]
    qi = jnp.asarray(np.array([p[0] for p in pairs], np.int32))
    kj = jnp.asarray(np.array([p[1] for p in pairs], np.int32))
    last = jnp.asarray(np.array([int(p[1] == (p[0] * TQ + TQ - 1) // TK) for p in pairs], np.int32))
    grid_spec = pltpu.PrefetchScalarGridSpec(
        num_scalar_prefetch=4,
        grid=(b, len(pairs)),
        in_specs=[pl.BlockSpec((1, MLA_HEADS, LANES, TQ), lambda i, p, qi, kj, *_: (i, 0, 0, qi[p])),
                  pl.BlockSpec((1, MLA_HEADS, TK, LANES), lambda i, p, qi, kj, *_: (i, 0, kj[p], 0)),
                  pl.BlockSpec((1, MLA_HEADS * VT_ROWS, TK), lambda i, p, qi, kj, *_: (i, 0, kj[p])),
                  pl.BlockSpec((1, MLA_HEADS, SUBLANES, TQ), lambda i, p, qi, kj, *_: (i, 0, 0, qi[p]))],
        out_specs=pl.BlockSpec((1, TQ, MLA_WIDTH), lambda i, p, qi, kj, *_: (i, qi[p], 0)),
        scratch_shapes=[pltpu.VMEM((MLA_HEADS, 1, TQ), F32),
                        pltpu.VMEM((MLA_HEADS, VT_ROWS, TQ), F32)],
    )
    return pl.pallas_call(
        _attn_kernel,
        grid_spec=grid_spec,
        out_shape=jax.ShapeDtypeStruct((b, s, MLA_WIDTH), BF16),
        compiler_params=_params("parallel", "arbitrary"),
        name="mla_attention",
    )(qi, kj, last, flag, q, k, vt, u)


POOL_HALO = 16


def _post_kernel(x_ref, xa_ref, yb_ref, yc_ref, wp_ref, ps_ref, wo_ref, g_ref, wu_ref, wd_ref, o_ref,
                 halo_sc, ext_sc):
    ts = x_ref.shape[1]
    s_idx = pl.program_id(1)

    @pl.when(s_idx == 0)
    def _():
        halo_sc[...] = jnp.zeros_like(halo_sc)

    xa = xa_ref[0]
    ext_sc[0:POOL_HALO, :] = halo_sc[...]
    ext_sc[POOL_HALO:, :] = xa
    halo_sc[...] = xa_ref[0, ts - POOL_HALO:ts, :]

    assert POOL_WINDOWS == tuple(2 << g for g in range(POOL_GROUPS)) and POOL_WINDOWS[-1] <= POOL_HALO
    e = ext_sc[...]
    sums = []
    for g in range(POOL_GROUPS):
        e = e + pltpu.roll(e, 1 << g, 0)
        sums.append(e[POOL_HALO:])
    lane = lax.broadcasted_iota(jnp.int32, (ts, POOL_WIDTH), 1)
    grp = lane >> (POOL_GROUP_DIM.bit_length() - 1)
    win = jnp.left_shift(2, grp)
    t = s_idx * ts + lax.broadcasted_iota(jnp.int32, (ts, POOL_WIDTH), 0)
    count = jnp.minimum(t + 1, win).astype(F32)
    pooled = sums[POOL_GROUPS - 1]
    for g in range(POOL_GROUPS - 2, -1, -1):
        pooled = jnp.where(grp == g, sums[g], pooled)
    ya = _dot((pooled / count - xa).astype(BF16), wp_ref[...]) * ps_ref[...]

    acc = x_ref[0] + _dot(ya.astype(BF16), wo_ref[0:POOL_WIDTH, :])
    acc = acc + _dot(yb_ref[0].astype(BF16), wo_ref[POOL_WIDTH:POOL_WIDTH + DN_WIDTH, :])
    x1 = acc + _dot(yc_ref[0].astype(BF16), wo_ref[POOL_WIDTH + DN_WIDTH:, :])

    h = _rms_rows(x1, g_ref[...]).astype(BF16)
    u = jnp.maximum(_dot(h, wu_ref[...]), 0.0)
    o_ref[0] = x1 + _dot((u * u).astype(BF16), wd_ref[...])


def _post(x, xa, yb, yc, wp_bd, pool_scale, w_out, gain, w_up, w_down):
    b, s, _ = x.shape
    ts = TM_PROJ

    def tok(n):
        return pl.BlockSpec((1, ts, n), lambda i, j: (i, j, 0))

    return pl.pallas_call(
        _post_kernel,
        grid=(b, s // ts),
        in_specs=[tok(D_MODEL), tok(POOL_WIDTH), tok(DN_WIDTH), tok(MLA_WIDTH),
                  _const_spec((POOL_WIDTH, POOL_WIDTH)), _const_spec((1, POOL_WIDTH)),
                  _const_spec((D_MODEL, D_MODEL)), _const_spec((1, D_MODEL)),
                  _const_spec((D_MODEL, D_FF)), _const_spec((D_FF, D_MODEL))],
        out_specs=tok(D_MODEL),
        out_shape=jax.ShapeDtypeStruct((b, s, D_MODEL), F32),
        scratch_shapes=[pltpu.VMEM((POOL_HALO, POOL_WIDTH), F32),
                        pltpu.VMEM((ts + POOL_HALO, POOL_WIDTH), F32)],
        compiler_params=_params("parallel", "arbitrary"),
        name="outproj_pool_mlp",
    )(x, xa, yb, yc, wp_bd, pool_scale, w_out, gain, w_up, w_down)


def _head_block(nope, pe):
    z16 = jnp.zeros(pe.shape[:-1] + (HALF_ROPE,), pe.dtype)
    return jnp.concatenate([nope, pe[..., :HALF_ROPE], z16, pe[..., HALF_ROPE:], z16], axis=-1)


def _prep_layer(w_in, pool_w, pool_scale, dn_conv, dn_a_log, dn_dt_bias, dn_norm,
                q_a_norm, w_q_b, kv_a_norm, w_kv_b, q_norm, k_norm, w_out, w_up, w_down):
    offs = np.cumsum((0,) + IN_SPLITS)
    seg = [w_in[:, offs[i]:offs[i + 1]] for i in range(len(IN_SPLITS))]
    w_xa, w_qkv, w_z, w_b, w_a, w_cq, w_ckv, w_kpe = seg
    d = w_in.shape[0]
    z64 = jnp.zeros((d, MLA_NOPE), w_in.dtype)
    w_misc = jnp.concatenate([w_b, w_a, jnp.zeros((d, MLA_NOPE - 2 * DN_HEADS), w_in.dtype),
                              _head_block(z64, w_kpe)[:, MLA_NOPE:]], axis=1)
    w_cat = jnp.concatenate([w_qkv, w_xa, w_z, w_cq, w_ckv, w_misc], axis=1).astype(BF16)

    lane_pad = jnp.zeros((LANES - 2 * DN_HEADS,), F32)
    alog_l = jnp.concatenate([jnp.zeros((DN_HEADS,), F32), dn_a_log, lane_pad])[None]
    dtb_l = jnp.concatenate([jnp.zeros((DN_HEADS,), F32), dn_dt_bias, lane_pad])[None]

    wq = w_q_b.reshape(Q_LORA, MLA_HEADS, MLA_QK_DIM)
    wq = _head_block(wq[..., :MLA_NOPE], wq[..., MLA_NOPE:]).reshape(Q_LORA, MLA_HEADS * LANES).astype(BF16)
    wkv = w_kv_b.reshape(KV_LORA, MLA_HEADS, MLA_NOPE + MLA_V)
    wk = _head_block(wkv[..., :MLA_NOPE], jnp.zeros((KV_LORA, MLA_HEADS, MLA_ROPE), F32))
    wk = wk.reshape(KV_LORA, MLA_HEADS * LANES).astype(BF16)
    wv = jnp.concatenate([wkv[..., MLA_NOPE:], jnp.zeros((KV_LORA, MLA_HEADS, VT_ROWS - MLA_V), F32)], axis=-1)
    wvt = wv.reshape(KV_LORA, MLA_HEADS * VT_ROWS).T.astype(BF16)
    qg = _head_block(q_norm[:MLA_NOPE], q_norm[MLA_NOPE:])[None]
    kg = _head_block(k_norm[:MLA_NOPE], jnp.zeros((MLA_ROPE,), F32))[None]
    kpg = _head_block(jnp.zeros((MLA_NOPE,), F32), k_norm[MLA_NOPE:])[None]

    wp_bd = jax.scipy.linalg.block_diag(*[pool_w[g] for g in range(POOL_GROUPS)]).astype(BF16)
    return dict(w_cat=w_cat, alog_l=alog_l, dtb_l=dtb_l, conv_w=dn_conv, dn_norm=dn_norm[None],
                qag=q_a_norm[None], kvag=kv_a_norm[None], wq=wq, wk=wk, wvt=wvt, qg=qg, kg=kg, kpg=kpg,
                wp_bd=wp_bd, pool_scale=pool_scale[None], w_out=w_out.astype(BF16),
                w_up=w_up.astype(BF16), w_down=w_down.astype(BF16))


def kernel(x, positions, attn_norm, w_in, pool_w, pool_scale, dn_conv, dn_a_log, dn_dt_bias, dn_norm,
           mla_q_a_norm, mla_w_q_b, mla_kv_a_norm, mla_w_kv_b, mla_q_norm, mla_k_norm,
           w_out, mlp_norm, w_up, w_down):
    b, s, d = x.shape
    depth = w_in.shape[0]
    inv_freq = ROPE_THETA ** (-jnp.arange(0, MLA_ROPE, 2, dtype=F32) / MLA_ROPE)
    freq_lanes = _head_block(jnp.zeros((MLA_NOPE,), F32), jnp.concatenate([inv_freq, inv_freq]))[None]
    cos, sina, sinb = _rope_tables(positions, freq_lanes)

    for l in range(depth):
        p = _prep_layer(w_in[l], pool_w[l], pool_scale[l], dn_conv[l], dn_a_log[l], dn_dt_bias[l], dn_norm[l],
                        mla_q_a_norm[l], mla_w_q_b[l], mla_kv_a_norm[l], mla_w_kv_b[l], mla_q_norm[l],
                        mla_k_norm[l], w_out[l], w_up[l], w_down[l])
        qkv, xa, z, cq, ckv, misc = _inproj(x, attn_norm[l][None], p["w_cat"], p["conv_w"])
        y_b = _deltanet(qkv, z, misc, p["alog_l"], p["dtb_l"], p["dn_norm"])
        q, k, vt, qsq, ksq = _mla_prep(cq, ckv, misc, cos, sina, sinb, p["qag"], p["kvag"], p["wq"], p["wk"],
                                       p["wvt"], p["qg"], p["kg"], p["kpg"])
        u, flag = _score_bounds(qsq, ksq)
        y_c = _attention(q, k, vt, u, flag[:, 0, 0])
        x = _post(x, xa, y_b, y_c, p["wp_bd"], p["pool_scale"], p["w_out"], mlp_norm[l][None], p["w_up"], p["w_down"])
    return x
```

```python
import functools
import math

import jax
import jax.numpy as jnp
import numpy as np
from jax import lax
from jax.experimental import pallas as pl
from jax.experimental.pallas import tpu as pltpu

F32 = jnp.float32
BF16 = jnp.bfloat16

D_MODEL = 1024
POOL_GROUPS = 4
POOL_GROUP_DIM = 64
POOL_WIDTH = POOL_GROUPS * POOL_GROUP_DIM
POOL_WINDOWS = (2, 4, 8, 16)
DN_HEADS = 4
DN_HEAD_DIM = 128
DN_WIDTH = DN_HEADS * DN_HEAD_DIM
DN_CONV = 4
DN_CHUNK = 64
MLA_HEADS = 4
MLA_NOPE = 64
MLA_ROPE = 32
MLA_QK_DIM = MLA_NOPE + MLA_ROPE
MLA_V = 64
MLA_WIDTH = MLA_HEADS * MLA_V
Q_LORA = 256
KV_LORA = 128
ROPE_THETA = 10000.0
D_FF = 4 * D_MODEL
EPS = 1e-6
IN_SPLITS = (POOL_WIDTH, 3 * DN_WIDTH, DN_WIDTH, DN_HEADS, DN_HEADS, Q_LORA, KV_LORA, MLA_ROPE)

LANES = 128
SUBLANES = 8
MXU_COLS = 256
VMEM_LIMIT_BYTES = 56 * 1024 * 1024

TM_INPROJ = 256
TM_PROJ = 512
TM_PREP = 1024
DN_TILE = 256
DN_PAIR = 2 * DN_CHUNK
TQ = 1024
TK = 1024
HALF_ROPE = MLA_ROPE // 2
VT_ROWS = MLA_V + 16
SCORE_BOUND_SLACK = 1.02
SCORE_BOUND_LIMIT = 30.0
PE1_LANE = MLA_NOPE
PE2_LANE = MLA_NOPE + 2 * HALF_ROPE


def _dot(a, b):
    return jnp.dot(a, b, preferred_element_type=F32)


def _dot_nt(a, b):
    return lax.dot_general(a, b, (((1,), (1,)), ((), ())), preferred_element_type=F32)


def _rms_rows(x, gain):
    return x * lax.rsqrt(jnp.mean(x * x, axis=-1, keepdims=True) + EPS) * gain


def _split3(x):
    x1 = x.astype(BF16)
    r1 = x - x1.astype(F32)
    x2 = r1.astype(BF16)
    r2 = r1 - x2.astype(F32)
    return x1, x2, r2.astype(BF16)


def _sigmoid(x):
    return 1.0 / (1.0 + jnp.exp(-x))


def _params(*sem):
    return pltpu.CompilerParams(dimension_semantics=sem, vmem_limit_bytes=VMEM_LIMIT_BYTES)


def _const_spec(shape):
    nd = len(shape)
    return pl.BlockSpec(shape, lambda *_: (0,) * nd, pipeline_mode=pl.Buffered(1))


IN_SEGS = (3 * DN_WIDTH, POOL_WIDTH, DN_WIDTH, Q_LORA, KV_LORA, LANES)
CONV_HALO = SUBLANES


def _inproj_kernel(x_ref, g_ref, w_ref, cw_ref, qkv_ref, *rest):
    o_refs, (halo_sc, ext_sc) = rest[:-2], rest[-2:]
    tm = x_ref.shape[1]
    nqkv = 3 * DN_WIDTH
    D = DN_HEAD_DIM

    @pl.when(pl.program_id(1) == 0)
    def _():
        halo_sc[...] = jnp.zeros_like(halo_sc)

    h = _rms_rows(x_ref[0], g_ref[...]).astype(BF16)
    ext_sc[0:CONV_HALO, :] = halo_sc[...]

    def conv_cols(c0):
        e = ext_sc[:, c0:c0 + D]
        cw = cw_ref[:, c0:c0 + D]
        e1 = pltpu.roll(e, 1, 0)
        near = cw[3:4] * e + cw[2:3] * e1
        far = cw[1:2] * e + cw[0:1] * e1
        acc = (near + pltpu.roll(far, 2, 0))[CONV_HALO:]
        y = acc * _sigmoid(acc)
        if c0 < 2 * DN_WIDTH:
            y = y * lax.rsqrt(jnp.sum(y * y, axis=-1, keepdims=True) + EPS)
            if c0 < DN_WIDTH:
                y = y * (D ** -0.5)
        qkv_ref[0, :, c0:c0 + D] = y

    pieces = []
    off = nqkv
    for o_ref in o_refs:
        n = o_ref.shape[-1]
        pieces += [(o_ref, c, off + c, LANES) for c in range(0, n, LANES)]
        off += n
    per_dot = MXU_COLS // LANES
    others = [pieces[i:i + per_dot] for i in range(0, len(pieces), per_dot)]
    def qkv_cols(c0):
        ext_sc[CONV_HALO:, c0:c0 + MXU_COLS] = _dot(h, w_ref[:, c0:c0 + MXU_COLS])

    nsteps = nqkv // MXU_COLS
    qkv_cols(0)
    for i in range(nsteps):
        if i + 1 < nsteps:
            qkv_cols((i + 1) * MXU_COLS)
        for c in range(i * MXU_COLS, (i + 1) * MXU_COLS, D):
            conv_cols(c)
        for group in others[i::nsteps]:
            w0 = group[0][2]
            res = _dot(h, w_ref[:, w0:w0 + LANES * len(group)])
            for o_ref, c, woff, n in group:
                o_ref[0, :, c:c + n] = res[:, woff - w0:woff - w0 + n]
    halo_sc[...] = ext_sc[tm:tm + CONV_HALO, :]


def _inproj(x, gain, w_cat, conv_w):
    b, s, _ = x.shape
    tm = TM_INPROJ
    n_all = sum(IN_SEGS)
    return pl.pallas_call(
        _inproj_kernel,
        grid=(b, s // tm),
        in_specs=[pl.BlockSpec((1, tm, D_MODEL), lambda i, j: (i, j, 0)),
                  _const_spec((1, D_MODEL)),
                  _const_spec((D_MODEL, n_all)),
                  _const_spec((DN_CONV, 3 * DN_WIDTH))],
        out_specs=[pl.BlockSpec((1, tm, n), lambda i, j: (i, j, 0)) for n in IN_SEGS],
        out_shape=[jax.ShapeDtypeStruct((b, s, n), F32) for n in IN_SEGS],
        scratch_shapes=[pltpu.VMEM((CONV_HALO, 3 * DN_WIDTH), F32),
                        pltpu.VMEM((tm + CONV_HALO, 3 * DN_WIDTH), F32)],
        compiler_params=_params("parallel", "arbitrary"),
        name="inproj",
    )(x, gain, w_cat, conv_w)


def _dn_kernel(qkv_ref, z_ref, misc_ref, alog_ref, dtb_ref, ng_ref, y_ref, state_sc):
    nb = qkv_ref.shape[0]
    L = DN_TILE
    C = DN_CHUNK
    nchunk = L // C
    D = DN_HEAD_DIM
    s_idx = pl.program_id(0)
    chains = [(b, h) for b in range(nb) for h in range(DN_HEADS)]

    @pl.when(s_idx == 0)
    def _():
        state_sc[...] = jnp.zeros_like(state_sc)

    def chunk_masks(n):
        ri = lax.broadcasted_iota(jnp.int32, (n, n), 0)
        ci = lax.broadcasted_iota(jnp.int32, (n, n), 1)
        shift = DN_CHUNK.bit_length() - 1
        same = (ri >> shift) == (ci >> shift)
        return same, same & (ci <= ri), same & (ci < ri), ri == ci

    same_l, causal_l, _, _ = chunk_masks(L)
    cum_mat = jnp.concatenate([causal_l.astype(BF16), same_l.astype(BF16)], axis=0)
    P = DN_PAIR
    nblk = L // P
    _, causal_bd, strict_bd, diag = chunk_masks(P)
    eye = diag.astype(F32)

    beta_all, gcum_all, glast_all, gcum_t = [], [], [], []
    for b in range(nb):
        misc = misc_ref[b]
        beta_all.append(_sigmoid(misc))
        sp_in = misc + dtb_ref[...]
        softplus = jnp.maximum(sp_in, 0.0) + jnp.log1p(jnp.exp(-jnp.abs(sp_in)))
        g_all = -jnp.exp(alog_ref[...]) * softplus
        g1, g2, g3 = _split3(g_all)
        cum = _dot(cum_mat, g1) + _dot(cum_mat, g2) + _dot(cum_mat, g3)
        gcum_all.append(cum[:L])
        glast_all.append(cum[L:])
        gcum_t.append(cum[:L].T)

    def stage1(b, h):
        a_mats, attns = [], []
        lane = DN_HEADS + h
        gc_col = gcum_all[b][:, lane:lane + 1]
        gl_col = glast_all[b][:, lane:lane + 1]
        gc_row = gcum_t[b][lane:lane + 1, :]
        beta = beta_all[b][:, h:h + 1]
        qn = qkv_ref[b, :, h * D:(h + 1) * D]
        kn = qkv_ref[b, :, DN_WIDTH + h * D:DN_WIDTH + (h + 1) * D]
        v = qkv_ref[b, :, 2 * DN_WIDTH + h * D:2 * DN_WIDTH + (h + 1) * D]
        kb = kn * beta
        e_col = jnp.exp(gc_col)
        kn16 = kn.astype(BF16)
        kb16 = kb.astype(BF16)
        qn16 = qn.astype(BF16)
        for j in range(nblk):
            rows = slice(j * P, (j + 1) * P)
            decay = jnp.exp(jnp.where(causal_bd, gc_col[rows] - gc_row[:, rows], 0.0))
            gram = _dot_nt(jnp.concatenate([kb16[rows], qn16[rows]], axis=0), kn16[rows])
            a_mats.append(jnp.where(strict_bd, gram[:P] * decay, 0.0))
            attns.append(jnp.where(causal_bd, gram[P:] * decay, 0.0).astype(BF16))
        return dict(a=a_mats, attn=attns,
                    rhs=jnp.concatenate([v * beta, kb * e_col], axis=1).astype(BF16),
                    qd=(qn * e_col).astype(BF16),
                    kdt=(kn * jnp.exp(gl_col - gc_col)).T.astype(BF16),
                    gdec=jnp.exp(jnp.broadcast_to(gl_col, (L, D))))

    def stage2(group):
        a_mats = [a for ch in group for a in ch["a"]]
        xps = [(-a).astype(BF16) for a in a_mats]
        t_invs = [eye - a for a in a_mats]
        xps = [_dot(xp, xp).astype(BF16) for xp in xps]
        for _ in range(4):
            prods = [_dot(jnp.concatenate([t.astype(BF16), xp], axis=0), xp) for t, xp in zip(t_invs, xps)]
            t_invs = [t + pr[:P] for t, pr in zip(t_invs, prods)]
            xps = [pr[P:].astype(BF16) for pr in prods]
        t_invs = [t + _dot(t.astype(BF16), xp) for t, xp in zip(t_invs, xps)]
        for i, ch in enumerate(group):
            ch["uw"] = [_dot(t_invs[i * nblk + j].astype(BF16), ch["rhs"][j * P:(j + 1) * P])
                        for j in range(nblk)]

    chs = [stage1(b, h) for b, h in chains]
    stage2(chs)

    states = [state_sc[b, h] for b, h in chains]
    o_parts = [[] for _ in chains]
    zeros_c = jnp.zeros((C, D), BF16)
    for c in range(nchunk):
        r0 = c * C
        j, half = divmod(c, P // C)
        p0 = half * C
        rs = [_dot(jnp.concatenate([ch["uw"][j][p0:p0 + C, D:].astype(BF16), ch["qd"][r0:r0 + C]], axis=0),
                   states[i].astype(BF16)) for i, ch in enumerate(chs)]
        for i, ch in enumerate(chs):
            v_new = (ch["uw"][j][p0:p0 + C, :D] - rs[i][:C]).astype(BF16)
            v_blk = jnp.concatenate([zeros_c] * half + [v_new] + [zeros_c] * (P // C - 1 - half), axis=0)
            lhs = jnp.concatenate([ch["attn"][j][p0:p0 + C, :], ch["kdt"][:, j * P:(j + 1) * P]], axis=0)
            m2 = _dot(lhs, v_blk)
            o_parts[i].append(rs[i][C:] + m2[:C])
            states[i] = states[i] * ch["gdec"][r0:r0 + 1, :] + m2[C:]

    for i, (b, h) in enumerate(chains):
        state_sc[b, h] = states[i]
        o = jnp.concatenate(o_parts[i], axis=0)
        zh = z_ref[b, :, h * D:(h + 1) * D]
        y_ref[b, :, h * D:(h + 1) * D] = (_rms_rows(o, ng_ref[...]) * (zh * _sigmoid(zh))).astype(y_ref.dtype)


def _deltanet(qkv, z, misc, alog_l, dtb_l, norm_gain):
    b, s, _ = qkv.shape
    L = DN_TILE
    return pl.pallas_call(
        _dn_kernel,
        grid=(s // L,),
        in_specs=[pl.BlockSpec((b, L, 3 * DN_WIDTH), lambda j: (0, j, 0)),
                  pl.BlockSpec((b, L, DN_WIDTH), lambda j: (0, j, 0)),
                  pl.BlockSpec((b, L, LANES), lambda j: (0, j, 0)),
                  _const_spec((1, LANES)),
                  _const_spec((1, LANES)),
                  _const_spec((1, DN_HEAD_DIM))],
        out_specs=pl.BlockSpec((b, L, DN_WIDTH), lambda j: (0, j, 0)),
        out_shape=jax.ShapeDtypeStruct((b, s, DN_WIDTH), BF16),
        scratch_shapes=[pltpu.VMEM((b, DN_HEADS, DN_HEAD_DIM, DN_HEAD_DIM), F32)],
        compiler_params=_params("arbitrary"),
        name="deltanet",
    )(qkv, z, misc, alog_l, dtb_l, norm_gain)


def _rope_kernel(pos_ref, freq_ref, cos_ref, sina_ref, sinb_ref):
    tm = pos_ref.shape[1]
    half = tm // 2
    fold = LANES // 2
    lane = lax.broadcasted_iota(jnp.int32, (half, LANES), 1)
    upper = lane >= fold
    freq = freq_ref[...]
    freq2 = freq + pltpu.roll(freq, fold, 1)
    pos = jnp.where(upper, pos_ref[0, 0:half].astype(F32), pos_ref[0, half:tm].astype(F32))
    ang = pos * freq2
    cos2 = jnp.cos(ang)
    sin2 = jnp.sin(ang)
    pe1 = (lane >= PE1_LANE) & (lane < PE1_LANE + HALF_ROPE)
    pe2 = (lane >= PE2_LANE) & (lane < PE2_LANE + HALF_ROPE)
    for rows, cos, sin in ((slice(0, half), cos2, sin2),
                           (slice(half, tm), pltpu.roll(cos2, fold, 1), pltpu.roll(sin2, fold, 1))):
        cos_ref[0, rows] = jnp.where(lane < MLA_NOPE, 1.0, jnp.where(pe1 | pe2, cos, 0.0))
        sina_ref[0, rows] = jnp.where(pe2, sin, 0.0)
        sinb_ref[0, rows] = jnp.where(pe1, -sin, 0.0)


def _rope_tables(positions, freq_lanes):
    b, s = positions.shape
    tm = TM_PREP
    spec = pl.BlockSpec((1, tm, LANES), lambda i, j: (i, j, 0))
    return pl.pallas_call(
        _rope_kernel,
        grid=(b, s // tm),
        in_specs=[pl.BlockSpec((1, tm, 1), lambda i, j: (i, j, 0)), _const_spec((1, LANES))],
        out_specs=[spec, spec, spec],
        out_shape=[jax.ShapeDtypeStruct((b, s, LANES), F32)] * 3,
        compiler_params=_params("parallel", "parallel"),
        name="rope_tables",
    )(positions.reshape(b, s, 1), freq_lanes)


def _rope(x, cos, sina, sinb):
    return x * cos + pltpu.roll(x, 2 * HALF_ROPE, 1) * sina + pltpu.roll(x, LANES - 2 * HALF_ROPE, 1) * sinb


def _mla_prep_kernel(cq_ref, ckv_ref, misc_ref, cos_ref, sina_ref, sinb_ref,
                     qag_ref, kvag_ref, wq_ref, wk_ref, wvt_ref, qg_ref, kg_ref, kpg_ref,
                     q_ref, k_ref, vt_ref, qsq_ref, ksq_ref):
    tm = cq_ref.shape[1]
    ones8 = jnp.ones((SUBLANES, LANES), BF16)
    eye = (lax.broadcasted_iota(jnp.int32, (LANES, LANES), 0)
           == lax.broadcasted_iota(jnp.int32, (LANES, LANES), 1)).astype(BF16)

    def row_sq_norms(x16):
        xf = x16.astype(F32)
        return _dot_nt(ones8, (xf * xf).astype(BF16))

    cos = cos_ref[0]
    sina = sina_ref[0]
    sinb = sinb_ref[0]
    lane = lax.broadcasted_iota(jnp.int32, (tm, LANES), 1)
    is_nope = lane < MLA_NOPE

    cqn = _rms_rows(cq_ref[0], qag_ref[...]).astype(BF16)
    ckvn = _rms_rows(ckv_ref[0], kvag_ref[...]).astype(BF16)
    qf = _dot(cqn, wq_ref[...])
    kf = _dot(ckvn, wk_ref[...])
    vt = _dot_nt(wvt_ref[...], ckvn)
    vrow = lax.broadcasted_iota(jnp.int32, vt.shape, 0)
    ones_row = functools.reduce(jnp.logical_or, [vrow == h * VT_ROWS + MLA_V for h in range(MLA_HEADS)])
    vt_ref[0] = jnp.where(ones_row, 1.0, vt).astype(BF16)

    kp = jnp.where(is_nope, 0.0, misc_ref[0])
    kp_ms = jnp.sum(kp * kp, axis=-1, keepdims=True) * (1.0 / MLA_ROPE)
    kp = _rope(kp * lax.rsqrt(kp_ms + EPS) * kpg_ref[...], cos, sina, sinb)

    scale = MLA_QK_DIM ** -0.5 * math.log2(math.e)
    for h in range(MLA_HEADS):
        xq = qf[:, h * LANES:(h + 1) * LANES]
        sq = xq * xq
        ms_n = jnp.sum(jnp.where(is_nope, sq, 0.0), axis=-1, keepdims=True) * (1.0 / MLA_NOPE)
        ms_p = jnp.sum(jnp.where(is_nope, 0.0, sq), axis=-1, keepdims=True) * (1.0 / MLA_ROPE)
        inv = jnp.where(is_nope, lax.rsqrt(ms_n + EPS), lax.rsqrt(ms_p + EPS))
        qh = _rope(xq * inv * qg_ref[...], cos, sina, sinb) * scale
        qt = _dot_nt(eye, qh.astype(BF16))
        q_ref[0, h] = qt.astype(BF16)
        qsq_ref[0, h] = jnp.broadcast_to(jnp.sum(qt * qt, axis=0, keepdims=True), (SUBLANES, tm))

        xk = kf[:, h * LANES:(h + 1) * LANES]
        ms_k = jnp.sum(xk * xk, axis=-1, keepdims=True) * (1.0 / MLA_NOPE)
        kh = xk * lax.rsqrt(ms_k + EPS) * kg_ref[...] + kp
        k16 = kh.astype(BF16)
        k_ref[0, h] = k16
        ksq_ref[0, h] = row_sq_norms(k16)


def _mla_prep(cq, ckv, misc, cos, sina, sinb, qag, kvag, wq, wk, wvt, qg, kg, kpg):
    b, s, _ = cq.shape
    tm = TM_PREP
    hl = MLA_HEADS * LANES

    def tok(n):
        return pl.BlockSpec((1, tm, n), lambda i, j: (i, j, 0))

    return pl.pallas_call(
        _mla_prep_kernel,
        grid=(b, s // tm),
        in_specs=[tok(Q_LORA), tok(KV_LORA), tok(LANES), tok(LANES), tok(LANES), tok(LANES),
                  _const_spec((1, Q_LORA)), _const_spec((1, KV_LORA)),
                  _const_spec((Q_LORA, hl)), _const_spec((KV_LORA, hl)), _const_spec((MLA_HEADS * VT_ROWS, KV_LORA)),
                  _const_spec((1, LANES)), _const_spec((1, LANES)), _const_spec((1, LANES))],
        out_specs=[pl.BlockSpec((1, MLA_HEADS, LANES, tm), lambda i, j: (i, 0, 0, j)),
                   pl.BlockSpec((1, MLA_HEADS, tm, LANES), lambda i, j: (i, 0, j, 0)),
                   pl.BlockSpec((1, MLA_HEADS * VT_ROWS, tm), lambda i, j: (i, 0, j)),
                   pl.BlockSpec((1, MLA_HEADS, SUBLANES, tm), lambda i, j: (i, 0, 0, j)),
                   pl.BlockSpec((1, MLA_HEADS, SUBLANES, tm), lambda i, j: (i, 0, 0, j))],
        out_shape=[jax.ShapeDtypeStruct((b, MLA_HEADS, LANES, s), BF16),
                   jax.ShapeDtypeStruct((b, MLA_HEADS, s, LANES), BF16),
                   jax.ShapeDtypeStruct((b, MLA_HEADS * VT_ROWS, s), BF16),
                   jax.ShapeDtypeStruct((b, MLA_HEADS, SUBLANES, s), F32),
                   jax.ShapeDtypeStruct((b, MLA_HEADS, SUBLANES, s), F32)],
        compiler_params=_params("parallel", "parallel"),
        name="mla_prep",
    )(cq, ckv, misc, cos, sina, sinb, qag, kvag, wq, wk, wvt, qg, kg, kpg)


def _bounds_kernel(qsq_ref, ksq_ref, u_ref, flag_ref):
    worst = None
    for h in range(MLA_HEADS):
        kmax = jnp.max(ksq_ref[0, h], axis=-1, keepdims=True)
        u = jnp.sqrt(qsq_ref[0, h] * kmax) * SCORE_BOUND_SLACK
        u_ref[0, h] = u
        umax = jnp.max(u, axis=-1, keepdims=True)
        worst = umax if worst is None else jnp.maximum(worst, umax)
    flag_ref[0] = jnp.broadcast_to((worst <= SCORE_BOUND_LIMIT).astype(jnp.int32), flag_ref.shape[1:])


def _score_bounds(qsq, ksq):
    b, h, r, s = qsq.shape
    spec = pl.BlockSpec((1, h, r, s), lambda i: (i, 0, 0, 0))
    return pl.pallas_call(
        _bounds_kernel,
        grid=(b,),
        in_specs=[spec, spec],
        out_specs=[spec, pl.BlockSpec((1, SUBLANES, LANES), lambda i: (i, 0, 0))],
        out_shape=[jax.ShapeDtypeStruct((b, h, r, s), F32), jax.ShapeDtypeStruct((b, SUBLANES, LANES), jnp.int32)],
        compiler_params=_params("parallel"),
        name="score_bounds",
    )(qsq, ksq)


def _attn_block(q_ref, k_ref, vt_ref, u_ref, m_sc, acc_sc, visible, bounded):
    scores = [_dot(k_ref[0, 0], q_ref[0, 0])]
    for h in range(MLA_HEADS):
        if h + 1 < MLA_HEADS:
            scores.append(_dot(k_ref[0, h + 1], q_ref[0, h + 1]))
        st = scores[h]
        if visible is not None:
            st = jnp.where(visible, st, -jnp.inf)
        vt = vt_ref[0, h * VT_ROWS:(h + 1) * VT_ROWS, :]
        if bounded:
            pt = jnp.exp2(st - u_ref[0, h, 0:1, :])
            acc_sc[h] = acc_sc[h] + _dot(vt, pt.astype(BF16))
        else:
            m_prev = m_sc[h]
            m_new = jnp.maximum(m_prev, jnp.max(st, axis=0, keepdims=True))
            alpha = jnp.exp2(m_prev - m_new)
            pt = jnp.exp2(st - m_new)
            acc_sc[h] = alpha * acc_sc[h] + _dot(vt, pt.astype(BF16))
            m_sc[h] = m_new


def _attn_diag_block(q_ref, k_ref, vt_ref, u_ref, acc_sc):
    hk = TK // 2
    r = lax.broadcasted_iota(jnp.int32, (hk, TQ), 0)
    c = lax.broadcasted_iota(jnp.int32, (hk, TQ), 1)
    vis_old = r <= c
    vis_new = vis_old[:, :hk]

    def scores(h):
        return (_dot(k_ref[0, h, 0:hk, :], q_ref[0, h]),
                _dot(k_ref[0, h, hk:, :], q_ref[0, h, :, hk:]))

    nxt = scores(0)
    for h in range(MLA_HEADS):
        s_old, s_new = nxt
        if h + 1 < MLA_HEADS:
            nxt = scores(h + 1)
        u = u_ref[0, h, 0:1, :]
        p_old = jnp.exp2(jnp.where(vis_old, s_old, -jnp.inf) - u)
        p_new = jnp.exp2(jnp.where(vis_new, s_new, -jnp.inf) - u[:, hk:])
        vt = vt_ref[0, h * VT_ROWS:(h + 1) * VT_ROWS, :]
        acc_sc[h] = acc_sc[h] + _dot(vt[:, :hk], p_old.astype(BF16))
        acc_sc[h, :, hk:] = acc_sc[h, :, hk:] + _dot(vt[:, hk:], p_new.astype(BF16))


def _attn_kernel(qi_ref, kj_ref, last_ref, flag_ref, q_ref, k_ref, vt_ref, u_ref, o_ref, m_sc, acc_sc):
    p = pl.program_id(1)
    qi = qi_ref[p]
    kj = kj_ref[p]
    bounded = flag_ref[pl.program_id(0)] == 1

    @pl.when(kj == 0)
    def _():
        m_sc[...] = jnp.full_like(m_sc, -jnp.inf)
        acc_sc[...] = jnp.zeros_like(acc_sc)

    all_visible = kj * TK + (TK - 1) <= qi * TQ

    def run(visible_fn, use_bound):
        def body():
            _attn_block(q_ref, k_ref, vt_ref, u_ref, m_sc, acc_sc, visible_fn(), use_bound)
        return body

    def causal_mask():
        kpos = kj * TK + lax.broadcasted_iota(jnp.int32, (TK, TQ), 0)
        qpos = qi * TQ + lax.broadcasted_iota(jnp.int32, (TK, TQ), 1)
        return kpos <= qpos

    partly = jnp.logical_not(all_visible)
    unbounded = jnp.logical_not(bounded)
    pl.when(all_visible & bounded)(run(lambda: None, True))
    if TQ == TK:
        pl.when(partly & bounded)(lambda: _attn_diag_block(q_ref, k_ref, vt_ref, u_ref, acc_sc))
    else:
        pl.when(partly & bounded)(run(causal_mask, True))
    pl.when(all_visible & unbounded)(run(lambda: None, False))
    pl.when(partly & unbounded)(run(causal_mask, False))

    @pl.when(last_ref[p] == 1)
    def _():
        out_t = jnp.concatenate([acc_sc[h, :MLA_V] / acc_sc[h, MLA_V:MLA_V + 1] for h in range(MLA_HEADS)], axis=0)
        o_ref[0] = out_t.T.astype(o_ref.dtype)


def _attention(q, k, vt, u, flag):
    b, _, s, _ = k.shape
    nq = s // TQ
    pairs = [(i, j) for i in range(nq) for j in range((i * TQ + TQ - 1) // TK + 1)]
    qi = jnp.asarray(np.array([p[0] for p in pairs], np.int32))
    kj = jnp.asarray(np.array([p[1] for p in pairs], np.int32))
    last = jnp.asarray(np.array([int(p[1] == (p[0] * TQ + TQ - 1) // TK) for p in pairs], np.int32))
    grid_spec = pltpu.PrefetchScalarGridSpec(
        num_scalar_prefetch=4,
        grid=(b, len(pairs)),
        in_specs=[pl.BlockSpec((1, MLA_HEADS, LANES, TQ), lambda i, p, qi, kj, *_: (i, 0, 0, qi[p])),
                  pl.BlockSpec((1, MLA_HEADS, TK, LANES), lambda i, p, qi, kj, *_: (i, 0, kj[p], 0)),
                  pl.BlockSpec((1, MLA_HEADS * VT_ROWS, TK), lambda i, p, qi, kj, *_: (i, 0, kj[p])),
                  pl.BlockSpec((1, MLA_HEADS, SUBLANES, TQ), lambda i, p, qi, kj, *_: (i, 0, 0, qi[p]))],
        out_specs=pl.BlockSpec((1, TQ, MLA_WIDTH), lambda i, p, qi, kj, *_: (i, qi[p], 0)),
        scratch_shapes=[pltpu.VMEM((MLA_HEADS, 1, TQ), F32),
                        pltpu.VMEM((MLA_HEADS, VT_ROWS, TQ), F32)],
    )
    return pl.pallas_call(
        _attn_kernel,
        grid_spec=grid_spec,
        out_shape=jax.ShapeDtypeStruct((b, s, MLA_WIDTH), BF16),
        compiler_params=_params("parallel", "arbitrary"),
        name="mla_attention",
    )(qi, kj, last, flag, q, k, vt, u)


POOL_HALO = 16


def _post_kernel(x_ref, xa_ref, yb_ref, yc_ref, wp_ref, ps_ref, wo_ref, g_ref, wu_ref, wd_ref, o_ref,
                 halo_sc, ext_sc):
    ts = x_ref.shape[1]
    s_idx = pl.program_id(1)

    @pl.when(s_idx == 0)
    def _():
        halo_sc[...] = jnp.zeros_like(halo_sc)

    xa = xa_ref[0]
    ext_sc[0:POOL_HALO, :] = halo_sc[...]
    ext_sc[POOL_HALO:, :] = xa
    halo_sc[...] = xa_ref[0, ts - POOL_HALO:ts, :]

    assert POOL_WINDOWS == tuple(2 << g for g in range(POOL_GROUPS)) and POOL_WINDOWS[-1] <= POOL_HALO
    e = ext_sc[...]
    sums = []
    for g in range(POOL_GROUPS):
        e = e + pltpu.roll(e, 1 << g, 0)
        sums.append(e[POOL_HALO:])
    lane = lax.broadcasted_iota(jnp.int32, (ts, POOL_WIDTH), 1)
    grp = lane >> (POOL_GROUP_DIM.bit_length() - 1)
    win = jnp.left_shift(2, grp)
    t = s_idx * ts + lax.broadcasted_iota(jnp.int32, (ts, POOL_WIDTH), 0)
    count = jnp.minimum(t + 1, win).astype(F32)
    pooled = sums[POOL_GROUPS - 1]
    for g in range(POOL_GROUPS - 2, -1, -1):
        pooled = jnp.where(grp == g, sums[g], pooled)
    ya = _dot((pooled / count - xa).astype(BF16), wp_ref[...]) * ps_ref[...]

    acc = x_ref[0] + _dot(ya.astype(BF16), wo_ref[0:POOL_WIDTH, :])
    acc = acc + _dot(yb_ref[0].astype(BF16), wo_ref[POOL_WIDTH:POOL_WIDTH + DN_WIDTH, :])
    x1 = acc + _dot(yc_ref[0].astype(BF16), wo_ref[POOL_WIDTH + DN_WIDTH:, :])

    h = _rms_rows(x1, g_ref[...]).astype(BF16)
    u = jnp.maximum(_dot(h, wu_ref[...]), 0.0)
    o_ref[0] = x1 + _dot((u * u).astype(BF16), wd_ref[...])


def _post(x, xa, yb, yc, wp_bd, pool_scale, w_out, gain, w_up, w_down):
    b, s, _ = x.shape
    ts = TM_PROJ

    def tok(n):
        return pl.BlockSpec((1, ts, n), lambda i, j: (i, j, 0))

    return pl.pallas_call(
        _post_kernel,
        grid=(b, s // ts),
        in_specs=[tok(D_MODEL), tok(POOL_WIDTH), tok(DN_WIDTH), tok(MLA_WIDTH),
                  _const_spec((POOL_WIDTH, POOL_WIDTH)), _const_spec((1, POOL_WIDTH)),
                  _const_spec((D_MODEL, D_MODEL)), _const_spec((1, D_MODEL)),
                  _const_spec((D_MODEL, D_FF)), _const_spec((D_FF, D_MODEL))],
        out_specs=tok(D_MODEL),
        out_shape=jax.ShapeDtypeStruct((b, s, D_MODEL), F32),
        scratch_shapes=[pltpu.VMEM((POOL_HALO, POOL_WIDTH), F32),
                        pltpu.VMEM((ts + POOL_HALO, POOL_WIDTH), F32)],
        compiler_params=_params("parallel", "arbitrary"),
        name="outproj_pool_mlp",
    )(x, xa, yb, yc, wp_bd, pool_scale, w_out, gain, w_up, w_down)


def _head_block(nope, pe):
    z16 = jnp.zeros(pe.shape[:-1] + (HALF_ROPE,), pe.dtype)
    return jnp.concatenate([nope, pe[..., :HALF_ROPE], z16, pe[..., HALF_ROPE:], z16], axis=-1)


def _prep_layer(w_in, pool_w, pool_scale, dn_conv, dn_a_log, dn_dt_bias, dn_norm,
                q_a_norm, w_q_b, kv_a_norm, w_kv_b, q_norm, k_norm, w_out, w_up, w_down):
    offs = np.cumsum((0,) + IN_SPLITS)
    seg = [w_in[:, offs[i]:offs[i + 1]] for i in range(len(IN_SPLITS))]
    w_xa, w_qkv, w_z, w_b, w_a, w_cq, w_ckv, w_kpe = seg
    d = w_in.shape[0]
    z64 = jnp.zeros((d, MLA_NOPE), w_in.dtype)
    w_misc = jnp.concatenate([w_b, w_a, jnp.zeros((d, MLA_NOPE - 2 * DN_HEADS), w_in.dtype),
                              _head_block(z64, w_kpe)[:, MLA_NOPE:]], axis=1)
    w_cat = jnp.concatenate([w_qkv, w_xa, w_z, w_cq, w_ckv, w_misc], axis=1).astype(BF16)

    lane_pad = jnp.zeros((LANES - 2 * DN_HEADS,), F32)
    alog_l = jnp.concatenate([jnp.zeros((DN_HEADS,), F32), dn_a_log, lane_pad])[None]
    dtb_l = jnp.concatenate([jnp.zeros((DN_HEADS,), F32), dn_dt_bias, lane_pad])[None]

    wq = w_q_b.reshape(Q_LORA, MLA_HEADS, MLA_QK_DIM)
    wq = _head_block(wq[..., :MLA_NOPE], wq[..., MLA_NOPE:]).reshape(Q_LORA, MLA_HEADS * LANES).astype(BF16)
    wkv = w_kv_b.reshape(KV_LORA, MLA_HEADS, MLA_NOPE + MLA_V)
    wk = _head_block(wkv[..., :MLA_NOPE], jnp.zeros((KV_LORA, MLA_HEADS, MLA_ROPE), F32))
    wk = wk.reshape(KV_LORA, MLA_HEADS * LANES).astype(BF16)
    wv = jnp.concatenate([wkv[..., MLA_NOPE:], jnp.zeros((KV_LORA, MLA_HEADS, VT_ROWS - MLA_V), F32)], axis=-1)
    wvt = wv.reshape(KV_LORA, MLA_HEADS * VT_ROWS).T.astype(BF16)
    qg = _head_block(q_norm[:MLA_NOPE], q_norm[MLA_NOPE:])[None]
    kg = _head_block(k_norm[:MLA_NOPE], jnp.zeros((MLA_ROPE,), F32))[None]
    kpg = _head_block(jnp.zeros((MLA_NOPE,), F32), k_norm[MLA_NOPE:])[None]

    wp_bd = jax.scipy.linalg.block_diag(*[pool_w[g] for g in range(POOL_GROUPS)]).astype(BF16)
    return dict(w_cat=w_cat, alog_l=alog_l, dtb_l=dtb_l, conv_w=dn_conv, dn_norm=dn_norm[None],
                qag=q_a_norm[None], kvag=kv_a_norm[None], wq=wq, wk=wk, wvt=wvt, qg=qg, kg=kg, kpg=kpg,
                wp_bd=wp_bd, pool_scale=pool_scale[None], w_out=w_out.astype(BF16),
                w_up=w_up.astype(BF16), w_down=w_down.astype(BF16))


def kernel(x, positions, attn_norm, w_in, pool_w, pool_scale, dn_conv, dn_a_log, dn_dt_bias, dn_norm,
           mla_q_a_norm, mla_w_q_b, mla_kv_a_norm, mla_w_kv_b, mla_q_norm, mla_k_norm,
           w_out, mlp_norm, w_up, w_down):
    b, s, d = x.shape
    depth = w_in.shape[0]
    inv_freq = ROPE_THETA ** (-jnp.arange(0, MLA_ROPE, 2, dtype=F32) / MLA_ROPE)
    freq_lanes = _head_block(jnp.zeros((MLA_NOPE,), F32), jnp.concatenate([inv_freq, inv_freq]))[None]
    cos, sina, sinb = _rope_tables(positions, freq_lanes)

    for l in range(depth):
        p = _prep_layer(w_in[l], pool_w[l], pool_scale[l], dn_conv[l], dn_a_log[l], dn_dt_bias[l], dn_norm[l],
                        mla_q_a_norm[l], mla_w_q_b[l], mla_kv_a_norm[l], mla_w_kv_b[l], mla_q_norm[l],
                        mla_k_norm[l], w_out[l], w_up[l], w_down[l])
        qkv, xa, z, cq, ckv, misc = _inproj(x, attn_norm[l][None], p["w_cat"], p["conv_w"])
        y_b = _deltanet(qkv, z, misc, p["alog_l"], p["dtb_l"], p["dn_norm"])
        q, k, vt, qsq, ksq = _mla_prep(cq, ckv, misc, cos, sina, sinb, p["qag"], p["kvag"], p["wq"], p["wk"],
                                       p["wvt"], p["qg"], p["kg"], p["kpg"])
        u, flag = _score_bounds(qsq, ksq)
        y_c = _attention(q, k, vt, u, flag[:, 0, 0])
        x = _post(x, xa, y_b, y_c, p["wp_bd"], p["pool_scale"], p["w_out"], mlp_norm[l][None], p["w_up"], p["w_down"])
    return x
```

```python
import functools
import math

import jax
import jax.numpy as jnp
import numpy as np
from jax import lax
from jax.experimental import pallas as pl
from jax.experimental.pallas import tpu as pltpu

F32 = jnp.float32
BF16 = jnp.bfloat16

D_MODEL = 1024
POOL_GROUPS = 4
POOL_GROUP_DIM = 64
POOL_WIDTH = POOL_GROUPS * POOL_GROUP_DIM
POOL_WINDOWS = (2, 4, 8, 16)
DN_HEADS = 4
DN_HEAD_DIM = 128
DN_WIDTH = DN_HEADS * DN_HEAD_DIM
DN_CONV = 4
DN_CHUNK = 64
MLA_HEADS = 4
MLA_NOPE = 64
MLA_ROPE = 32
MLA_QK_DIM = MLA_NOPE + MLA_ROPE
MLA_V = 64
MLA_WIDTH = MLA_HEADS * MLA_V
Q_LORA = 256
KV_LORA = 128
ROPE_THETA = 10000.0
D_FF = 4 * D_MODEL
EPS = 1e-6
IN_SPLITS = (POOL_WIDTH, 3 * DN_WIDTH, DN_WIDTH, DN_HEADS, DN_HEADS, Q_LORA, KV_LORA, MLA_ROPE)

LANES = 128
SUBLANES = 8
MXU_COLS = 256
VMEM_LIMIT_BYTES = 56 * 1024 * 1024

TM_INPROJ = 256
TM_PROJ = 512
TM_PREP = 1024
DN_TILE = 256
DN_PAIR = 2 * DN_CHUNK
TQ = 1024
TK = 1024
HALF_ROPE = MLA_ROPE // 2
VT_ROWS = MLA_V + 16
SCORE_BOUND_SLACK = 1.02
SCORE_BOUND_LIMIT = 30.0
PE1_LANE = MLA_NOPE
PE2_LANE = MLA_NOPE + 2 * HALF_ROPE


def _dot(a, b):
    return jnp.dot(a, b, preferred_element_type=F32)


def _dot_nt(a, b):
    return lax.dot_general(a, b, (((1,), (1,)), ((), ())), preferred_element_type=F32)


def _rms_rows(x, gain):
    return x * lax.rsqrt(jnp.mean(x * x, axis=-1, keepdims=True) + EPS) * gain


def _split3(x):
    x1 = x.astype(BF16)
    r1 = x - x1.astype(F32)
    x2 = r1.astype(BF16)
    r2 = r1 - x2.astype(F32)
    return x1, x2, r2.astype(BF16)


def _sigmoid(x):
    return 1.0 / (1.0 + jnp.exp(-x))


def _params(*sem):
    return pltpu.CompilerParams(dimension_semantics=sem, vmem_limit_bytes=VMEM_LIMIT_BYTES)


def _const_spec(shape):
    nd = len(shape)
    return pl.BlockSpec(shape, lambda *_: (0,) * nd, pipeline_mode=pl.Buffered(1))


IN_SEGS = (3 * DN_WIDTH, POOL_WIDTH, DN_WIDTH, Q_LORA, KV_LORA, LANES)
CONV_HALO = SUBLANES


def _inproj_kernel(x_ref, g_ref, w_ref, cw_ref, qkv_ref, *rest):
    o_refs, (halo_sc, ext_sc) = rest[:-2], rest[-2:]
    tm = x_ref.shape[1]
    nqkv = 3 * DN_WIDTH
    D = DN_HEAD_DIM

    @pl.when(pl.program_id(1) == 0)
    def _():
        halo_sc[...] = jnp.zeros_like(halo_sc)

    h = _rms_rows(x_ref[0], g_ref[...]).astype(BF16)
    ext_sc[0:CONV_HALO, :] = halo_sc[...]

    def conv_cols(c0):
        e = ext_sc[:, c0:c0 + D]
        cw = cw_ref[:, c0:c0 + D]
        e1 = pltpu.roll(e, 1, 0)
        near = cw[3:4] * e + cw[2:3] * e1
        far = cw[1:2] * e + cw[0:1] * e1
        acc = (near + pltpu.roll(far, 2, 0))[CONV_HALO:]
        y = acc * _sigmoid(acc)
        if c0 < 2 * DN_WIDTH:
            y = y * lax.rsqrt(jnp.sum(y * y, axis=-1, keepdims=True) + EPS)
            if c0 < DN_WIDTH:
                y = y * (D ** -0.5)
        qkv_ref[0, :, c0:c0 + D] = y

    pieces = []
    off = nqkv
    for o_ref in o_refs:
        n = o_ref.shape[-1]
        pieces += [(o_ref, c, off + c, LANES) for c in range(0, n, LANES)]
        off += n
    per_dot = MXU_COLS // LANES
    others = [pieces[i:i + per_dot] for i in range(0, len(pieces), per_dot)]
    def qkv_cols(c0):
        ext_sc[CONV_HALO:, c0:c0 + MXU_COLS] = _dot(h, w_ref[:, c0:c0 + MXU_COLS])

    nsteps = nqkv // MXU_COLS
    qkv_cols(0)
    for i in range(nsteps):
        if i + 1 < nsteps:
            qkv_cols((i + 1) * MXU_COLS)
        for c in range(i * MXU_COLS, (i + 1) * MXU_COLS, D):
            conv_cols(c)
        for group in others[i::nsteps]:
            w0 = group[0][2]
            res = _dot(h, w_ref[:, w0:w0 + LANES * len(group)])
            for o_ref, c, woff, n in group:
                o_ref[0, :, c:c + n] = res[:, woff - w0:woff - w0 + n]
    halo_sc[...] = ext_sc[tm:tm + CONV_HALO, :]


def _inproj(x, gain, w_cat, conv_w):
    b, s, _ = x.shape
    tm = TM_INPROJ
    n_all = sum(IN_SEGS)
    return pl.pallas_call(
        _inproj_kernel,
        grid=(b, s // tm),
        in_specs=[pl.BlockSpec((1, tm, D_MODEL), lambda i, j: (i, j, 0)),
                  _const_spec((1, D_MODEL)),
                  _const_spec((D_MODEL, n_all)),
                  _const_spec((DN_CONV, 3 * DN_WIDTH))],
        out_specs=[pl.BlockSpec((1, tm, n), lambda i, j: (i, j, 0)) for n in IN_SEGS],
        out_shape=[jax.ShapeDtypeStruct((b, s, n), F32) for n in IN_SEGS],
        scratch_shapes=[pltpu.VMEM((CONV_HALO, 3 * DN_WIDTH), F32),
                        pltpu.VMEM((tm + CONV_HALO, 3 * DN_WIDTH), F32)],
        compiler_params=_params("parallel", "arbitrary"),
        name="inproj",
    )(x, gain, w_cat, conv_w)


def _dn_kernel(qkv_ref, z_ref, misc_ref, alog_ref, dtb_ref, ng_ref, y_ref, state_sc):
    nb = qkv_ref.shape[0]
    L = DN_TILE
    C = DN_CHUNK
    nchunk = L // C
    D = DN_HEAD_DIM
    s_idx = pl.program_id(0)
    chains = [(b, h) for b in range(nb) for h in range(DN_HEADS)]

    @pl.when(s_idx == 0)
    def _():
        state_sc[...] = jnp.zeros_like(state_sc)

    def chunk_masks(n):
        ri = lax.broadcasted_iota(jnp.int32, (n, n), 0)
        ci = lax.broadcasted_iota(jnp.int32, (n, n), 1)
        shift = DN_CHUNK.bit_length() - 1
        same = (ri >> shift) == (ci >> shift)
        return same, same & (ci <= ri), same & (ci < ri), ri == ci

    same_l, causal_l, _, _ = chunk_masks(L)
    cum_mat = jnp.concatenate([causal_l.astype(BF16), same_l.astype(BF16)], axis=0)
    P = DN_PAIR
    nblk = L // P
    _, causal_bd, strict_bd, diag = chunk_masks(P)
    eye = diag.astype(F32)

    beta_all, gcum_all, glast_all, gcum_t = [], [], [], []
    for b in range(nb):
        misc = misc_ref[b]
        beta_all.append(_sigmoid(misc))
        sp_in = misc + dtb_ref[...]
        softplus = jnp.maximum(sp_in, 0.0) + jnp.log1p(jnp.exp(-jnp.abs(sp_in)))
        g_all = -jnp.exp(alog_ref[...]) * softplus
        g1, g2, g3 = _split3(g_all)
        cum = _dot(cum_mat, g1) + _dot(cum_mat, g2) + _dot(cum_mat, g3)
        gcum_all.append(cum[:L])
        glast_all.append(cum[L:])
        gcum_t.append(cum[:L].T)

    def stage1(b, h):
        a_mats, attns = [], []
        lane = DN_HEADS + h
        gc_col = gcum_all[b][:, lane:lane + 1]
        gl_col = glast_all[b][:, lane:lane + 1]
        gc_row = gcum_t[b][lane:lane + 1, :]
        beta = beta_all[b][:, h:h + 1]
        qn = qkv_ref[b, :, h * D:(h + 1) * D]
        kn = qkv_ref[b, :, DN_WIDTH + h * D:DN_WIDTH + (h + 1) * D]
        v = qkv_ref[b, :, 2 * DN_WIDTH + h * D:2 * DN_WIDTH + (h + 1) * D]
        kb = kn * beta
        e_col = jnp.exp(gc_col)
        kn16 = kn.astype(BF16)
        kb16 = kb.astype(BF16)
        qn16 = qn.astype(BF16)
        for j in range(nblk):
            rows = slice(j * P, (j + 1) * P)
            decay = jnp.exp(jnp.where(causal_bd, gc_col[rows] - gc_row[:, rows], 0.0))
            gram = _dot_nt(jnp.concatenate([kb16[rows], qn16[rows]], axis=0), kn16[rows])
            a_mats.append(jnp.where(strict_bd, gram[:P] * decay, 0.0))
            attns.append(jnp.where(causal_bd, gram[P:] * decay, 0.0).astype(BF16))
        return dict(a=a_mats, attn=attns,
                    rhs=jnp.concatenate([v * beta, kb * e_col], axis=1).astype(BF16),
                    qd=(qn * e_col).astype(BF16),
                    kdt=(kn * jnp.exp(gl_col - gc_col)).T.astype(BF16),
                    gdec=jnp.exp(jnp.broadcast_to(gl_col, (L, D))))

    def stage2(group):
        a_mats = [a for ch in group for a in ch["a"]]
        xps = [(-a).astype(BF16) for a in a_mats]
        t_invs = [eye - a for a in a_mats]
        xps = [_dot(xp, xp).astype(BF16) for xp in xps]
        for _ in range(4):
            prods = [_dot(jnp.concatenate([t.astype(BF16), xp], axis=0), xp) for t, xp in zip(t_invs, xps)]
            t_invs = [t + pr[:P] for t, pr in zip(t_invs, prods)]
            xps = [pr[P:].astype(BF16) for pr in prods]
        t_invs = [t + _dot(t.astype(BF16), xp) for t, xp in zip(t_invs, xps)]
        for i, ch in enumerate(group):
            ch["uw"] = [_dot(t_invs[i * nblk + j].astype(BF16), ch["rhs"][j * P:(j + 1) * P])
                        for j in range(nblk)]

    chs = [stage1(b, h) for b, h in chains]
    stage2(chs)

    states = [state_sc[b, h] for b, h in chains]
    o_parts = [[] for _ in chains]
    zeros_c = jnp.zeros((C, D), BF16)
    for c in range(nchunk):
        r0 = c * C
        j, half = divmod(c, P // C)
        p0 = half * C
        rs = [_dot(jnp.concatenate([ch["uw"][j][p0:p0 + C, D:].astype(BF16), ch["qd"][r0:r0 + C]], axis=0),
                   states[i].astype(BF16)) for i, ch in enumerate(chs)]
        for i, ch in enumerate(chs):
            v_new = (ch["uw"][j][p0:p0 + C, :D] - rs[i][:C]).astype(BF16)
            v_blk = jnp.concatenate([zeros_c] * half + [v_new] + [zeros_c] * (P // C - 1 - half), axis=0)
            lhs = jnp.concatenate([ch["attn"][j][p0:p0 + C, :], ch["kdt"][:, j * P:(j + 1) * P]], axis=0)
            m2 = _dot(lhs, v_blk)
            o_parts[i].append(rs[i][C:] + m2[:C])
            states[i] = states[i] * ch["gdec"][r0:r0 + 1, :] + m2[C:]

    for i, (b, h) in enumerate(chains):
        state_sc[b, h] = states[i]
        o = jnp.concatenate(o_parts[i], axis=0)
        zh = z_ref[b, :, h * D:(h + 1) * D]
        y_ref[b, :, h * D:(h + 1) * D] = (_rms_rows(o, ng_ref[...]) * (zh * _sigmoid(zh))).astype(y_ref.dtype)


def _deltanet(qkv, z, misc, alog_l, dtb_l, norm_gain):
    b, s, _ = qkv.shape
    L = DN_TILE
    return pl.pallas_call(
        _dn_kernel,
        grid=(s // L,),
        in_specs=[pl.BlockSpec((b, L, 3 * DN_WIDTH), lambda j: (0, j, 0)),
                  pl.BlockSpec((b, L, DN_WIDTH), lambda j: (0, j, 0)),
                  pl.BlockSpec((b, L, LANES), lambda j: (0, j, 0)),
                  _const_spec((1, LANES)),
                  _const_spec((1, LANES)),
                  _const_spec((1, DN_HEAD_DIM))],
        out_specs=pl.BlockSpec((b, L, DN_WIDTH), lambda j: (0, j, 0)),
        out_shape=jax.ShapeDtypeStruct((b, s, DN_WIDTH), BF16),
        scratch_shapes=[pltpu.VMEM((b, DN_HEADS, DN_HEAD_DIM, DN_HEAD_DIM), F32)],
        compiler_params=_params("arbitrary"),
        name="deltanet",
    )(qkv, z, misc, alog_l, dtb_l, norm_gain)


def _rope_kernel(pos_ref, freq_ref, cos_ref, sina_ref, sinb_ref):
    tm = pos_ref.shape[1]
    half = tm // 2
    fold = LANES // 2
    lane = lax.broadcasted_iota(jnp.int32, (half, LANES), 1)
    upper = lane >= fold
    freq = freq_ref[...]
    freq2 = freq + pltpu.roll(freq, fold, 1)
    pos = jnp.where(upper, pos_ref[0, 0:half].astype(F32), pos_ref[0, half:tm].astype(F32))
    ang = pos * freq2
    cos2 = jnp.cos(ang)
    sin2 = jnp.sin(ang)
    pe1 = (lane >= PE1_LANE) & (lane < PE1_LANE + HALF_ROPE)
    pe2 = (lane >= PE2_LANE) & (lane < PE2_LANE + HALF_ROPE)
    for rows, cos, sin in ((slice(0, half), cos2, sin2),
                           (slice(half, tm), pltpu.roll(cos2, fold, 1), pltpu.roll(sin2, fold, 1))):
        cos_ref[0, rows] = jnp.where(lane < MLA_NOPE, 1.0, jnp.where(pe1 | pe2, cos, 0.0))
        sina_ref[0, rows] = jnp.where(pe2, sin, 0.0)
        sinb_ref[0, rows] = jnp.where(pe1, -sin, 0.0)


def _rope_tables(positions, freq_lanes):
    b, s = positions.shape
    tm = TM_PREP
    spec = pl.BlockSpec((1, tm, LANES), lambda i, j: (i, j, 0))
    return pl.pallas_call(
        _rope_kernel,
        grid=(b, s // tm),
        in_specs=[pl.BlockSpec((1, tm, 1), lambda i, j: (i, j, 0)), _const_spec((1, LANES))],
        out_specs=[spec, spec, spec],
        out_shape=[jax.ShapeDtypeStruct((b, s, LANES), F32)] * 3,
        compiler_params=_params("parallel", "parallel"),
        name="rope_tables",
    )(positions.reshape(b, s, 1), freq_lanes)


def _rope(x, cos, sina, sinb):
    return x * cos + pltpu.roll(x, 2 * HALF_ROPE, 1) * sina + pltpu.roll(x, LANES - 2 * HALF_ROPE, 1) * sinb


def _mla_prep_kernel(cq_ref, ckv_ref, misc_ref, cos_ref, sina_ref, sinb_ref,
                     qag_ref, kvag_ref, wq_ref, wk_ref, wvt_ref, qg_ref, kg_ref, kpg_ref,
                     q_ref, k_ref, vt_ref, qsq_ref, ksq_ref):
    tm = cq_ref.shape[1]
    ones8 = jnp.ones((SUBLANES, LANES), BF16)
    eye = (lax.broadcasted_iota(jnp.int32, (LANES, LANES), 0)
           == lax.broadcasted_iota(jnp.int32, (LANES, LANES), 1)).astype(BF16)

    def row_sq_norms(x16):
        xf = x16.astype(F32)
        return _dot_nt(ones8, (xf * xf).astype(BF16))

    cos = cos_ref[0]
    sina = sina_ref[0]
    sinb = sinb_ref[0]
    lane = lax.broadcasted_iota(jnp.int32, (tm, LANES), 1)
    is_nope = lane < MLA_NOPE

    cqn = _rms_rows(cq_ref[0], qag_ref[...]).astype(BF16)
    ckvn = _rms_rows(ckv_ref[0], kvag_ref[...]).astype(BF16)
    qf = _dot(cqn, wq_ref[...])
    kf = _dot(ckvn, wk_ref[...])
    vt = _dot_nt(wvt_ref[...], ckvn)
    vrow = lax.broadcasted_iota(jnp.int32, vt.shape, 0)
    ones_row = functools.reduce(jnp.logical_or, [vrow == h * VT_ROWS + MLA_V for h in range(MLA_HEADS)])
    vt_ref[0] = jnp.where(ones_row, 1.0, vt).astype(BF16)

    kp = jnp.where(is_nope, 0.0, misc_ref[0])
    kp_ms = jnp.sum(kp * kp, axis=-1, keepdims=True) * (1.0 / MLA_ROPE)
    kp = _rope(kp * lax.rsqrt(kp_ms + EPS) * kpg_ref[...], cos, sina, sinb)

    scale = MLA_QK_DIM ** -0.5 * math.log2(math.e)
    for h in range(MLA_HEADS):
        xq = qf[:, h * LANES:(h + 1) * LANES]
        sq = xq * xq
        ms_n = jnp.sum(jnp.where(is_nope, sq, 0.0), axis=-1, keepdims=True) * (1.0 / MLA_NOPE)
        ms_p = jnp.sum(jnp.where(is_nope, 0.0, sq), axis=-1, keepdims=True) * (1.0 / MLA_ROPE)
        inv = jnp.where(is_nope, lax.rsqrt(ms_n + EPS), lax.rsqrt(ms_p + EPS))
        qh = _rope(xq * inv * qg_ref[...], cos, sina, sinb) * scale
        qt = _dot_nt(eye, qh.astype(BF16))
        q_ref[0, h] = qt.astype(BF16)
        qsq_ref[0, h] = jnp.broadcast_to(jnp.sum(qt * qt, axis=0, keepdims=True), (SUBLANES, tm))

        xk = kf[:, h * LANES:(h + 1) * LANES]
        ms_k = jnp.sum(xk * xk, axis=-1, keepdims=True) * (1.0 / MLA_NOPE)
        kh = xk * lax.rsqrt(ms_k + EPS) * kg_ref[...] + kp
        k16 = kh.astype(BF16)
        k_ref[0, h] = k16
        ksq_ref[0, h] = row_sq_norms(k16)


def _mla_prep(cq, ckv, misc, cos, sina, sinb, qag, kvag, wq, wk, wvt, qg, kg, kpg):
    b, s, _ = cq.shape
    tm = TM_PREP
    hl = MLA_HEADS * LANES

    def tok(n):
        return pl.BlockSpec((1, tm, n), lambda i, j: (i, j, 0))

    return pl.pallas_call(
        _mla_prep_kernel,
        grid=(b, s // tm),
        in_specs=[tok(Q_LORA), tok(KV_LORA), tok(LANES), tok(LANES), tok(LANES), tok(LANES),
                  _const_spec((1, Q_LORA)), _const_spec((1, KV_LORA)),
                  _const_spec((Q_LORA, hl)), _const_spec((KV_LORA, hl)), _const_spec((MLA_HEADS * VT_ROWS, KV_LORA)),
                  _const_spec((1, LANES)), _const_spec((1, LANES)), _const_spec((1, LANES))],
        out_specs=[pl.BlockSpec((1, MLA_HEADS, LANES, tm), lambda i, j: (i, 0, 0, j)),
                   pl.BlockSpec((1, MLA_HEADS, tm, LANES), lambda i, j: (i, 0, j, 0)),
                   pl.BlockSpec((1, MLA_HEADS * VT_ROWS, tm), lambda i, j: (i, 0, j)),
                   pl.BlockSpec((1, MLA_HEADS, SUBLANES, tm), lambda i, j: (i, 0, 0, j)),
                   pl.BlockSpec((1, MLA_HEADS, SUBLANES, tm), lambda i, j: (i, 0, 0, j))],
        out_shape=[jax.ShapeDtypeStruct((b, MLA_HEADS, LANES, s), BF16),
                   jax.ShapeDtypeStruct((b, MLA_HEADS, s, LANES), BF16),
                   jax.ShapeDtypeStruct((b, MLA_HEADS * VT_ROWS, s), BF16),
                   jax.ShapeDtypeStruct((b, MLA_HEADS, SUBLANES, s), F32),
                   jax.ShapeDtypeStruct((b, MLA_HEADS, SUBLANES, s), F32)],
        compiler_params=_params("parallel", "parallel"),
        name="mla_prep",
    )(cq, ckv, misc, cos, sina, sinb, qag, kvag, wq, wk, wvt, qg, kg, kpg)


def _bounds_kernel(qsq_ref, ksq_ref, u_ref, flag_ref):
    worst = None
    for h in range(MLA_HEADS):
        kmax = jnp.max(ksq_ref[0, h], axis=-1, keepdims=True)
        u = jnp.sqrt(qsq_ref[0, h] * kmax) * SCORE_BOUND_SLACK
        u_ref[0, h] = u
        umax = jnp.max(u, axis=-1, keepdims=True)
        worst = umax if worst is None else jnp.maximum(worst, umax)
    flag_ref[0] = jnp.broadcast_to((worst <= SCORE_BOUND_LIMIT).astype(jnp.int32), flag_ref.shape[1:])


def _score_bounds(qsq, ksq):
    b, h, r, s = qsq.shape
    spec = pl.BlockSpec((1, h, r, s), lambda i: (i, 0, 0, 0))
    return pl.pallas_call(
        _bounds_kernel,
        grid=(b,),
        in_specs=[spec, spec],
        out_specs=[spec, pl.BlockSpec((1, SUBLANES, LANES), lambda i: (i, 0, 0))],
        out_shape=[jax.ShapeDtypeStruct((b, h, r, s), F32), jax.ShapeDtypeStruct((b, SUBLANES, LANES), jnp.int32)],
        compiler_params=_params("parallel"),
        name="score_bounds",
    )(qsq, ksq)


def _attn_block(q_ref, k_ref, vt_ref, u_ref, m_sc, acc_sc, visible, bounded):
    if bounded and visible is None:
        hq = TQ // 2
        items = [(h, c) for h in range(MLA_HEADS) for c in (0, hq)]

        def half_scores(h, c):
            return _dot(k_ref[0, h], q_ref[0, h, :, c:c + hq])

        nxt = half_scores(*items[0])
        for i, (h, c) in enumerate(items):
            st = nxt
            if i + 1 < len(items):
                nxt = half_scores(*items[i + 1])
            pt = jnp.exp2(st - u_ref[0, h, 0:1, c:c + hq])
            vt = vt_ref[0, h * VT_ROWS:(h + 1) * VT_ROWS, :]
            acc_sc[h, :, c:c + hq] = acc_sc[h, :, c:c + hq] + _dot(vt, pt.astype(BF16))
        return

    scores = [_dot(k_ref[0, 0], q_ref[0, 0])]
    for h in range(MLA_HEADS):
        if h + 1 < MLA_HEADS:
            scores.append(_dot(k_ref[0, h + 1], q_ref[0, h + 1]))
        st = scores[h]
        if visible is not None:
            st = jnp.where(visible, st, -jnp.inf)
        vt = vt_ref[0, h * VT_ROWS:(h + 1) * VT_ROWS, :]
        if bounded:
            pt = jnp.exp2(st - u_ref[0, h, 0:1, :])
            acc_sc[h] = acc_sc[h] + _dot(vt, pt.astype(BF16))
        else:
            m_prev = m_sc[h]
            m_new = jnp.maximum(m_prev, jnp.max(st, axis=0, keepdims=True))
            alpha = jnp.exp2(m_prev - m_new)
            pt = jnp.exp2(st - m_new)
            acc_sc[h] = alpha * acc_sc[h] + _dot(vt, pt.astype(BF16))
            m_sc[h] = m_new


def _attn_diag_block(q_ref, k_ref, vt_ref, u_ref, acc_sc):
    hk = TK // 2
    r = lax.broadcasted_iota(jnp.int32, (hk, TQ), 0)
    c = lax.broadcasted_iota(jnp.int32, (hk, TQ), 1)
    vis_old = r <= c
    vis_new = vis_old[:, :hk]

    def scores(h):
        return (_dot(k_ref[0, h, 0:hk, :], q_ref[0, h]),
                _dot(k_ref[0, h, hk:, :], q_ref[0, h, :, hk:]))

    nxt = scores(0)
    for h in range(MLA_HEADS):
        s_old, s_new = nxt
        if h + 1 < MLA_HEADS:
            nxt = scores(h + 1)
        u = u_ref[0, h, 0:1, :]
        p_old = jnp.exp2(jnp.where(vis_old, s_old, -jnp.inf) - u)
        p_new = jnp.exp2(jnp.where(vis_new, s_new, -jnp.inf) - u[:, hk:])
        vt = vt_ref[0, h * VT_ROWS:(h + 1) * VT_ROWS, :]
        acc_sc[h] = acc_sc[h] + _dot(vt[:, :hk], p_old.astype(BF16))
        acc_sc[h, :, hk:] = acc_sc[h, :, hk:] + _dot(vt[:, hk:], p_new.astype(BF16))


def _attn_kernel(qi_ref, kj_ref, last_ref, flag_ref, q_ref, k_ref, vt_ref, u_ref, o_ref, m_sc, acc_sc):
    p = pl.program_id(1)
    qi = qi_ref[p]
    kj = kj_ref[p]
    bounded = flag_ref[pl.program_id(0)] == 1

    @pl.when(kj == 0)
    def _():
        m_sc[...] = jnp.full_like(m_sc, -jnp.inf)
        acc_sc[...] = jnp.zeros_like(acc_sc)

    all_visible = kj * TK + (TK - 1) <= qi * TQ

    def run(visible_fn, use_bound):
        def body():
            _attn_block(q_ref, k_ref, vt_ref, u_ref, m_sc, acc_sc, visible_fn(), use_bound)
        return body

    def causal_mask():
        kpos = kj * TK + lax.broadcasted_iota(jnp.int32, (TK, TQ), 0)
        qpos = qi * TQ + lax.broadcasted_iota(jnp.int32, (TK, TQ), 1)
        return kpos <= qpos

    partly = jnp.logical_not(all_visible)
    unbounded = jnp.logical_not(bounded)
    pl.when(all_visible & bounded)(run(lambda: None, True))
    if TQ == TK:
        pl.when(partly & bounded)(lambda: _attn_diag_block(q_ref, k_ref, vt_ref, u_ref, acc_sc))
    else:
        pl.when(partly & bounded)(run(causal_mask, True))
    pl.when(all_visible & unbounded)(run(lambda: None, False))
    pl.when(partly & unbounded)(run(causal_mask, False))

    @pl.when(last_ref[p] == 1)
    def _():
        out_t = jnp.concatenate([acc_sc[h, :MLA_V] / acc_sc[h, MLA_V:MLA_V + 1] for h in range(MLA_HEADS)], axis=0)
        o_ref[0] = out_t.T.astype(o_ref.dtype)


def _attention(q, k, vt, u, flag):
    b, _, s, _ = k.shape
    nq = s // TQ
    pairs = [(i, j) for i in range(nq) for j in range((i * TQ + TQ - 1) // TK + 1)]
    qi = jnp.asarray(np.array([p[0] for p in pairs], np.int32))
    kj = jnp.asarray(np.array([p[1] for p in pairs], np.int32))
    last = jnp.asarray(np.array([int(p[1] == (p[0] * TQ + TQ - 1) // TK) for p in pairs], np.int32))
    grid_spec = pltpu.PrefetchScalarGridSpec(
        num_scalar_prefetch=4,
        grid=(b, len(pairs)),
        in_specs=[pl.BlockSpec((1, MLA_HEADS, LANES, TQ), lambda i, p, qi, kj, *_: (i, 0, 0, qi[p])),
                  pl.BlockSpec((1, MLA_HEADS, TK, LANES), lambda i, p, qi, kj, *_: (i, 0, kj[p], 0)),
                  pl.BlockSpec((1, MLA_HEADS * VT_ROWS, TK), lambda i, p, qi, kj, *_: (i, 0, kj[p])),
                  pl.BlockSpec((1, MLA_HEADS, SUBLANES, TQ), lambda i, p, qi, kj, *_: (i, 0, 0, qi[p]))],
        out_specs=pl.BlockSpec((1, TQ, MLA_WIDTH), lambda i, p, qi, kj, *_: (i, qi[p], 0)),
        scratch_shapes=[pltpu.VMEM((MLA_HEADS, 1, TQ), F32),
                        pltpu.VMEM((MLA_HEADS, VT_ROWS, TQ), F32)],
    )
    return pl.pallas_call(
        _attn_kernel,
        grid_spec=grid_spec,
        out_shape=jax.ShapeDtypeStruct((b, s, MLA_WIDTH), BF16),
        compiler_params=_params("parallel", "arbitrary"),
        name="mla_attention",
    )(qi, kj, last, flag, q, k, vt, u)


POOL_HALO = 16


def _post_kernel(x_ref, xa_ref, yb_ref, yc_ref, wp_ref, ps_ref, wo_ref, g_ref, wu_ref, wd_ref, o_ref,
                 halo_sc, ext_sc):
    ts = x_ref.shape[1]
    s_idx = pl.program_id(1)

    @pl.when(s_idx == 0)
    def _():
        halo_sc[...] = jnp.zeros_like(halo_sc)

    xa = xa_ref[0]
    ext_sc[0:POOL_HALO, :] = halo_sc[...]
    ext_sc[POOL_HALO:, :] = xa
    halo_sc[...] = xa_ref[0, ts - POOL_HALO:ts, :]

    assert POOL_WINDOWS == tuple(2 << g for g in range(POOL_GROUPS)) and POOL_WINDOWS[-1] <= POOL_HALO
    e = ext_sc[...]
    sums = []
    for g in range(POOL_GROUPS):
        e = e + pltpu.roll(e, 1 << g, 0)
        sums.append(e[POOL_HALO:])
    lane = lax.broadcasted_iota(jnp.int32, (ts, POOL_WIDTH), 1)
    grp = lane >> (POOL_GROUP_DIM.bit_length() - 1)
    win = jnp.left_shift(2, grp)
    t = s_idx * ts + lax.broadcasted_iota(jnp.int32, (ts, POOL_WIDTH), 0)
    count = jnp.minimum(t + 1, win).astype(F32)
    pooled = sums[POOL_GROUPS - 1]
    for g in range(POOL_GROUPS - 2, -1, -1):
        pooled = jnp.where(grp == g, sums[g], pooled)
    ya = _dot((pooled / count - xa).astype(BF16), wp_ref[...]) * ps_ref[...]

    acc = x_ref[0] + _dot(ya.astype(BF16), wo_ref[0:POOL_WIDTH, :])
    acc = acc + _dot(yb_ref[0].astype(BF16), wo_ref[POOL_WIDTH:POOL_WIDTH + DN_WIDTH, :])
    x1 = acc + _dot(yc_ref[0].astype(BF16), wo_ref[POOL_WIDTH + DN_WIDTH:, :])

    h = _rms_rows(x1, g_ref[...]).astype(BF16)
    u = jnp.maximum(_dot(h, wu_ref[...]), 0.0)
    o_ref[0] = x1 + _dot((u * u).astype(BF16), wd_ref[...])


def _post(x, xa, yb, yc, wp_bd, pool_scale, w_out, gain, w_up, w_down):
    b, s, _ = x.shape
    ts = TM_PROJ

    def tok(n):
        return pl.BlockSpec((1, ts, n), lambda i, j: (i, j, 0))

    return pl.pallas_call(
        _post_kernel,
        grid=(b, s // ts),
        in_specs=[tok(D_MODEL), tok(POOL_WIDTH), tok(DN_WIDTH), tok(MLA_WIDTH),
                  _const_spec((POOL_WIDTH, POOL_WIDTH)), _const_spec((1, POOL_WIDTH)),
                  _const_spec((D_MODEL, D_MODEL)), _const_spec((1, D_MODEL)),
                  _const_spec((D_MODEL, D_FF)), _const_spec((D_FF, D_MODEL))],
        out_specs=tok(D_MODEL),
        out_shape=jax.ShapeDtypeStruct((b, s, D_MODEL), F32),
        scratch_shapes=[pltpu.VMEM((POOL_HALO, POOL_WIDTH), F32),
                        pltpu.VMEM((ts + POOL_HALO, POOL_WIDTH), F32)],
        compiler_params=_params("parallel", "arbitrary"),
        name="outproj_pool_mlp",
    )(x, xa, yb, yc, wp_bd, pool_scale, w_out, gain, w_up, w_down)


def _head_block(nope, pe):
    z16 = jnp.zeros(pe.shape[:-1] + (HALF_ROPE,), pe.dtype)
    return jnp.concatenate([nope, pe[..., :HALF_ROPE], z16, pe[..., HALF_ROPE:], z16], axis=-1)


def _prep_layer(w_in, pool_w, pool_scale, dn_conv, dn_a_log, dn_dt_bias, dn_norm,
                q_a_norm, w_q_b, kv_a_norm, w_kv_b, q_norm, k_norm, w_out, w_up, w_down):
    offs = np.cumsum((0,) + IN_SPLITS)
    seg = [w_in[:, offs[i]:offs[i + 1]] for i in range(len(IN_SPLITS))]
    w_xa, w_qkv, w_z, w_b, w_a, w_cq, w_ckv, w_kpe = seg
    d = w_in.shape[0]
    z64 = jnp.zeros((d, MLA_NOPE), w_in.dtype)
    w_misc = jnp.concatenate([w_b, w_a, jnp.zeros((d, MLA_NOPE - 2 * DN_HEADS), w_in.dtype),
                              _head_block(z64, w_kpe)[:, MLA_NOPE:]], axis=1)
    w_cat = jnp.concatenate([w_qkv, w_xa, w_z, w_cq, w_ckv, w_misc], axis=1).astype(BF16)

    lane_pad = jnp.zeros((LANES - 2 * DN_HEADS,), F32)
    alog_l = jnp.concatenate([jnp.zeros((DN_HEADS,), F32), dn_a_log, lane_pad])[None]
    dtb_l = jnp.concatenate([jnp.zeros((DN_HEADS,), F32), dn_dt_bias, lane_pad])[None]

    wq = w_q_b.reshape(Q_LORA, MLA_HEADS, MLA_QK_DIM)
    wq = _head_block(wq[..., :MLA_NOPE], wq[..., MLA_NOPE:]).reshape(Q_LORA, MLA_HEADS * LANES).astype(BF16)
    wkv = w_kv_b.reshape(KV_LORA, MLA_HEADS, MLA_NOPE + MLA_V)
    wk = _head_block(wkv[..., :MLA_NOPE], jnp.zeros((KV_LORA, MLA_HEADS, MLA_ROPE), F32))
    wk = wk.reshape(KV_LORA, MLA_HEADS * LANES).astype(BF16)
    wv = jnp.concatenate([wkv[..., MLA_NOPE:], jnp.zeros((KV_LORA, MLA_HEADS, VT_ROWS - MLA_V), F32)], axis=-1)
    wvt = wv.reshape(KV_LORA, MLA_HEADS * VT_ROWS).T.astype(BF16)
    qg = _head_block(q_norm[:MLA_NOPE], q_norm[MLA_NOPE:])[None]
    kg = _head_block(k_norm[:MLA_NOPE], jnp.zeros((MLA_ROPE,), F32))[None]
    kpg = _head_block(jnp.zeros((MLA_NOPE,), F32), k_norm[MLA_NOPE:])[None]

    wp_bd = jax.scipy.linalg.block_diag(*[pool_w[g] for g in range(POOL_GROUPS)]).astype(BF16)
    return dict(w_cat=w_cat, alog_l=alog_l, dtb_l=dtb_l, conv_w=dn_conv, dn_norm=dn_norm[None],
                qag=q_a_norm[None], kvag=kv_a_norm[None], wq=wq, wk=wk, wvt=wvt, qg=qg, kg=kg, kpg=kpg,
                wp_bd=wp_bd, pool_scale=pool_scale[None], w_out=w_out.astype(BF16),
                w_up=w_up.astype(BF16), w_down=w_down.astype(BF16))


def kernel(x, positions, attn_norm, w_in, pool_w, pool_scale, dn_conv, dn_a_log, dn_dt_bias, dn_norm,
           mla_q_a_norm, mla_w_q_b, mla_kv_a_norm, mla_w_kv_b, mla_q_norm, mla_k_norm,
           w_out, mlp_norm, w_up, w_down):
    b, s, d = x.shape
    depth = w_in.shape[0]
    inv_freq = ROPE_THETA ** (-jnp.arange(0, MLA_ROPE, 2, dtype=F32) / MLA_ROPE)
    freq_lanes = _head_block(jnp.zeros((MLA_NOPE,), F32), jnp.concatenate([inv_freq, inv_freq]))[None]
    cos, sina, sinb = _rope_tables(positions, freq_lanes)

    for l in range(depth):
        p = _prep_layer(w_in[l], pool_w[l], pool_scale[l], dn_conv[l], dn_a_log[l], dn_dt_bias[l], dn_norm[l],
                        mla_q_a_norm[l], mla_w_q_b[l], mla_kv_a_norm[l], mla_w_kv_b[l], mla_q_norm[l],
                        mla_k_norm[l], w_out[l], w_up[l], w_down[l])
        qkv, xa, z, cq, ckv, misc = _inproj(x, attn_norm[l][None], p["w_cat"], p["conv_w"])
        y_b = _deltanet(qkv, z, misc, p["alog_l"], p["dtb_l"], p["dn_norm"])
        q, k, vt, qsq, ksq = _mla_prep(cq, ckv, misc, cos, sina, sinb, p["qag"], p["kvag"], p["wq"], p["wk"],
                                       p["wvt"], p["qg"], p["kg"], p["kpg"])
        u, flag = _score_bounds(qsq, ksq)
        y_c = _attention(q, k, vt, u, flag[:, 0, 0])
        x = _post(x, xa, y_b, y_c, p["wp_bd"], p["pool_scale"], p["w_out"], mlp_norm[l][None], p["w_up"], p["w_down"])
    return x
```

```python
import functools
import math

import jax
import jax.numpy as jnp
import numpy as np
from jax import lax
from jax.experimental import pallas as pl
from jax.experimental.pallas import tpu as pltpu

F32 = jnp.float32
BF16 = jnp.bfloat16

D_MODEL = 1024
POOL_GROUPS = 4
POOL_GROUP_DIM = 64
POOL_WIDTH = POOL_GROUPS * POOL_GROUP_DIM
POOL_WINDOWS = (2, 4, 8, 16)
DN_HEADS = 4
DN_HEAD_DIM = 128
DN_WIDTH = DN_HEADS * DN_HEAD_DIM
DN_CONV = 4
DN_CHUNK = 64
MLA_HEADS = 4
MLA_NOPE = 64
MLA_ROPE = 32
MLA_QK_DIM = MLA_NOPE + MLA_ROPE
MLA_V = 64
MLA_WIDTH = MLA_HEADS * MLA_V
Q_LORA = 256
KV_LORA = 128
ROPE_THETA = 10000.0
D_FF = 4 * D_MODEL
EPS = 1e-6
IN_SPLITS = (POOL_WIDTH, 3 * DN_WIDTH, DN_WIDTH, DN_HEADS, DN_HEADS, Q_LORA, KV_LORA, MLA_ROPE)

LANES = 128
SUBLANES = 8
MXU_COLS = 256
VMEM_LIMIT_BYTES = 56 * 1024 * 1024

TM_INPROJ = 256
TM_PROJ = 512
TM_PREP = 1024
DN_TILE = 256
DN_PAIR = 2 * DN_CHUNK
TQ = 1024
TK = 1024
HALF_ROPE = MLA_ROPE // 2
VT_ROWS = MLA_V + 16
SCORE_BOUND_SLACK = 1.02
SCORE_BOUND_LIMIT = 30.0
PE1_LANE = MLA_NOPE
PE2_LANE = MLA_NOPE + 2 * HALF_ROPE


def _dot(a, b):
    return jnp.dot(a, b, preferred_element_type=F32)


def _dot_nt(a, b):
    return lax.dot_general(a, b, (((1,), (1,)), ((), ())), preferred_element_type=F32)


def _rms_rows(x, gain):
    return x * lax.rsqrt(jnp.mean(x * x, axis=-1, keepdims=True) + EPS) * gain


def _split3(x):
    x1 = x.astype(BF16)
    r1 = x - x1.astype(F32)
    x2 = r1.astype(BF16)
    r2 = r1 - x2.astype(F32)
    return x1, x2, r2.astype(BF16)


def _sigmoid(x):
    return 1.0 / (1.0 + jnp.exp(-x))


def _params(*sem):
    return pltpu.CompilerParams(dimension_semantics=sem, vmem_limit_bytes=VMEM_LIMIT_BYTES)


def _const_spec(shape):
    nd = len(shape)
    return pl.BlockSpec(shape, lambda *_: (0,) * nd, pipeline_mode=pl.Buffered(1))


IN_SEGS = (3 * DN_WIDTH, POOL_WIDTH, DN_WIDTH, Q_LORA, KV_LORA, LANES)
CONV_HALO = SUBLANES


def _inproj_kernel(x_ref, g_ref, w_ref, cw_ref, qkv_ref, *rest):
    o_refs, (halo_sc, ext_sc) = rest[:-2], rest[-2:]
    tm = x_ref.shape[1]
    nqkv = 3 * DN_WIDTH
    D = DN_HEAD_DIM

    @pl.when(pl.program_id(1) == 0)
    def _():
        halo_sc[...] = jnp.zeros_like(halo_sc)

    h = _rms_rows(x_ref[0], g_ref[...]).astype(BF16)
    ext_sc[0:CONV_HALO, :] = halo_sc[...]

    def conv_cols(c0):
        e = ext_sc[:, c0:c0 + D]
        cw = cw_ref[:, c0:c0 + D]
        e1 = pltpu.roll(e, 1, 0)
        near = cw[3:4] * e + cw[2:3] * e1
        far = cw[1:2] * e + cw[0:1] * e1
        acc = (near + pltpu.roll(far, 2, 0))[CONV_HALO:]
        y = acc * _sigmoid(acc)
        if c0 < 2 * DN_WIDTH:
            y = y * lax.rsqrt(jnp.sum(y * y, axis=-1, keepdims=True) + EPS)
            if c0 < DN_WIDTH:
                y = y * (D ** -0.5)
        qkv_ref[0, :, c0:c0 + D] = y

    pieces = []
    off = nqkv
    for o_ref in o_refs:
        n = o_ref.shape[-1]
        pieces += [(o_ref, c, off + c, LANES) for c in range(0, n, LANES)]
        off += n
    per_dot = MXU_COLS // LANES
    others = [pieces[i:i + per_dot] for i in range(0, len(pieces), per_dot)]
    def qkv_cols(c0):
        ext_sc[CONV_HALO:, c0:c0 + MXU_COLS] = _dot(h, w_ref[:, c0:c0 + MXU_COLS])

    nsteps = nqkv // MXU_COLS
    qkv_cols(0)
    for i in range(nsteps):
        if i + 1 < nsteps:
            qkv_cols((i + 1) * MXU_COLS)
        for c in range(i * MXU_COLS, (i + 1) * MXU_COLS, D):
            conv_cols(c)
        for group in others[i::nsteps]:
            w0 = group[0][2]
            res = _dot(h, w_ref[:, w0:w0 + LANES * len(group)])
            for o_ref, c, woff, n in group:
                o_ref[0, :, c:c + n] = res[:, woff - w0:woff - w0 + n]
    halo_sc[...] = ext_sc[tm:tm + CONV_HALO, :]


def _inproj(x, gain, w_cat, conv_w):
    b, s, _ = x.shape
    tm = TM_INPROJ
    n_all = sum(IN_SEGS)
    return pl.pallas_call(
        _inproj_kernel,
        grid=(b, s // tm),
        in_specs=[pl.BlockSpec((1, tm, D_MODEL), lambda i, j: (i, j, 0)),
                  _const_spec((1, D_MODEL)),
                  _const_spec((D_MODEL, n_all)),
                  _const_spec((DN_CONV, 3 * DN_WIDTH))],
        out_specs=[pl.BlockSpec((1, tm, n), lambda i, j: (i, j, 0)) for n in IN_SEGS],
        out_shape=[jax.ShapeDtypeStruct((b, s, n), F32) for n in IN_SEGS],
        scratch_shapes=[pltpu.VMEM((CONV_HALO, 3 * DN_WIDTH), F32),
                        pltpu.VMEM((tm + CONV_HALO, 3 * DN_WIDTH), F32)],
        compiler_params=_params("parallel", "arbitrary"),
        name="inproj",
    )(x, gain, w_cat, conv_w)


def _dn_kernel(qkv_ref, z_ref, misc_ref, alog_ref, dtb_ref, ng_ref, y_ref, state_sc):
    nb = qkv_ref.shape[0]
    L = DN_TILE
    C = DN_CHUNK
    nchunk = L // C
    D = DN_HEAD_DIM
    s_idx = pl.program_id(0)
    chains = [(b, h) for b in range(nb) for h in range(DN_HEADS)]

    @pl.when(s_idx == 0)
    def _():
        state_sc[...] = jnp.zeros_like(state_sc)

    def chunk_masks(n):
        ri = lax.broadcasted_iota(jnp.int32, (n, n), 0)
        ci = lax.broadcasted_iota(jnp.int32, (n, n), 1)
        shift = DN_CHUNK.bit_length() - 1
        same = (ri >> shift) == (ci >> shift)
        return same, same & (ci <= ri), same & (ci < ri), ri == ci

    same_l, causal_l, _, _ = chunk_masks(L)
    cum_mat = jnp.concatenate([causal_l.astype(BF16), same_l.astype(BF16)], axis=0)
    P = DN_PAIR
    nblk = L // P
    _, causal_bd, strict_bd, diag = chunk_masks(P)
    eye = diag.astype(F32)

    beta_all, gcum_all, glast_all, gcum_t = [], [], [], []
    for b in range(nb):
        misc = misc_ref[b]
        beta_all.append(_sigmoid(misc))
        sp_in = misc + dtb_ref[...]
        softplus = jnp.maximum(sp_in, 0.0) + jnp.log1p(jnp.exp(-jnp.abs(sp_in)))
        g_all = -jnp.exp(alog_ref[...]) * softplus
        g1, g2, g3 = _split3(g_all)
        cum = _dot(cum_mat, g1) + _dot(cum_mat, g2) + _dot(cum_mat, g3)
        gcum_all.append(cum[:L])
        glast_all.append(cum[L:])
        gcum_t.append(cum[:L].T)

    def stage1(b, h):
        a_mats, attns = [], []
        lane = DN_HEADS + h
        gc_col = gcum_all[b][:, lane:lane + 1]
        gl_col = glast_all[b][:, lane:lane + 1]
        gc_row = gcum_t[b][lane:lane + 1, :]
        beta = beta_all[b][:, h:h + 1]
        qn = qkv_ref[b, :, h * D:(h + 1) * D]
        kn = qkv_ref[b, :, DN_WIDTH + h * D:DN_WIDTH + (h + 1) * D]
        v = qkv_ref[b, :, 2 * DN_WIDTH + h * D:2 * DN_WIDTH + (h + 1) * D]
        kb = kn * beta
        e_col = jnp.exp(gc_col)
        kn16 = kn.astype(BF16)
        kb16 = kb.astype(BF16)
        qn16 = qn.astype(BF16)
        for j in range(nblk):
            rows = slice(j * P, (j + 1) * P)
            decay = jnp.exp(jnp.where(causal_bd, gc_col[rows] - gc_row[:, rows], 0.0))
            gram = _dot_nt(jnp.concatenate([kb16[rows], qn16[rows]], axis=0), kn16[rows])
            a_mats.append(jnp.where(strict_bd, gram[:P] * decay, 0.0))
            attns.append(jnp.where(causal_bd, gram[P:] * decay, 0.0).astype(BF16))
        return dict(a=a_mats, attn=attns,
                    rhs=jnp.concatenate([v * beta, kb * e_col], axis=1).astype(BF16),
                    qd=(qn * e_col).astype(BF16),
                    kdt=(kn * jnp.exp(gl_col - gc_col)).T.astype(BF16),
                    gdec=jnp.exp(jnp.broadcast_to(gl_col, (L, D))))

    def stage2(group):
        a_mats = [a for ch in group for a in ch["a"]]
        xps = [(-a).astype(BF16) for a in a_mats]
        t_invs = [eye - a for a in a_mats]
        xps = [_dot(xp, xp).astype(BF16) for xp in xps]
        for _ in range(4):
            prods = [_dot(jnp.concatenate([t.astype(BF16), xp], axis=0), xp) for t, xp in zip(t_invs, xps)]
            t_invs = [t + pr[:P] for t, pr in zip(t_invs, prods)]
            xps = [pr[P:].astype(BF16) for pr in prods]
        t_invs = [t + _dot(t.astype(BF16), xp) for t, xp in zip(t_invs, xps)]
        for i, ch in enumerate(group):
            ch["uw"] = [_dot(t_invs[i * nblk + j].astype(BF16), ch["rhs"][j * P:(j + 1) * P])
                        for j in range(nblk)]

    chs = [stage1(b, h) for b, h in chains]
    stage2(chs)

    states = [state_sc[b, h] for b, h in chains]
    o_parts = [[] for _ in chains]
    zeros_c = jnp.zeros((C, D), BF16)
    for c in range(nchunk):
        r0 = c * C
        j, half = divmod(c, P // C)
        p0 = half * C
        rs = [_dot(jnp.concatenate([ch["uw"][j][p0:p0 + C, D:].astype(BF16), ch["qd"][r0:r0 + C]], axis=0),
                   states[i].astype(BF16)) for i, ch in enumerate(chs)]
        for i, ch in enumerate(chs):
            v_new = (ch["uw"][j][p0:p0 + C, :D] - rs[i][:C]).astype(BF16)
            v_blk = jnp.concatenate([zeros_c] * half + [v_new] + [zeros_c] * (P // C - 1 - half), axis=0)
            lhs = jnp.concatenate([ch["attn"][j][p0:p0 + C, :], ch["kdt"][:, j * P:(j + 1) * P]], axis=0)
            m2 = _dot(lhs, v_blk)
            o_parts[i].append(rs[i][C:] + m2[:C])
            states[i] = states[i] * ch["gdec"][r0:r0 + 1, :] + m2[C:]

    for i, (b, h) in enumerate(chains):
        state_sc[b, h] = states[i]
        o = jnp.concatenate(o_parts[i], axis=0)
        zh = z_ref[b, :, h * D:(h + 1) * D]
        y_ref[b, :, h * D:(h + 1) * D] = (_rms_rows(o, ng_ref[...]) * (zh * _sigmoid(zh))).astype(y_ref.dtype)


def _deltanet(qkv, z, misc, alog_l, dtb_l, norm_gain):
    b, s, _ = qkv.shape
    L = DN_TILE
    return pl.pallas_call(
        _dn_kernel,
        grid=(s // L,),
        in_specs=[pl.BlockSpec((b, L, 3 * DN_WIDTH), lambda j: (0, j, 0)),
                  pl.BlockSpec((b, L, DN_WIDTH), lambda j: (0, j, 0)),
                  pl.BlockSpec((b, L, LANES), lambda j: (0, j, 0)),
                  _const_spec((1, LANES)),
                  _const_spec((1, LANES)),
                  _const_spec((1, DN_HEAD_DIM))],
        out_specs=pl.BlockSpec((b, L, DN_WIDTH), lambda j: (0, j, 0)),
        out_shape=jax.ShapeDtypeStruct((b, s, DN_WIDTH), BF16),
        scratch_shapes=[pltpu.VMEM((b, DN_HEADS, DN_HEAD_DIM, DN_HEAD_DIM), F32)],
        compiler_params=_params("arbitrary"),
        name="deltanet",
    )(qkv, z, misc, alog_l, dtb_l, norm_gain)


def _rope_kernel(pos_ref, freq_ref, cos_ref, sina_ref, sinb_ref):
    tm = pos_ref.shape[1]
    half = tm // 2
    fold = LANES // 2
    lane = lax.broadcasted_iota(jnp.int32, (half, LANES), 1)
    upper = lane >= fold
    freq = freq_ref[...]
    freq2 = freq + pltpu.roll(freq, fold, 1)
    pos = jnp.where(upper, pos_ref[0, 0:half].astype(F32), pos_ref[0, half:tm].astype(F32))
    ang = pos * freq2
    cos2 = jnp.cos(ang)
    sin2 = jnp.sin(ang)
    pe1 = (lane >= PE1_LANE) & (lane < PE1_LANE + HALF_ROPE)
    pe2 = (lane >= PE2_LANE) & (lane < PE2_LANE + HALF_ROPE)
    for rows, cos, sin in ((slice(0, half), cos2, sin2),
                           (slice(half, tm), pltpu.roll(cos2, fold, 1), pltpu.roll(sin2, fold, 1))):
        cos_ref[0, rows] = jnp.where(lane < MLA_NOPE, 1.0, jnp.where(pe1 | pe2, cos, 0.0))
        sina_ref[0, rows] = jnp.where(pe2, sin, 0.0)
        sinb_ref[0, rows] = jnp.where(pe1, -sin, 0.0)


def _rope_tables(positions, freq_lanes):
    b, s = positions.shape
    tm = TM_PREP
    spec = pl.BlockSpec((1, tm, LANES), lambda i, j: (i, j, 0))
    return pl.pallas_call(
        _rope_kernel,
        grid=(b, s // tm),
        in_specs=[pl.BlockSpec((1, tm, 1), lambda i, j: (i, j, 0)), _const_spec((1, LANES))],
        out_specs=[spec, spec, spec],
        out_shape=[jax.ShapeDtypeStruct((b, s, LANES), F32)] * 3,
        compiler_params=_params("parallel", "parallel"),
        name="rope_tables",
    )(positions.reshape(b, s, 1), freq_lanes)


def _rope(x, cos, sina, sinb):
    return x * cos + pltpu.roll(x, 2 * HALF_ROPE, 1) * sina + pltpu.roll(x, LANES - 2 * HALF_ROPE, 1) * sinb


def _mla_prep_kernel(cq_ref, ckv_ref, misc_ref, cos_ref, sina_ref, sinb_ref,
                     qag_ref, kvag_ref, wq_ref, wk_ref, wvt_ref, qg_ref, kg_ref, kpg_ref,
                     q_ref, k_ref, vt_ref, qsq_ref, ksq_ref):
    tm = cq_ref.shape[1]
    ones8 = jnp.ones((SUBLANES, LANES), BF16)
    eye = (lax.broadcasted_iota(jnp.int32, (LANES, LANES), 0)
           == lax.broadcasted_iota(jnp.int32, (LANES, LANES), 1)).astype(BF16)

    def row_sq_norms(x16):
        xf = x16.astype(F32)
        return _dot_nt(ones8, (xf * xf).astype(BF16))

    cos = cos_ref[0]
    sina = sina_ref[0]
    sinb = sinb_ref[0]
    lane = lax.broadcasted_iota(jnp.int32, (tm, LANES), 1)
    is_nope = lane < MLA_NOPE

    cqn = _rms_rows(cq_ref[0], qag_ref[...]).astype(BF16)
    ckvn = _rms_rows(ckv_ref[0], kvag_ref[...]).astype(BF16)
    qf = _dot(cqn, wq_ref[...])
    kf = _dot(ckvn, wk_ref[...])
    vt = _dot_nt(wvt_ref[...], ckvn)
    vrow = lax.broadcasted_iota(jnp.int32, vt.shape, 0)
    ones_row = functools.reduce(jnp.logical_or, [vrow == h * VT_ROWS + MLA_V for h in range(MLA_HEADS)])
    vt_ref[0] = jnp.where(ones_row, 1.0, vt).astype(BF16)

    kp = jnp.where(is_nope, 0.0, misc_ref[0])
    kp_ms = jnp.sum(kp * kp, axis=-1, keepdims=True) * (1.0 / MLA_ROPE)
    kp = _rope(kp * lax.rsqrt(kp_ms + EPS) * kpg_ref[...], cos, sina, sinb)

    scale = MLA_QK_DIM ** -0.5 * math.log2(math.e)
    for h in range(MLA_HEADS):
        xq = qf[:, h * LANES:(h + 1) * LANES]
        sq = xq * xq
        ms_n = jnp.sum(jnp.where(is_nope, sq, 0.0), axis=-1, keepdims=True) * (1.0 / MLA_NOPE)
        ms_p = jnp.sum(jnp.where(is_nope, 0.0, sq), axis=-1, keepdims=True) * (1.0 / MLA_ROPE)
        inv = jnp.where(is_nope, lax.rsqrt(ms_n + EPS), lax.rsqrt(ms_p + EPS))
        qh = _rope(xq * inv * qg_ref[...], cos, sina, sinb) * scale
        qt = _dot_nt(eye, qh.astype(BF16))
        q_ref[0, h] = qt.astype(BF16)
        qsq_ref[0, h] = jnp.broadcast_to(jnp.sum(qt * qt, axis=0, keepdims=True), (SUBLANES, tm))

        xk = kf[:, h * LANES:(h + 1) * LANES]
        ms_k = jnp.sum(xk * xk, axis=-1, keepdims=True) * (1.0 / MLA_NOPE)
        kh = xk * lax.rsqrt(ms_k + EPS) * kg_ref[...] + kp
        k16 = kh.astype(BF16)
        k_ref[0, h] = k16
        ksq_ref[0, h] = row_sq_norms(k16)


def _mla_prep(cq, ckv, misc, cos, sina, sinb, qag, kvag, wq, wk, wvt, qg, kg, kpg):
    b, s, _ = cq.shape
    tm = TM_PREP
    hl = MLA_HEADS * LANES

    def tok(n):
        return pl.BlockSpec((1, tm, n), lambda i, j: (i, j, 0))

    return pl.pallas_call(
        _mla_prep_kernel,
        grid=(b, s // tm),
        in_specs=[tok(Q_LORA), tok(KV_LORA), tok(LANES), tok(LANES), tok(LANES), tok(LANES),
                  _const_spec((1, Q_LORA)), _const_spec((1, KV_LORA)),
                  _const_spec((Q_LORA, hl)), _const_spec((KV_LORA, hl)), _const_spec((MLA_HEADS * VT_ROWS, KV_LORA)),
                  _const_spec((1, LANES)), _const_spec((1, LANES)), _const_spec((1, LANES))],
        out_specs=[pl.BlockSpec((1, MLA_HEADS, LANES, tm), lambda i, j: (i, 0, 0, j)),
                   pl.BlockSpec((1, MLA_HEADS, tm, LANES), lambda i, j: (i, 0, j, 0)),
                   pl.BlockSpec((1, MLA_HEADS * VT_ROWS, tm), lambda i, j: (i, 0, j)),
                   pl.BlockSpec((1, MLA_HEADS, SUBLANES, tm), lambda i, j: (i, 0, 0, j)),
                   pl.BlockSpec((1, MLA_HEADS, SUBLANES, tm), lambda i, j: (i, 0, 0, j))],
        out_shape=[jax.ShapeDtypeStruct((b, MLA_HEADS, LANES, s), BF16),
                   jax.ShapeDtypeStruct((b, MLA_HEADS, s, LANES), BF16),
                   jax.ShapeDtypeStruct((b, MLA_HEADS * VT_ROWS, s), BF16),
                   jax.ShapeDtypeStruct((b, MLA_HEADS, SUBLANES, s), F32),
                   jax.ShapeDtypeStruct((b, MLA_HEADS, SUBLANES, s), F32)],
        compiler_params=_params("parallel", "parallel"),
        name="mla_prep",
    )(cq, ckv, misc, cos, sina, sinb, qag, kvag, wq, wk, wvt, qg, kg, kpg)


def _dn_prep_kernel(qkv_ref, z_ref, misc_ref, alog_ref, dtb_ref, ng_ref,
                    cq_ref, ckv_ref, cos_ref, sina_ref, sinb_ref,
                    qag_ref, kvag_ref, wq_ref, wk_ref, wvt_ref, qg_ref, kg_ref, kpg_ref,
                    y_ref, q_ref, k_ref, vt_ref, qsq_ref, ksq_ref, state_sc):
    for b in range(qkv_ref.shape[0]):
        row = lambda r: r.at[pl.ds(b, 1)]
        _mla_prep_kernel(row(cq_ref), row(ckv_ref), row(misc_ref), row(cos_ref), row(sina_ref), row(sinb_ref),
                         qag_ref, kvag_ref, wq_ref, wk_ref, wvt_ref, qg_ref, kg_ref, kpg_ref,
                         row(q_ref), row(k_ref), row(vt_ref), row(qsq_ref), row(ksq_ref))
    _dn_kernel(qkv_ref, z_ref, misc_ref, alog_ref, dtb_ref, ng_ref, y_ref, state_sc)


def _deltanet_mla_prep(qkv, z, misc, alog_l, dtb_l, norm_gain,
                       cq, ckv, cos, sina, sinb, qag, kvag, wq, wk, wvt, qg, kg, kpg):
    b, s, _ = qkv.shape
    L = DN_TILE
    hl = MLA_HEADS * LANES

    def tok(n):
        return pl.BlockSpec((b, L, n), lambda j: (0, j, 0))

    lanes_tok = pl.BlockSpec((b, MLA_HEADS, SUBLANES, L), lambda j: (0, 0, 0, j))
    return pl.pallas_call(
        _dn_prep_kernel,
        grid=(s // L,),
        in_specs=[tok(3 * DN_WIDTH), tok(DN_WIDTH), tok(LANES),
                  _const_spec((1, LANES)), _const_spec((1, LANES)), _const_spec((1, DN_HEAD_DIM)),
                  tok(Q_LORA), tok(KV_LORA), tok(LANES), tok(LANES), tok(LANES),
                  _const_spec((1, Q_LORA)), _const_spec((1, KV_LORA)),
                  _const_spec((Q_LORA, hl)), _const_spec((KV_LORA, hl)), _const_spec((MLA_HEADS * VT_ROWS, KV_LORA)),
                  _const_spec((1, LANES)), _const_spec((1, LANES)), _const_spec((1, LANES))],
        out_specs=[tok(DN_WIDTH),
                   pl.BlockSpec((b, MLA_HEADS, LANES, L), lambda j: (0, 0, 0, j)),
                   pl.BlockSpec((b, MLA_HEADS, L, LANES), lambda j: (0, 0, j, 0)),
                   pl.BlockSpec((b, MLA_HEADS * VT_ROWS, L), lambda j: (0, 0, j)),
                   lanes_tok, lanes_tok],
        out_shape=[jax.ShapeDtypeStruct((b, s, DN_WIDTH), BF16),
                   jax.ShapeDtypeStruct((b, MLA_HEADS, LANES, s), BF16),
                   jax.ShapeDtypeStruct((b, MLA_HEADS, s, LANES), BF16),
                   jax.ShapeDtypeStruct((b, MLA_HEADS * VT_ROWS, s), BF16),
                   jax.ShapeDtypeStruct((b, MLA_HEADS, SUBLANES, s), F32),
                   jax.ShapeDtypeStruct((b, MLA_HEADS, SUBLANES, s), F32)],
        scratch_shapes=[pltpu.VMEM((b, DN_HEADS, DN_HEAD_DIM, DN_HEAD_DIM), F32)],
        compiler_params=_params("arbitrary"),
        name="deltanet_mla_prep",
    )(qkv, z, misc, alog_l, dtb_l, norm_gain, cq, ckv, cos, sina, sinb, qag, kvag, wq, wk, wvt, qg, kg, kpg)


def _bounds_kernel(qsq_ref, ksq_ref, u_ref, flag_ref):
    worst = None
    for h in range(MLA_HEADS):
        kmax = jnp.max(ksq_ref[0, h], axis=-1, keepdims=True)
        u = jnp.sqrt(qsq_ref[0, h] * kmax) * SCORE_BOUND_SLACK
        u_ref[0, h] = u
        umax = jnp.max(u, axis=-1, keepdims=True)
        worst = umax if worst is None else jnp.maximum(worst, umax)
    flag_ref[0] = jnp.broadcast_to((worst <= SCORE_BOUND_LIMIT).astype(jnp.int32), flag_ref.shape[1:])


def _score_bounds(qsq, ksq):
    b, h, r, s = qsq.shape
    spec = pl.BlockSpec((1, h, r, s), lambda i: (i, 0, 0, 0))
    return pl.pallas_call(
        _bounds_kernel,
        grid=(b,),
        in_specs=[spec, spec],
        out_specs=[spec, pl.BlockSpec((1, SUBLANES, LANES), lambda i: (i, 0, 0))],
        out_shape=[jax.ShapeDtypeStruct((b, h, r, s), F32), jax.ShapeDtypeStruct((b, SUBLANES, LANES), jnp.int32)],
        compiler_params=_params("parallel"),
        name="score_bounds",
    )(qsq, ksq)


def _attn_block(q_ref, k_ref, vt_ref, u_ref, m_sc, acc_sc, visible, bounded):
    scores = [_dot(k_ref[0, 0], q_ref[0, 0])]
    for h in range(MLA_HEADS):
        if h + 1 < MLA_HEADS:
            scores.append(_dot(k_ref[0, h + 1], q_ref[0, h + 1]))
        st = scores[h]
        if visible is not None:
            st = jnp.where(visible, st, -jnp.inf)
        vt = vt_ref[0, h * VT_ROWS:(h + 1) * VT_ROWS, :]
        if bounded:
            pt = jnp.exp2(st - u_ref[0, h, 0:1, :])
            acc_sc[h] = acc_sc[h] + _dot(vt, pt.astype(BF16))
        else:
            m_prev = m_sc[h]
            m_new = jnp.maximum(m_prev, jnp.max(st, axis=0, keepdims=True))
            alpha = jnp.exp2(m_prev - m_new)
            pt = jnp.exp2(st - m_new)
            acc_sc[h] = alpha * acc_sc[h] + _dot(vt, pt.astype(BF16))
            m_sc[h] = m_new


def _attn_diag_block(q_ref, k_ref, vt_ref, u_ref, acc_sc):
    hk = TK // 2
    r = lax.broadcasted_iota(jnp.int32, (hk, TQ), 0)
    c = lax.broadcasted_iota(jnp.int32, (hk, TQ), 1)
    vis_old = r <= c
    vis_new = vis_old[:, :hk]

    def scores(h):
        return (_dot(k_ref[0, h, 0:hk, :], q_ref[0, h]),
                _dot(k_ref[0, h, hk:, :], q_ref[0, h, :, hk:]))

    nxt = scores(0)
    for h in range(MLA_HEADS):
        s_old, s_new = nxt
        if h + 1 < MLA_HEADS:
            nxt = scores(h + 1)
        u = u_ref[0, h, 0:1, :]
        p_old = jnp.exp2(jnp.where(vis_old, s_old, -jnp.inf) - u)
        p_new = jnp.exp2(jnp.where(vis_new, s_new, -jnp.inf) - u[:, hk:])
        vt = vt_ref[0, h * VT_ROWS:(h + 1) * VT_ROWS, :]
        acc_sc[h] = acc_sc[h] + _dot(vt[:, :hk], p_old.astype(BF16))
        acc_sc[h, :, hk:] = acc_sc[h, :, hk:] + _dot(vt[:, hk:], p_new.astype(BF16))


def _attn_kernel(qi_ref, kj_ref, last_ref, flag_ref, q_ref, k_ref, vt_ref, u_ref, o_ref, m_sc, acc_sc):
    p = pl.program_id(1)
    qi = qi_ref[p]
    kj = kj_ref[p]
    bounded = flag_ref[pl.program_id(0)] == 1

    @pl.when(kj == 0)
    def _():
        m_sc[...] = jnp.full_like(m_sc, -jnp.inf)
        acc_sc[...] = jnp.zeros_like(acc_sc)

    all_visible = kj * TK + (TK - 1) <= qi * TQ

    def run(visible_fn, use_bound):
        def body():
            _attn_block(q_ref, k_ref, vt_ref, u_ref, m_sc, acc_sc, visible_fn(), use_bound)
        return body

    def causal_mask():
        kpos = kj * TK + lax.broadcasted_iota(jnp.int32, (TK, TQ), 0)
        qpos = qi * TQ + lax.broadcasted_iota(jnp.int32, (TK, TQ), 1)
        return kpos <= qpos

    partly = jnp.logical_not(all_visible)
    unbounded = jnp.logical_not(bounded)
    pl.when(all_visible & bounded)(run(lambda: None, True))
    if TQ == TK:
        pl.when(partly & bounded)(lambda: _attn_diag_block(q_ref, k_ref, vt_ref, u_ref, acc_sc))
    else:
        pl.when(partly & bounded)(run(causal_mask, True))
    pl.when(all_visible & unbounded)(run(lambda: None, False))
    pl.when(partly & unbounded)(run(causal_mask, False))

    @pl.when(last_ref[p] == 1)
    def _():
        out_t = jnp.concatenate([acc_sc[h, :MLA_V] / acc_sc[h, MLA_V:MLA_V + 1] for h in range(MLA_HEADS)], axis=0)
        o_ref[0] = out_t.T.astype(o_ref.dtype)


def _attention(q, k, vt, u, flag):
    b, _, s, _ = k.shape
    nq = s // TQ
    pairs = [(i, j) for i in range(nq) for j in range((i * TQ + TQ - 1) // TK + 1)]
    qi = jnp.asarray(np.array([p[0] for p in pairs], np.int32))
    kj = jnp.asarray(np.array([p[1] for p in pairs], np.int32))
    last = jnp.asarray(np.array([int(p[1] == (p[0] * TQ + TQ - 1) // TK) for p in pairs], np.int32))
    grid_spec = pltpu.PrefetchScalarGridSpec(
        num_scalar_prefetch=4,
        grid=(b, len(pairs)),
        in_specs=[pl.BlockSpec((1, MLA_HEADS, LANES, TQ), lambda i, p, qi, kj, *_: (i, 0, 0, qi[p])),
                  pl.BlockSpec((1, MLA_HEADS, TK, LANES), lambda i, p, qi, kj, *_: (i, 0, kj[p], 0)),
                  pl.BlockSpec((1, MLA_HEADS * VT_ROWS, TK), lambda i, p, qi, kj, *_: (i, 0, kj[p])),
                  pl.BlockSpec((1, MLA_HEADS, SUBLANES, TQ), lambda i, p, qi, kj, *_: (i, 0, 0, qi[p]))],
        out_specs=pl.BlockSpec((1, TQ, MLA_WIDTH), lambda i, p, qi, kj, *_: (i, qi[p], 0)),
        scratch_shapes=[pltpu.VMEM((MLA_HEADS, 1, TQ), F32),
                        pltpu.VMEM((MLA_HEADS, VT_ROWS, TQ), F32)],
    )
    return pl.pallas_call(
        _attn_kernel,
        grid_spec=grid_spec,
        out_shape=jax.ShapeDtypeStruct((b, s, MLA_WIDTH), BF16),
        compiler_params=_params("parallel", "arbitrary"),
        name="mla_attention",
    )(qi, kj, last, flag, q, k, vt, u)


POOL_HALO = 16


def _post_kernel(x_ref, xa_ref, yb_ref, yc_ref, wp_ref, ps_ref, wo_ref, g_ref, wu_ref, wd_ref, o_ref,
                 halo_sc, ext_sc):
    ts = x_ref.shape[1]
    s_idx = pl.program_id(1)

    @pl.when(s_idx == 0)
    def _():
        halo_sc[...] = jnp.zeros_like(halo_sc)

    xa = xa_ref[0]
    ext_sc[0:POOL_HALO, :] = halo_sc[...]
    ext_sc[POOL_HALO:, :] = xa
    halo_sc[...] = xa_ref[0, ts - POOL_HALO:ts, :]

    assert POOL_WINDOWS == tuple(2 << g for g in range(POOL_GROUPS)) and POOL_WINDOWS[-1] <= POOL_HALO
    e = ext_sc[...]
    sums = []
    for g in range(POOL_GROUPS):
        e = e + pltpu.roll(e, 1 << g, 0)
        sums.append(e[POOL_HALO:])
    lane = lax.broadcasted_iota(jnp.int32, (ts, POOL_WIDTH), 1)
    grp = lane >> (POOL_GROUP_DIM.bit_length() - 1)
    win = jnp.left_shift(2, grp)
    t = s_idx * ts + lax.broadcasted_iota(jnp.int32, (ts, POOL_WIDTH), 0)
    count = jnp.minimum(t + 1, win).astype(F32)
    pooled = sums[POOL_GROUPS - 1]
    for g in range(POOL_GROUPS - 2, -1, -1):
        pooled = jnp.where(grp == g, sums[g], pooled)
    ya = _dot((pooled / count - xa).astype(BF16), wp_ref[...]) * ps_ref[...]

    acc = x_ref[0] + _dot(ya.astype(BF16), wo_ref[0:POOL_WIDTH, :])
    acc = acc + _dot(yb_ref[0].astype(BF16), wo_ref[POOL_WIDTH:POOL_WIDTH + DN_WIDTH, :])
    x1 = acc + _dot(yc_ref[0].astype(BF16), wo_ref[POOL_WIDTH + DN_WIDTH:, :])

    h = _rms_rows(x1, g_ref[...]).astype(BF16)
    u = jnp.maximum(_dot(h, wu_ref[...]), 0.0)
    o_ref[0] = x1 + _dot((u * u).astype(BF16), wd_ref[...])


def _post(x, xa, yb, yc, wp_bd, pool_scale, w_out, gain, w_up, w_down):
    b, s, _ = x.shape
    ts = TM_PROJ

    def tok(n):
        return pl.BlockSpec((1, ts, n), lambda i, j: (i, j, 0))

    return pl.pallas_call(
        _post_kernel,
        grid=(b, s // ts),
        in_specs=[tok(D_MODEL), tok(POOL_WIDTH), tok(DN_WIDTH), tok(MLA_WIDTH),
                  _const_spec((POOL_WIDTH, POOL_WIDTH)), _const_spec((1, POOL_WIDTH)),
                  _const_spec((D_MODEL, D_MODEL)), _const_spec((1, D_MODEL)),
                  _const_spec((D_MODEL, D_FF)), _const_spec((D_FF, D_MODEL))],
        out_specs=tok(D_MODEL),
        out_shape=jax.ShapeDtypeStruct((b, s, D_MODEL), F32),
        scratch_shapes=[pltpu.VMEM((POOL_HALO, POOL_WIDTH), F32),
                        pltpu.VMEM((ts + POOL_HALO, POOL_WIDTH), F32)],
        compiler_params=_params("parallel", "arbitrary"),
        name="outproj_pool_mlp",
    )(x, xa, yb, yc, wp_bd, pool_scale, w_out, gain, w_up, w_down)


def _head_block(nope, pe):
    z16 = jnp.zeros(pe.shape[:-1] + (HALF_ROPE,), pe.dtype)
    return jnp.concatenate([nope, pe[..., :HALF_ROPE], z16, pe[..., HALF_ROPE:], z16], axis=-1)


def _prep_layer(w_in, pool_w, pool_scale, dn_conv, dn_a_log, dn_dt_bias, dn_norm,
                q_a_norm, w_q_b, kv_a_norm, w_kv_b, q_norm, k_norm, w_out, w_up, w_down):
    offs = np.cumsum((0,) + IN_SPLITS)
    seg = [w_in[:, offs[i]:offs[i + 1]] for i in range(len(IN_SPLITS))]
    w_xa, w_qkv, w_z, w_b, w_a, w_cq, w_ckv, w_kpe = seg
    d = w_in.shape[0]
    z64 = jnp.zeros((d, MLA_NOPE), w_in.dtype)
    w_misc = jnp.concatenate([w_b, w_a, jnp.zeros((d, MLA_NOPE - 2 * DN_HEADS), w_in.dtype),
                              _head_block(z64, w_kpe)[:, MLA_NOPE:]], axis=1)
    w_cat = jnp.concatenate([w_qkv, w_xa, w_z, w_cq, w_ckv, w_misc], axis=1).astype(BF16)

    lane_pad = jnp.zeros((LANES - 2 * DN_HEADS,), F32)
    alog_l = jnp.concatenate([jnp.zeros((DN_HEADS,), F32), dn_a_log, lane_pad])[None]
    dtb_l = jnp.concatenate([jnp.zeros((DN_HEADS,), F32), dn_dt_bias, lane_pad])[None]

    wq = w_q_b.reshape(Q_LORA, MLA_HEADS, MLA_QK_DIM)
    wq = _head_block(wq[..., :MLA_NOPE], wq[..., MLA_NOPE:]).reshape(Q_LORA, MLA_HEADS * LANES).astype(BF16)
    wkv = w_kv_b.reshape(KV_LORA, MLA_HEADS, MLA_NOPE + MLA_V)
    wk = _head_block(wkv[..., :MLA_NOPE], jnp.zeros((KV_LORA, MLA_HEADS, MLA_ROPE), F32))
    wk = wk.reshape(KV_LORA, MLA_HEADS * LANES).astype(BF16)
    wv = jnp.concatenate([wkv[..., MLA_NOPE:], jnp.zeros((KV_LORA, MLA_HEADS, VT_ROWS - MLA_V), F32)], axis=-1)
    wvt = wv.reshape(KV_LORA, MLA_HEADS * VT_ROWS).T.astype(BF16)
    qg = _head_block(q_norm[:MLA_NOPE], q_norm[MLA_NOPE:])[None]
    kg = _head_block(k_norm[:MLA_NOPE], jnp.zeros((MLA_ROPE,), F32))[None]
    kpg = _head_block(jnp.zeros((MLA_NOPE,), F32), k_norm[MLA_NOPE:])[None]

    wp_bd = jax.scipy.linalg.block_diag(*[pool_w[g] for g in range(POOL_GROUPS)]).astype(BF16)
    return dict(w_cat=w_cat, alog_l=alog_l, dtb_l=dtb_l, conv_w=dn_conv, dn_norm=dn_norm[None],
                qag=q_a_norm[None], kvag=kv_a_norm[None], wq=wq, wk=wk, wvt=wvt, qg=qg, kg=kg, kpg=kpg,
                wp_bd=wp_bd, pool_scale=pool_scale[None], w_out=w_out.astype(BF16),
                w_up=w_up.astype(BF16), w_down=w_down.astype(BF16))


def kernel(x, positions, attn_norm, w_in, pool_w, pool_scale, dn_conv, dn_a_log, dn_dt_bias, dn_norm,
           mla_q_a_norm, mla_w_q_b, mla_kv_a_norm, mla_w_kv_b, mla_q_norm, mla_k_norm,
           w_out, mlp_norm, w_up, w_down):
    b, s, d = x.shape
    depth = w_in.shape[0]
    inv_freq = ROPE_THETA ** (-jnp.arange(0, MLA_ROPE, 2, dtype=F32) / MLA_ROPE)
    freq_lanes = _head_block(jnp.zeros((MLA_NOPE,), F32), jnp.concatenate([inv_freq, inv_freq]))[None]
    cos, sina, sinb = _rope_tables(positions, freq_lanes)

    for l in range(depth):
        p = _prep_layer(w_in[l], pool_w[l], pool_scale[l], dn_conv[l], dn_a_log[l], dn_dt_bias[l], dn_norm[l],
                        mla_q_a_norm[l], mla_w_q_b[l], mla_kv_a_norm[l], mla_w_kv_b[l], mla_q_norm[l],
                        mla_k_norm[l], w_out[l], w_up[l], w_down[l])
        qkv, xa, z, cq, ckv, misc = _inproj(x, attn_norm[l][None], p["w_cat"], p["conv_w"])
        y_b, q, k, vt, qsq, ksq = _deltanet_mla_prep(
            qkv, z, misc, p["alog_l"], p["dtb_l"], p["dn_norm"], cq, ckv, cos, sina, sinb,
            p["qag"], p["kvag"], p["wq"], p["wk"], p["wvt"], p["qg"], p["kg"], p["kpg"])
        u, flag = _score_bounds(qsq, ksq)
        y_c = _attention(q, k, vt, u, flag[:, 0, 0])
        x = _post(x, xa, y_b, y_c, p["wp_bd"], p["pool_scale"], p["w_out"], mlp_norm[l][None], p["w_up"], p["w_down"])
    return x
```

```python
import functools
import math

import jax
import jax.numpy as jnp
import numpy as np
from jax import lax
from jax.experimental import pallas as pl
from jax.experimental.pallas import tpu as pltpu

F32 = jnp.float32
BF16 = jnp.bfloat16

D_MODEL = 1024
POOL_GROUPS = 4
POOL_GROUP_DIM = 64
POOL_WIDTH = POOL_GROUPS * POOL_GROUP_DIM
POOL_WINDOWS = (2, 4, 8, 16)
DN_HEADS = 4
DN_HEAD_DIM = 128
DN_WIDTH = DN_HEADS * DN_HEAD_DIM
DN_CONV = 4
DN_CHUNK = 64
MLA_HEADS = 4
MLA_NOPE = 64
MLA_ROPE = 32
MLA_QK_DIM = MLA_NOPE + MLA_ROPE
MLA_V = 64
MLA_WIDTH = MLA_HEADS * MLA_V
Q_LORA = 256
KV_LORA = 128
ROPE_THETA = 10000.0
D_FF = 4 * D_MODEL
EPS = 1e-6
IN_SPLITS = (POOL_WIDTH, 3 * DN_WIDTH, DN_WIDTH, DN_HEADS, DN_HEADS, Q_LORA, KV_LORA, MLA_ROPE)

LANES = 128
SUBLANES = 8
MXU_COLS = 256
VMEM_LIMIT_BYTES = 56 * 1024 * 1024

TM_INPROJ = 256
TM_PROJ = 512
TM_PREP = 1024
TM_ROPE = 4096
DN_TILE = 256
DN_PAIR = 2 * DN_CHUNK
TQ = 1024
TK = 1024
HALF_ROPE = MLA_ROPE // 2
VT_ROWS = MLA_V + 16
SCORE_BOUND_SLACK = 1.02
SCORE_BOUND_LIMIT = 30.0
PE1_LANE = MLA_NOPE
PE2_LANE = MLA_NOPE + 2 * HALF_ROPE


def _dot(a, b):
    return jnp.dot(a, b, preferred_element_type=F32)


def _dot_nt(a, b):
    return lax.dot_general(a, b, (((1,), (1,)), ((), ())), preferred_element_type=F32)


def _rms_rows(x, gain):
    return x * lax.rsqrt(jnp.mean(x * x, axis=-1, keepdims=True) + EPS) * gain


def _split3(x):
    x1 = x.astype(BF16)
    r1 = x - x1.astype(F32)
    x2 = r1.astype(BF16)
    r2 = r1 - x2.astype(F32)
    return x1, x2, r2.astype(BF16)


def _sigmoid(x):
    return 1.0 / (1.0 + jnp.exp(-x))


def _params(*sem):
    return pltpu.CompilerParams(dimension_semantics=sem, vmem_limit_bytes=VMEM_LIMIT_BYTES)


def _const_spec(shape):
    nd = len(shape)
    return pl.BlockSpec(shape, lambda *_: (0,) * nd, pipeline_mode=pl.Buffered(1))


IN_SEGS = (3 * DN_WIDTH, POOL_WIDTH, DN_WIDTH, Q_LORA, KV_LORA, LANES)
CONV_HALO = SUBLANES


def _inproj_kernel(x_ref, g_ref, w_ref, cw_ref, qkv_ref, *rest):
    o_refs, (halo_sc, ext_sc) = rest[:-2], rest[-2:]
    tm = x_ref.shape[1]
    nqkv = 3 * DN_WIDTH
    D = DN_HEAD_DIM

    @pl.when(pl.program_id(1) == 0)
    def _():
        halo_sc[...] = jnp.zeros_like(halo_sc)

    h = _rms_rows(x_ref[0], g_ref[...]).astype(BF16)
    ext_sc[0:CONV_HALO, :] = halo_sc[...]

    def conv_cols(c0):
        e = ext_sc[:, c0:c0 + D]
        cw = cw_ref[:, c0:c0 + D]
        e1 = pltpu.roll(e, 1, 0)
        near = cw[3:4] * e + cw[2:3] * e1
        far = cw[1:2] * e + cw[0:1] * e1
        acc = (near + pltpu.roll(far, 2, 0))[CONV_HALO:]
        y = acc * _sigmoid(acc)
        if c0 < 2 * DN_WIDTH:
            y = y * lax.rsqrt(jnp.sum(y * y, axis=-1, keepdims=True) + EPS)
            if c0 < DN_WIDTH:
                y = y * (D ** -0.5)
        qkv_ref[0, :, c0:c0 + D] = y

    pieces = []
    off = nqkv
    for o_ref in o_refs:
        n = o_ref.shape[-1]
        pieces += [(o_ref, c, off + c, LANES) for c in range(0, n, LANES)]
        off += n
    per_dot = MXU_COLS // LANES
    others = [pieces[i:i + per_dot] for i in range(0, len(pieces), per_dot)]
    def qkv_cols(c0):
        ext_sc[CONV_HALO:, c0:c0 + MXU_COLS] = _dot(h, w_ref[:, c0:c0 + MXU_COLS])

    nsteps = nqkv // MXU_COLS
    qkv_cols(0)
    for i in range(nsteps):
        if i + 1 < nsteps:
            qkv_cols((i + 1) * MXU_COLS)
        for group in others[i::nsteps]:
            w0 = group[0][2]
            res = _dot(h, w_ref[:, w0:w0 + LANES * len(group)])
            for o_ref, c, woff, n in group:
                o_ref[0, :, c:c + n] = res[:, woff - w0:woff - w0 + n]
        for c in range(i * MXU_COLS, (i + 1) * MXU_COLS, D):
            conv_cols(c)
    halo_sc[...] = ext_sc[tm:tm + CONV_HALO, :]


def _inproj(x, gain, w_cat, conv_w):
    b, s, _ = x.shape
    tm = TM_INPROJ
    n_all = sum(IN_SEGS)
    return pl.pallas_call(
        _inproj_kernel,
        grid=(b, s // tm),
        in_specs=[pl.BlockSpec((1, tm, D_MODEL), lambda i, j: (i, j, 0)),
                  _const_spec((1, D_MODEL)),
                  _const_spec((D_MODEL, n_all)),
                  _const_spec((DN_CONV, 3 * DN_WIDTH))],
        out_specs=[pl.BlockSpec((1, tm, n), lambda i, j: (i, j, 0)) for n in IN_SEGS],
        out_shape=[jax.ShapeDtypeStruct((b, s, n), F32) for n in IN_SEGS],
        scratch_shapes=[pltpu.VMEM((CONV_HALO, 3 * DN_WIDTH), F32),
                        pltpu.VMEM((tm + CONV_HALO, 3 * DN_WIDTH), F32)],
        compiler_params=_params("parallel", "arbitrary"),
        name="inproj",
    )(x, gain, w_cat, conv_w)


def _dn_kernel(qkv_ref, z_ref, misc_ref, alog_ref, dtb_ref, ng_ref, y_ref, state_sc):
    nb = qkv_ref.shape[0]
    L = DN_TILE
    C = DN_CHUNK
    nchunk = L // C
    D = DN_HEAD_DIM
    s_idx = pl.program_id(0)
    chains = [(b, h) for b in range(nb) for h in range(DN_HEADS)]

    @pl.when(s_idx == 0)
    def _():
        state_sc[...] = jnp.zeros_like(state_sc)

    def chunk_masks(n):
        ri = lax.broadcasted_iota(jnp.int32, (n, n), 0)
        ci = lax.broadcasted_iota(jnp.int32, (n, n), 1)
        shift = DN_CHUNK.bit_length() - 1
        same = (ri >> shift) == (ci >> shift)
        return same, same & (ci <= ri), same & (ci < ri), ri == ci

    same_l, causal_l, _, _ = chunk_masks(L)
    cum_mat = jnp.concatenate([causal_l.astype(BF16), same_l.astype(BF16)], axis=0)
    P = DN_PAIR
    nblk = L // P
    _, causal_bd, strict_bd, diag = chunk_masks(P)
    eye = diag.astype(F32)

    beta_all, gcum_all, glast_all, gcum_t = [], [], [], []
    for b in range(nb):
        misc = misc_ref[b]
        beta_all.append(_sigmoid(misc))
        sp_in = misc + dtb_ref[...]
        softplus = jnp.maximum(sp_in, 0.0) + jnp.log1p(jnp.exp(-jnp.abs(sp_in)))
        g_all = -jnp.exp(alog_ref[...]) * softplus
        g1, g2, g3 = _split3(g_all)
        cum = _dot(cum_mat, g1) + _dot(cum_mat, g2) + _dot(cum_mat, g3)
        gcum_all.append(cum[:L])
        glast_all.append(cum[L:])
        gcum_t.append(cum[:L].T)

    def stage1(b, h):
        a_mats, attns = [], []
        lane = DN_HEADS + h
        gc_col = gcum_all[b][:, lane:lane + 1]
        gl_col = glast_all[b][:, lane:lane + 1]
        gc_row = gcum_t[b][lane:lane + 1, :]
        beta = beta_all[b][:, h:h + 1]
        qn = qkv_ref[b, :, h * D:(h + 1) * D]
        kn = qkv_ref[b, :, DN_WIDTH + h * D:DN_WIDTH + (h + 1) * D]
        v = qkv_ref[b, :, 2 * DN_WIDTH + h * D:2 * DN_WIDTH + (h + 1) * D]
        kb = kn * beta
        e_col = jnp.exp(gc_col)
        kn16 = kn.astype(BF16)
        kb16 = kb.astype(BF16)
        qn16 = qn.astype(BF16)
        for j in range(nblk):
            rows = slice(j * P, (j + 1) * P)
            decay = jnp.exp(jnp.where(causal_bd, gc_col[rows] - gc_row[:, rows], 0.0))
            gram = _dot_nt(jnp.concatenate([kb16[rows], qn16[rows]], axis=0), kn16[rows])
            a_mats.append(jnp.where(strict_bd, gram[:P] * decay, 0.0))
            attns.append(jnp.where(causal_bd, gram[P:] * decay, 0.0).astype(BF16))
        return dict(a=a_mats, attn=attns,
                    rhs=jnp.concatenate([v * beta, kb * e_col], axis=1).astype(BF16),
                    qd=(qn * e_col).astype(BF16),
                    kdt=(kn * jnp.exp(gl_col - gc_col)).T.astype(BF16),
                    gdec=jnp.exp(jnp.broadcast_to(gl_col, (L, D))))

    def stage2(group):
        a_mats = [a for ch in group for a in ch["a"]]
        xps = [(-a).astype(BF16) for a in a_mats]
        t_invs = [eye - a for a in a_mats]
        xps = [_dot(xp, xp).astype(BF16) for xp in xps]
        for _ in range(4):
            prods = [_dot(jnp.concatenate([t.astype(BF16), xp], axis=0), xp) for t, xp in zip(t_invs, xps)]
            t_invs = [t + pr[:P] for t, pr in zip(t_invs, prods)]
            xps = [pr[P:].astype(BF16) for pr in prods]
        t_invs = [t + _dot(t.astype(BF16), xp) for t, xp in zip(t_invs, xps)]
        for i, ch in enumerate(group):
            ch["uw"] = [_dot(t_invs[i * nblk + j].astype(BF16), ch["rhs"][j * P:(j + 1) * P])
                        for j in range(nblk)]

    chs = [stage1(b, h) for b, h in chains]
    stage2(chs)

    states = [state_sc[b, h] for b, h in chains]
    o_parts = [[] for _ in chains]
    zeros_c = jnp.zeros((C, D), BF16)
    for c in range(nchunk):
        r0 = c * C
        j, half = divmod(c, P // C)
        p0 = half * C
        rs = [_dot(jnp.concatenate([ch["uw"][j][p0:p0 + C, D:].astype(BF16), ch["qd"][r0:r0 + C]], axis=0),
                   states[i].astype(BF16)) for i, ch in enumerate(chs)]
        for i, ch in enumerate(chs):
            v_new = (ch["uw"][j][p0:p0 + C, :D] - rs[i][:C]).astype(BF16)
            v_blk = jnp.concatenate([zeros_c] * half + [v_new] + [zeros_c] * (P // C - 1 - half), axis=0)
            lhs = jnp.concatenate([ch["attn"][j][p0:p0 + C, :], ch["kdt"][:, j * P:(j + 1) * P]], axis=0)
            m2 = _dot(lhs, v_blk)
            o_parts[i].append(rs[i][C:] + m2[:C])
            states[i] = states[i] * ch["gdec"][r0:r0 + 1, :] + m2[C:]

    for i, (b, h) in enumerate(chains):
        state_sc[b, h] = states[i]
        o = jnp.concatenate(o_parts[i], axis=0)
        zh = z_ref[b, :, h * D:(h + 1) * D]
        y_ref[b, :, h * D:(h + 1) * D] = (_rms_rows(o, ng_ref[...]) * (zh * _sigmoid(zh))).astype(y_ref.dtype)


def _deltanet(qkv, z, misc, alog_l, dtb_l, norm_gain):
    b, s, _ = qkv.shape
    L = DN_TILE
    return pl.pallas_call(
        _dn_kernel,
        grid=(s // L,),
        in_specs=[pl.BlockSpec((b, L, 3 * DN_WIDTH), lambda j: (0, j, 0)),
                  pl.BlockSpec((b, L, DN_WIDTH), lambda j: (0, j, 0)),
                  pl.BlockSpec((b, L, LANES), lambda j: (0, j, 0)),
                  _const_spec((1, LANES)),
                  _const_spec((1, LANES)),
                  _const_spec((1, DN_HEAD_DIM))],
        out_specs=pl.BlockSpec((b, L, DN_WIDTH), lambda j: (0, j, 0)),
        out_shape=jax.ShapeDtypeStruct((b, s, DN_WIDTH), BF16),
        scratch_shapes=[pltpu.VMEM((b, DN_HEADS, DN_HEAD_DIM, DN_HEAD_DIM), F32)],
        compiler_params=_params("arbitrary"),
        name="deltanet",
    )(qkv, z, misc, alog_l, dtb_l, norm_gain)


def _rope_kernel(pos_ref, freq_ref, cos_ref, sina_ref, sinb_ref):
    tm = pos_ref.shape[1]
    half = tm // 2
    fold = LANES // 2
    lane = lax.broadcasted_iota(jnp.int32, (half, LANES), 1)
    upper = lane >= fold
    freq = freq_ref[...]
    freq2 = freq + pltpu.roll(freq, fold, 1)
    pos = jnp.where(upper, pos_ref[0, 0:half].astype(F32), pos_ref[0, half:tm].astype(F32))
    ang = pos * freq2
    cos2 = jnp.cos(ang)
    sin2 = jnp.sin(ang)
    pe1 = (lane >= PE1_LANE) & (lane < PE1_LANE + HALF_ROPE)
    pe2 = (lane >= PE2_LANE) & (lane < PE2_LANE + HALF_ROPE)
    for rows, cos, sin in ((slice(0, half), cos2, sin2),
                           (slice(half, tm), pltpu.roll(cos2, fold, 1), pltpu.roll(sin2, fold, 1))):
        cos_ref[0, rows] = jnp.where(lane < MLA_NOPE, 1.0, jnp.where(pe1 | pe2, cos, 0.0))
        sina_ref[0, rows] = jnp.where(pe2, sin, 0.0)
        sinb_ref[0, rows] = jnp.where(pe1, -sin, 0.0)


def _rope_tables(positions, freq_lanes):
    b, s = positions.shape
    tm = TM_ROPE
    spec = pl.BlockSpec((1, tm, LANES), lambda i, j: (i, j, 0))
    return pl.pallas_call(
        _rope_kernel,
        grid=(b, s // tm),
        in_specs=[pl.BlockSpec((1, tm, 1), lambda i, j: (i, j, 0)), _const_spec((1, LANES))],
        out_specs=[spec, spec, spec],
        out_shape=[jax.ShapeDtypeStruct((b, s, LANES), F32)] * 3,
        compiler_params=_params("parallel", "parallel"),
        name="rope_tables",
    )(positions.reshape(b, s, 1), freq_lanes)


def _rope(x, cos, sina, sinb):
    return x * cos + pltpu.roll(x, 2 * HALF_ROPE, 1) * sina + pltpu.roll(x, LANES - 2 * HALF_ROPE, 1) * sinb


def _mla_prep_kernel(cq_ref, ckv_ref, misc_ref, cos_ref, sina_ref, sinb_ref,
                     qag_ref, kvag_ref, wq_ref, wk_ref, wvt_ref, qg_ref, kg_ref, kpg_ref,
                     q_ref, k_ref, vt_ref, qsq_ref, ksq_ref):
    tm = cq_ref.shape[1]
    ones8 = jnp.ones((SUBLANES, LANES), BF16)
    eye = (lax.broadcasted_iota(jnp.int32, (LANES, LANES), 0)
           == lax.broadcasted_iota(jnp.int32, (LANES, LANES), 1)).astype(BF16)

    def row_sq_norms(x16):
        xf = x16.astype(F32)
        return _dot_nt(ones8, (xf * xf).astype(BF16))

    cos = cos_ref[0]
    sina = sina_ref[0]
    sinb = sinb_ref[0]
    lane = lax.broadcasted_iota(jnp.int32, (tm, LANES), 1)
    is_nope = lane < MLA_NOPE

    cqn = _rms_rows(cq_ref[0], qag_ref[...]).astype(BF16)
    ckvn = _rms_rows(ckv_ref[0], kvag_ref[...]).astype(BF16)
    qf = _dot(cqn, wq_ref[...])
    kf = _dot(ckvn, wk_ref[...])
    vt = _dot_nt(wvt_ref[...], ckvn)
    vrow = lax.broadcasted_iota(jnp.int32, vt.shape, 0)
    ones_row = functools.reduce(jnp.logical_or, [vrow == h * VT_ROWS + MLA_V for h in range(MLA_HEADS)])
    vt_ref[0] = jnp.where(ones_row, 1.0, vt).astype(BF16)

    kp = jnp.where(is_nope, 0.0, misc_ref[0])
    kp_ms = jnp.sum(kp * kp, axis=-1, keepdims=True) * (1.0 / MLA_ROPE)
    kp = _rope(kp * lax.rsqrt(kp_ms + EPS) * kpg_ref[...], cos, sina, sinb)

    scale = MLA_QK_DIM ** -0.5 * math.log2(math.e)
    for h in range(MLA_HEADS):
        xq = qf[:, h * LANES:(h + 1) * LANES]
        sq = xq * xq
        ms_n = jnp.sum(jnp.where(is_nope, sq, 0.0), axis=-1, keepdims=True) * (1.0 / MLA_NOPE)
        ms_p = jnp.sum(jnp.where(is_nope, 0.0, sq), axis=-1, keepdims=True) * (1.0 / MLA_ROPE)
        inv = jnp.where(is_nope, lax.rsqrt(ms_n + EPS), lax.rsqrt(ms_p + EPS))
        qh = _rope(xq * inv * qg_ref[...], cos, sina, sinb) * scale
        qt = _dot_nt(eye, qh.astype(BF16))
        q_ref[0, h] = qt.astype(BF16)
        qsq_ref[0, h] = jnp.broadcast_to(jnp.sum(qt * qt, axis=0, keepdims=True), (SUBLANES, tm))

        xk = kf[:, h * LANES:(h + 1) * LANES]
        ms_k = jnp.sum(xk * xk, axis=-1, keepdims=True) * (1.0 / MLA_NOPE)
        kh = xk * lax.rsqrt(ms_k + EPS) * kg_ref[...] + kp
        k16 = kh.astype(BF16)
        k_ref[0, h] = k16
        ksq_ref[0, h] = row_sq_norms(k16)


def _mla_prep(cq, ckv, misc, cos, sina, sinb, qag, kvag, wq, wk, wvt, qg, kg, kpg):
    b, s, _ = cq.shape
    tm = TM_PREP
    hl = MLA_HEADS * LANES

    def tok(n):
        return pl.BlockSpec((1, tm, n), lambda i, j: (i, j, 0))

    return pl.pallas_call(
        _mla_prep_kernel,
        grid=(b, s // tm),
        in_specs=[tok(Q_LORA), tok(KV_LORA), tok(LANES), tok(LANES), tok(LANES), tok(LANES),
                  _const_spec((1, Q_LORA)), _const_spec((1, KV_LORA)),
                  _const_spec((Q_LORA, hl)), _const_spec((KV_LORA, hl)), _const_spec((MLA_HEADS * VT_ROWS, KV_LORA)),
                  _const_spec((1, LANES)), _const_spec((1, LANES)), _const_spec((1, LANES))],
        out_specs=[pl.BlockSpec((1, MLA_HEADS, LANES, tm), lambda i, j: (i, 0, 0, j)),
                   pl.BlockSpec((1, MLA_HEADS, tm, LANES), lambda i, j: (i, 0, j, 0)),
                   pl.BlockSpec((1, MLA_HEADS * VT_ROWS, tm), lambda i, j: (i, 0, j)),
                   pl.BlockSpec((1, MLA_HEADS, SUBLANES, tm), lambda i, j: (i, 0, 0, j)),
                   pl.BlockSpec((1, MLA_HEADS, SUBLANES, tm), lambda i, j: (i, 0, 0, j))],
        out_shape=[jax.ShapeDtypeStruct((b, MLA_HEADS, LANES, s), BF16),
                   jax.ShapeDtypeStruct((b, MLA_HEADS, s, LANES), BF16),
                   jax.ShapeDtypeStruct((b, MLA_HEADS * VT_ROWS, s), BF16),
                   jax.ShapeDtypeStruct((b, MLA_HEADS, SUBLANES, s), F32),
                   jax.ShapeDtypeStruct((b, MLA_HEADS, SUBLANES, s), F32)],
        compiler_params=_params("parallel", "parallel"),
        name="mla_prep",
    )(cq, ckv, misc, cos, sina, sinb, qag, kvag, wq, wk, wvt, qg, kg, kpg)


def _bounds_kernel(qsq_ref, ksq_ref, u_ref, flag_ref):
    worst = None
    for h in range(MLA_HEADS):
        kmax = jnp.max(ksq_ref[0, h], axis=-1, keepdims=True)
        u = jnp.sqrt(qsq_ref[0, h] * kmax) * SCORE_BOUND_SLACK
        u_ref[0, h] = u
        umax = jnp.max(u, axis=-1, keepdims=True)
        worst = umax if worst is None else jnp.maximum(worst, umax)
    flag_ref[0] = jnp.broadcast_to((worst <= SCORE_BOUND_LIMIT).astype(jnp.int32), flag_ref.shape[1:])


def _score_bounds(qsq, ksq):
    b, h, r, s = qsq.shape
    spec = pl.BlockSpec((1, h, r, s), lambda i: (i, 0, 0, 0))
    return pl.pallas_call(
        _bounds_kernel,
        grid=(b,),
        in_specs=[spec, spec],
        out_specs=[spec, pl.BlockSpec((1, SUBLANES, LANES), lambda i: (i, 0, 0))],
        out_shape=[jax.ShapeDtypeStruct((b, h, r, s), F32), jax.ShapeDtypeStruct((b, SUBLANES, LANES), jnp.int32)],
        compiler_params=_params("parallel"),
        name="score_bounds",
    )(qsq, ksq)


def _attn_block(q_ref, k_ref, vt_ref, u_ref, m_sc, acc_sc, visible, bounded):
    scores = [_dot(k_ref[0, 0], q_ref[0, 0])]
    for h in range(MLA_HEADS):
        if h + 1 < MLA_HEADS:
            scores.append(_dot(k_ref[0, h + 1], q_ref[0, h + 1]))
        st = scores[h]
        if visible is not None:
            st = jnp.where(visible, st, -jnp.inf)
        vt = vt_ref[0, h * VT_ROWS:(h + 1) * VT_ROWS, :]
        if bounded:
            pt = jnp.exp2(st - u_ref[0, h, 0:1, :])
            acc_sc[h] = acc_sc[h] + _dot(vt, pt.astype(BF16))
        else:
            m_prev = m_sc[h]
            m_new = jnp.maximum(m_prev, jnp.max(st, axis=0, keepdims=True))
            alpha = jnp.exp2(m_prev - m_new)
            pt = jnp.exp2(st - m_new)
            acc_sc[h] = alpha * acc_sc[h] + _dot(vt, pt.astype(BF16))
            m_sc[h] = m_new


def _attn_diag_block(q_ref, k_ref, vt_ref, u_ref, acc_sc):
    hk = TK // 2
    r = lax.broadcasted_iota(jnp.int32, (hk, TQ), 0)
    c = lax.broadcasted_iota(jnp.int32, (hk, TQ), 1)
    vis_old = r <= c
    vis_new = vis_old[:, :hk]

    def scores(h):
        return (_dot(k_ref[0, h, 0:hk, :], q_ref[0, h]),
                _dot(k_ref[0, h, hk:, :], q_ref[0, h, :, hk:]))

    nxt = scores(0)
    for h in range(MLA_HEADS):
        s_old, s_new = nxt
        if h + 1 < MLA_HEADS:
            nxt = scores(h + 1)
        u = u_ref[0, h, 0:1, :]
        p_old = jnp.exp2(jnp.where(vis_old, s_old, -jnp.inf) - u)
        p_new = jnp.exp2(jnp.where(vis_new, s_new, -jnp.inf) - u[:, hk:])
        vt = vt_ref[0, h * VT_ROWS:(h + 1) * VT_ROWS, :]
        acc_sc[h] = acc_sc[h] + _dot(vt[:, :hk], p_old.astype(BF16))
        acc_sc[h, :, hk:] = acc_sc[h, :, hk:] + _dot(vt[:, hk:], p_new.astype(BF16))


def _attn_kernel(qi_ref, kj_ref, last_ref, flag_ref, q_ref, k_ref, vt_ref, u_ref, o_ref, m_sc, acc_sc):
    p = pl.program_id(1)
    qi = qi_ref[p]
    kj = kj_ref[p]
    bounded = flag_ref[pl.program_id(0)] == 1

    @pl.when(kj == 0)
    def _():
        m_sc[...] = jnp.full_like(m_sc, -jnp.inf)
        acc_sc[...] = jnp.zeros_like(acc_sc)

    all_visible = kj * TK + (TK - 1) <= qi * TQ

    def run(visible_fn, use_bound):
        def body():
            _attn_block(q_ref, k_ref, vt_ref, u_ref, m_sc, acc_sc, visible_fn(), use_bound)
        return body

    def causal_mask():
        kpos = kj * TK + lax.broadcasted_iota(jnp.int32, (TK, TQ), 0)
        qpos = qi * TQ + lax.broadcasted_iota(jnp.int32, (TK, TQ), 1)
        return kpos <= qpos

    partly = jnp.logical_not(all_visible)
    unbounded = jnp.logical_not(bounded)
    pl.when(all_visible & bounded)(run(lambda: None, True))
    if TQ == TK:
        pl.when(partly & bounded)(lambda: _attn_diag_block(q_ref, k_ref, vt_ref, u_ref, acc_sc))
    else:
        pl.when(partly & bounded)(run(causal_mask, True))
    pl.when(all_visible & unbounded)(run(lambda: None, False))
    pl.when(partly & unbounded)(run(causal_mask, False))

    @pl.when(last_ref[p] == 1)
    def _():
        out_t = jnp.concatenate([acc_sc[h, :MLA_V] / acc_sc[h, MLA_V:MLA_V + 1] for h in range(MLA_HEADS)], axis=0)
        o_ref[0] = out_t.T.astype(o_ref.dtype)


def _attention(q, k, vt, u, flag):
    b, _, s, _ = k.shape
    nq = s // TQ
    pairs = [(i, j) for i in range(nq) for j in range((i * TQ + TQ - 1) // TK + 1)]
    qi = jnp.asarray(np.array([p[0] for p in pairs], np.int32))
    kj = jnp.asarray(np.array([p[1] for p in pairs], np.int32))
    last = jnp.asarray(np.array([int(p[1] == (p[0] * TQ + TQ - 1) // TK) for p in pairs], np.int32))
    grid_spec = pltpu.PrefetchScalarGridSpec(
        num_scalar_prefetch=4,
        grid=(b, len(pairs)),
        in_specs=[pl.BlockSpec((1, MLA_HEADS, LANES, TQ), lambda i, p, qi, kj, *_: (i, 0, 0, qi[p])),
                  pl.BlockSpec((1, MLA_HEADS, TK, LANES), lambda i, p, qi, kj, *_: (i, 0, kj[p], 0)),
                  pl.BlockSpec((1, MLA_HEADS * VT_ROWS, TK), lambda i, p, qi, kj, *_: (i, 0, kj[p])),
                  pl.BlockSpec((1, MLA_HEADS, SUBLANES, TQ), lambda i, p, qi, kj, *_: (i, 0, 0, qi[p]))],
        out_specs=pl.BlockSpec((1, TQ, MLA_WIDTH), lambda i, p, qi, kj, *_: (i, qi[p], 0)),
        scratch_shapes=[pltpu.VMEM((MLA_HEADS, 1, TQ), F32),
                        pltpu.VMEM((MLA_HEADS, VT_ROWS, TQ), F32)],
    )
    return pl.pallas_call(
        _attn_kernel,
        grid_spec=grid_spec,
        out_shape=jax.ShapeDtypeStruct((b, s, MLA_WIDTH), BF16),
        compiler_params=_params("parallel", "arbitrary"),
        name="mla_attention",
    )(qi, kj, last, flag, q, k, vt, u)


POOL_HALO = 16


def _post_kernel(x_ref, xa_ref, yb_ref, yc_ref, wp_ref, ps_ref, wo_ref, g_ref, wu_ref, wd_ref, o_ref,
                 halo_sc, ext_sc):
    ts = x_ref.shape[1]
    s_idx = pl.program_id(1)

    @pl.when(s_idx == 0)
    def _():
        halo_sc[...] = jnp.zeros_like(halo_sc)

    xa = xa_ref[0]
    ext_sc[0:POOL_HALO, :] = halo_sc[...]
    ext_sc[POOL_HALO:, :] = xa
    halo_sc[...] = xa_ref[0, ts - POOL_HALO:ts, :]

    assert POOL_WINDOWS == tuple(2 << g for g in range(POOL_GROUPS)) and POOL_WINDOWS[-1] <= POOL_HALO
    e = ext_sc[...]
    sums = []
    for g in range(POOL_GROUPS):
        e = e + pltpu.roll(e, 1 << g, 0)
        sums.append(e[POOL_HALO:])
    lane = lax.broadcasted_iota(jnp.int32, (ts, POOL_WIDTH), 1)
    grp = lane >> (POOL_GROUP_DIM.bit_length() - 1)
    win = jnp.left_shift(2, grp)
    t = s_idx * ts + lax.broadcasted_iota(jnp.int32, (ts, POOL_WIDTH), 0)
    count = jnp.minimum(t + 1, win).astype(F32)
    pooled = sums[POOL_GROUPS - 1]
    for g in range(POOL_GROUPS - 2, -1, -1):
        pooled = jnp.where(grp == g, sums[g], pooled)
    ya = _dot((pooled / count - xa).astype(BF16), wp_ref[...]) * ps_ref[...]

    acc = x_ref[0] + _dot(ya.astype(BF16), wo_ref[0:POOL_WIDTH, :])
    acc = acc + _dot(yb_ref[0].astype(BF16), wo_ref[POOL_WIDTH:POOL_WIDTH + DN_WIDTH, :])
    x1 = acc + _dot(yc_ref[0].astype(BF16), wo_ref[POOL_WIDTH + DN_WIDTH:, :])

    h = _rms_rows(x1, g_ref[...]).astype(BF16)
    u = jnp.maximum(_dot(h, wu_ref[...]), 0.0)
    o_ref[0] = x1 + _dot((u * u).astype(BF16), wd_ref[...])


def _post(x, xa, yb, yc, wp_bd, pool_scale, w_out, gain, w_up, w_down):
    b, s, _ = x.shape
    ts = TM_PROJ

    def tok(n):
        return pl.BlockSpec((1, ts, n), lambda i, j: (i, j, 0))

    return pl.pallas_call(
        _post_kernel,
        grid=(b, s // ts),
        in_specs=[tok(D_MODEL), tok(POOL_WIDTH), tok(DN_WIDTH), tok(MLA_WIDTH),
                  _const_spec((POOL_WIDTH, POOL_WIDTH)), _const_spec((1, POOL_WIDTH)),
                  _const_spec((D_MODEL, D_MODEL)), _const_spec((1, D_MODEL)),
                  _const_spec((D_MODEL, D_FF)), _const_spec((D_FF, D_MODEL))],
        out_specs=tok(D_MODEL),
        out_shape=jax.ShapeDtypeStruct((b, s, D_MODEL), F32),
        scratch_shapes=[pltpu.VMEM((POOL_HALO, POOL_WIDTH), F32),
                        pltpu.VMEM((ts + POOL_HALO, POOL_WIDTH), F32)],
        compiler_params=_params("parallel", "arbitrary"),
        name="outproj_pool_mlp",
    )(x, xa, yb, yc, wp_bd, pool_scale, w_out, gain, w_up, w_down)


def _head_block(nope, pe):
    z16 = jnp.zeros(pe.shape[:-1] + (HALF_ROPE,), pe.dtype)
    return jnp.concatenate([nope, pe[..., :HALF_ROPE], z16, pe[..., HALF_ROPE:], z16], axis=-1)


def _prep_layer(w_in, pool_w, pool_scale, dn_conv, dn_a_log, dn_dt_bias, dn_norm,
                q_a_norm, w_q_b, kv_a_norm, w_kv_b, q_norm, k_norm, w_out, w_up, w_down):
    offs = np.cumsum((0,) + IN_SPLITS)
    seg = [w_in[:, offs[i]:offs[i + 1]] for i in range(len(IN_SPLITS))]
    w_xa, w_qkv, w_z, w_b, w_a, w_cq, w_ckv, w_kpe = seg
    d = w_in.shape[0]
    z64 = jnp.zeros((d, MLA_NOPE), w_in.dtype)
    w_misc = jnp.concatenate([w_b, w_a, jnp.zeros((d, MLA_NOPE - 2 * DN_HEADS), w_in.dtype),
                              _head_block(z64, w_kpe)[:, MLA_NOPE:]], axis=1)
    w_cat = jnp.concatenate([w_qkv, w_xa, w_z, w_cq, w_ckv, w_misc], axis=1).astype(BF16)

    lane_pad = jnp.zeros((LANES - 2 * DN_HEADS,), F32)
    alog_l = jnp.concatenate([jnp.zeros((DN_HEADS,), F32), dn_a_log, lane_pad])[None]
    dtb_l = jnp.concatenate([jnp.zeros((DN_HEADS,), F32), dn_dt_bias, lane_pad])[None]

    wq = w_q_b.reshape(Q_LORA, MLA_HEADS, MLA_QK_DIM)
    wq = _head_block(wq[..., :MLA_NOPE], wq[..., MLA_NOPE:]).reshape(Q_LORA, MLA_HEADS * LANES).astype(BF16)
    wkv = w_kv_b.reshape(KV_LORA, MLA_HEADS, MLA_NOPE + MLA_V)
    wk = _head_block(wkv[..., :MLA_NOPE], jnp.zeros((KV_LORA, MLA_HEADS, MLA_ROPE), F32))
    wk = wk.reshape(KV_LORA, MLA_HEADS * LANES).astype(BF16)
    wv = jnp.concatenate([wkv[..., MLA_NOPE:], jnp.zeros((KV_LORA, MLA_HEADS, VT_ROWS - MLA_V), F32)], axis=-1)
    wvt = wv.reshape(KV_LORA, MLA_HEADS * VT_ROWS).T.astype(BF16)
    qg = _head_block(q_norm[:MLA_NOPE], q_norm[MLA_NOPE:])[None]
    kg = _head_block(k_norm[:MLA_NOPE], jnp.zeros((MLA_ROPE,), F32))[None]
    kpg = _head_block(jnp.zeros((MLA_NOPE,), F32), k_norm[MLA_NOPE:])[None]

    wp_bd = jax.scipy.linalg.block_diag(*[pool_w[g] for g in range(POOL_GROUPS)]).astype(BF16)
    return dict(w_cat=w_cat, alog_l=alog_l, dtb_l=dtb_l, conv_w=dn_conv, dn_norm=dn_norm[None],
                qag=q_a_norm[None], kvag=kv_a_norm[None], wq=wq, wk=wk, wvt=wvt, qg=qg, kg=kg, kpg=kpg,
                wp_bd=wp_bd, pool_scale=pool_scale[None], w_out=w_out.astype(BF16),
                w_up=w_up.astype(BF16), w_down=w_down.astype(BF16))


def kernel(x, positions, attn_norm, w_in, pool_w, pool_scale, dn_conv, dn_a_log, dn_dt_bias, dn_norm,
           mla_q_a_norm, mla_w_q_b, mla_kv_a_norm, mla_w_kv_b, mla_q_norm, mla_k_norm,
           w_out, mlp_norm, w_up, w_down):
    b, s, d = x.shape
    depth = w_in.shape[0]
    inv_freq = ROPE_THETA ** (-jnp.arange(0, MLA_ROPE, 2, dtype=F32) / MLA_ROPE)
    freq_lanes = _head_block(jnp.zeros((MLA_NOPE,), F32), jnp.concatenate([inv_freq, inv_freq]))[None]
    cos, sina, sinb = _rope_tables(positions, freq_lanes)

    for l in range(depth):
        p = _prep_layer(w_in[l], pool_w[l], pool_scale[l], dn_conv[l], dn_a_log[l], dn_dt_bias[l], dn_norm[l],
                        mla_q_a_norm[l], mla_w_q_b[l], mla_kv_a_norm[l], mla_w_kv_b[l], mla_q_norm[l],
                        mla_k_norm[l], w_out[l], w_up[l], w_down[l])
        qkv, xa, z, cq, ckv, misc = _inproj(x, attn_norm[l][None], p["w_cat"], p["conv_w"])
        y_b = _deltanet(qkv, z, misc, p["alog_l"], p["dtb_l"], p["dn_norm"])
        q, k, vt, qsq, ksq = _mla_prep(cq, ckv, misc, cos, sina, sinb, p["qag"], p["kvag"], p["wq"], p["wk"],
                                       p["wvt"], p["qg"], p["kg"], p["kpg"])
        u, flag = _score_bounds(qsq, ksq)
        y_c = _attention(q, k, vt, u, flag[:, 0, 0])
        x = _post(x, xa, y_b, y_c, p["wp_bd"], p["pool_scale"], p["w_out"], mlp_norm[l][None], p["w_up"], p["w_down"])
    return x
```
